```python
import jax, jax.numpy as jnp
from jax import lax
import numpy as np

D_MODEL = 2048
BATCH = 4
SEQ = 2048
DEPTH = 2

CTX_LEN = 256
GRID_W = 64
N_MIXERS = 2
RET_HEADS = 8
RET_QK_DIM = D_MODEL // RET_HEADS
RET_V_DIM = 2 * D_MODEL // RET_HEADS
RET_CHUNK = 128
ATT_HEADS = 16
ATT_KV_HEADS = 4
ATT_HEAD_DIM = D_MODEL // ATT_HEADS
ATT_BLOCK = 128
FFN_DIM = 256 * ((8 * D_MODEL // 3 + 255) // 256)
CONV_WIDTH = 3
ROPE_THETA = 10000.0
EPS = 1e-6

kernel_name = 'hybrid_retention_gqa_dit'


def rms_norm(x, w):
    xf = x.astype(jnp.float32)
    y = xf * lax.rsqrt(jnp.mean(xf * xf, axis=-1, keepdims=True) + EPS)
    return (y * w.astype(jnp.float32)).astype(x.dtype)


def modulate(h, shift, scale):
    return h * (1.0 + scale) + shift


def axial_rope_tables(n, head_dim):
    rows = n // GRID_W
    row = jnp.repeat(jnp.arange(rows, dtype=jnp.float32), GRID_W)
    col = jnp.tile(jnp.arange(GRID_W, dtype=jnp.float32), rows)
    n_freq = head_dim // 4
    inv = ROPE_THETA ** (-jnp.arange(n_freq, dtype=jnp.float32) / n_freq)
    ang = jnp.concatenate([row[:, None] * inv, col[:, None] * inv], axis=-1)
    return jnp.cos(ang), jnp.sin(ang)


def apply_rope(x, cos, sin):
    xf = x.astype(jnp.float32).reshape(*x.shape[:-1], -1, 2)
    x1, x2 = xf[..., 0], xf[..., 1]
    c = cos[None, :, None, :]
    s = sin[None, :, None, :]
    out = jnp.stack([x1 * c - x2 * s, x1 * s + x2 * c], axis=-1).reshape(x.shape)
    return out.astype(x.dtype)


def retention_scan(q, k, v, log_gamma, state0):
    bsz, heads, length, _ = q.shape
    dv = v.shape[-1]
    n_chunks = length // RET_CHUNK
    idx = jnp.arange(RET_CHUNK, dtype=jnp.float32)
    lg = log_gamma[:, None]
    diff = idx[:, None] - idx[None, :]
    lower = diff >= 0
    intra = jnp.where(lower, jnp.exp(lg[:, :, None] * jnp.where(lower, diff, 0.0)), 0.0)
    q_decay = jnp.exp(lg * (idx + 1.0))[:, :, None]
    k_decay = jnp.exp(lg * (RET_CHUNK - 1.0 - idx))[:, :, None]
    chunk_decay = jnp.exp(log_gamma * RET_CHUNK)[:, None, None]

    def to_chunks(t):
        return jnp.moveaxis(t.reshape(bsz, heads, n_chunks, RET_CHUNK, t.shape[-1]), 2, 0)

    def step(state, qkv):
        qc, kc, vc = qkv
        scores = jnp.einsum('bhnd,bhmd->bhnm', qc, kc) * intra
        out = jnp.einsum('bhnm,bhme->bhne', scores, vc) + jnp.einsum('bhnd,bhde->bhne', qc, state) * q_decay
        state = state * chunk_decay + jnp.einsum('bhmd,bhme->bhde', kc * k_decay, vc)
        return state, out

    state, out = lax.scan(step, state0, (to_chunks(q), to_chunks(k), to_chunks(v)))
    out = jnp.moveaxis(out, 0, 2).reshape(bsz, heads, length, dv)
    return out, state


def retention_mixer(hx, hc, w_in, w_out, log_decay, gn_w, cos, sin, need_ctx):
    d = D_MODEL

    def project(h):
        bsz, length = h.shape[:2]
        q, k, v, g = jnp.split(h @ w_in, [d, 2 * d, 4 * d], axis=-1)
        q = q.reshape(bsz, length, RET_HEADS, RET_QK_DIM)
        k = k.reshape(bsz, length, RET_HEADS, RET_QK_DIM) * (RET_QK_DIM ** -0.5)
        v = v.reshape(bsz, length, RET_HEADS, RET_V_DIM)
        return q, k, v, g

    def to_heads(t):
        return jnp.swapaxes(t, 1, 2).astype(jnp.float32)

    def flip(t):
        return t[:, :, ::-1]

    qx, kx, vx, gx = project(hx)
    qc, kc, vc, gc = project(hc)
    qx, kx = apply_rope(qx, cos, sin), apply_rope(kx, cos, sin)
    qx, kx, vx = to_heads(qx), to_heads(kx), to_heads(vx)
    qc, kc, vc = to_heads(qc), to_heads(kc), to_heads(vc)
    log_gamma = -jnp.exp(log_decay.astype(jnp.float32))
    zeros = jnp.zeros((hx.shape[0], RET_HEADS, RET_QK_DIM, RET_V_DIM), jnp.float32)
    oc_f, st_f = retention_scan(qc, kc, vc, log_gamma[0], zeros)
    oc_b, st_b = retention_scan(flip(qc), flip(kc), flip(vc), log_gamma[1], zeros)
    ox_f, _ = retention_scan(qx, kx, vx, log_gamma[0], st_f)
    ox_b, _ = retention_scan(flip(qx), flip(kx), flip(vx), log_gamma[1], st_b)

    def finish(y, g, dtype):
        bsz, _, length, _ = y.shape
        mu = jnp.mean(y, axis=-1, keepdims=True)
        var = jnp.mean(jnp.square(y - mu), axis=-1, keepdims=True)
        y = (y - mu) * lax.rsqrt(var + EPS)
        y = jnp.swapaxes(y, 1, 2).reshape(bsz, length, RET_HEADS * RET_V_DIM) * gn_w.astype(jnp.float32)
        return (jax.nn.silu(g.astype(jnp.float32)) * y).astype(dtype) @ w_out

    out_x = finish(ox_f + flip(ox_b), gx, hx.dtype)
    out_c = finish(oc_f + flip(oc_b), gc, hc.dtype) if need_ctx else None
    return out_x, out_c


def attend(q, k, v):
    bsz, lq = q.shape[:2]
    groups = ATT_HEADS // ATT_KV_HEADS
    n_blocks = lq // ATT_BLOCK
    qb = jnp.moveaxis(q.reshape(bsz, n_blocks, ATT_BLOCK, ATT_KV_HEADS, groups, ATT_HEAD_DIM), 1, 0)
    scale = ATT_HEAD_DIM ** -0.5

    def block(qi):
        s = jnp.einsum('bqkgd,btkd->bkgqt', qi, k).astype(jnp.float32) * scale
        p = jax.nn.softmax(s, axis=-1).astype(v.dtype)
        return jnp.einsum('bkgqt,btkd->bqkgd', p, v)

    o = lax.map(block, qb)
    return jnp.moveaxis(o, 0, 1).reshape(bsz, lq, ATT_HEADS * ATT_HEAD_DIM)


def gqa_mixer(hx, hc, w_in, w_out, q_norm, k_norm, cos, sin, need_ctx):
    q_w = ATT_HEADS * ATT_HEAD_DIM
    kv_w = ATT_KV_HEADS * ATT_HEAD_DIM

    def project(h):
        bsz, length = h.shape[:2]
        q, k, v = jnp.split(h @ w_in, [q_w, q_w + kv_w], axis=-1)
        q = rms_norm(q.reshape(bsz, length, ATT_HEADS, ATT_HEAD_DIM), q_norm)
        k = rms_norm(k.reshape(bsz, length, ATT_KV_HEADS, ATT_HEAD_DIM), k_norm)
        v = v.reshape(bsz, length, ATT_KV_HEADS, ATT_HEAD_DIM)
        return q, k, v

    qx, kx, vx = project(hx)
    qc, kc, vc = project(hc)
    qx, kx = apply_rope(qx, cos, sin), apply_rope(kx, cos, sin)
    k_all = jnp.concatenate([kc, kx], axis=1)
    v_all = jnp.concatenate([vc, vx], axis=1)
    out_x = attend(qx, k_all, v_all) @ w_out
    out_c = attend(qc, kc, vc) @ w_out if need_ctx else None
    return out_x, out_c


def conv_ffn(h, w_up, conv_w, conv_b, w_down):
    length = h.shape[1]
    u = h @ w_up
    pad = CONV_WIDTH // 2
    up = jnp.pad(u, ((0, 0), (pad, pad), (0, 0)))
    u = sum(up[:, j:j + length] * conv_w[j] for j in range(CONV_WIDTH)) + conv_b
    a, b = jnp.split(u, 2, axis=-1)
    return (jax.nn.silu(a) * b) @ w_down


def setup_inputs(seed: int = 0) -> dict:
    key = jax.random.key(seed)
    ks = jax.random.split(key, 24)
    d = D_MODEL
    n_ret = (DEPTH + 1) // 2
    n_att = DEPTH // 2
    f32 = jnp.float32

    def nrm(k, shape, scale):
        return jax.random.normal(k, shape, f32) * scale

    decay_rate = jnp.exp2(-5.0 - jnp.arange(RET_HEADS, dtype=f32))
    base_log_decay = jnp.log(-jnp.log1p(-decay_rate))
    att_in = (ATT_HEADS + 2 * ATT_KV_HEADS) * ATT_HEAD_DIM
    return {
        'x': nrm(ks[0], (BATCH, SEQ, d), 1.0),
        'c': nrm(ks[1], (BATCH, d), 1.0),
        'ctx': nrm(ks[2], (BATCH, CTX_LEN, d), 1.0),
        'c_ctx': nrm(ks[3], (d,), 1.0),
        'ada_w': nrm(ks[4], (DEPTH, d, 6 * d), d ** -0.5),
        'ada_b': nrm(ks[5], (DEPTH, 6 * d), 0.02),
        'norm_w': 1.0 + nrm(ks[6], (DEPTH, 2, d), 0.02),
        'ret_w_in': nrm(ks[7], (n_ret, d, 6 * d), d ** -0.5),
        'ret_w_out': nrm(ks[8], (n_ret, 2 * d, d), (2 * d) ** -0.5),
        'ret_log_decay': base_log_decay[None, None, :] + nrm(ks[9], (n_ret, 2, RET_HEADS), 0.1),
        'ret_gn_w': 1.0 + nrm(ks[10], (n_ret, 2 * d), 0.02),
        'attn_w_in': nrm(ks[11], (n_att, d, att_in), d ** -0.5),
        'attn_w_out': nrm(ks[12], (n_att, d, d), d ** -0.5),
        'attn_q_norm': 1.0 + nrm(ks[13], (n_att, ATT_HEAD_DIM), 0.02),
        'attn_k_norm': 1.0 + nrm(ks[14], (n_att, ATT_HEAD_DIM), 0.02),
        'ffn_w_up': nrm(ks[15], (DEPTH, d, 2 * FFN_DIM), d ** -0.5),
        'ffn_conv_w': nrm(ks[16], (DEPTH, CONV_WIDTH, 2 * FFN_DIM), CONV_WIDTH ** -0.5),
        'ffn_conv_b': nrm(ks[17], (DEPTH, 2 * FFN_DIM), 0.02),
        'ffn_w_down': nrm(ks[18], (DEPTH, FFN_DIM, d), FFN_DIM ** -0.5),
        'final_norm_w': 1.0 + nrm(ks[19], (d,), 0.02),
    }


def reference(x, c, ctx, c_ctx, ada_w, ada_b, norm_w, ret_w_in, ret_w_out, ret_log_decay, ret_gn_w,
              attn_w_in, attn_w_out, attn_q_norm, attn_k_norm, ffn_w_up, ffn_conv_w, ffn_conv_b,
              ffn_w_down, final_norm_w):
    n_tok = x.shape[1]
    cos_r, sin_r = axial_rope_tables(n_tok, RET_QK_DIM)
    cos_a, sin_a = axial_rope_tables(n_tok, ATT_HEAD_DIM)
    c_act = jax.nn.silu(c)
    cc_act = jax.nn.silu(c_ctx)
    for i in range(DEPTH):
        last = i == DEPTH - 1
        j = i // N_MIXERS
        mod_x = (c_act @ ada_w[i] + ada_b[i])[:, None, :]
        mod_c = cc_act @ ada_w[i] + ada_b[i]
        sh1x, sc1x, g1x, sh2x, sc2x, g2x = jnp.split(mod_x, 6, axis=-1)
        sh1c, sc1c, g1c, sh2c, sc2c, g2c = jnp.split(mod_c, 6, axis=-1)
        hx = modulate(rms_norm(x, norm_w[i, 0]), sh1x, sc1x)
        hc = modulate(rms_norm(ctx, norm_w[i, 0]), sh1c, sc1c)
        if i % N_MIXERS == 0:
            ox, oc = retention_mixer(hx, hc, ret_w_in[j], ret_w_out[j], ret_log_decay[j], ret_gn_w[j],
                                     cos_r, sin_r, not last)
        else:
            ox, oc = gqa_mixer(hx, hc, attn_w_in[j], attn_w_out[j], attn_q_norm[j], attn_k_norm[j],
                               cos_a, sin_a, not last)
        x = x + g1x * ox
        hx = modulate(rms_norm(x, norm_w[i, 1]), sh2x, sc2x)
        x = x + g2x * conv_ffn(hx, ffn_w_up[i], ffn_conv_w[i], ffn_conv_b[i], ffn_w_down[i])
        if not last:
            ctx = ctx + g1c * oc
            hc = modulate(rms_norm(ctx, norm_w[i, 1]), sh2c, sc2c)
            ctx = ctx + g2c * conv_ffn(hc, ffn_w_up[i], ffn_conv_w[i], ffn_conv_b[i], ffn_w_down[i])
    return rms_norm(x, final_norm_w)
```

```python
import functools

import jax
import jax.numpy as jnp
from jax import lax
from jax.experimental import pallas as pl
from jax.experimental.pallas import tpu as pltpu

D = 2048
B = 4
SEQ = 2048
CTX = 256
GRID_W = 64
RET_HEADS = 8
RET_DK = D // RET_HEADS
RET_DV = 2 * D // RET_HEADS
CHUNK = 128
ATT_HEADS = 16
ATT_KV = 4
ATT_HD = D // ATT_HEADS
ATT_GROUP = ATT_HEADS // ATT_KV
FFN = 256 * ((8 * D // 3 + 255) // 256)
ROPE_THETA = 10000.0
EPS = 1e-6

NX = B * SEQ
NC = B * CTX
NT = NX + NC
CTX_ROW = B
MOD_ROWS = 8

F32 = jnp.float32
BF16 = jnp.bfloat16

VMEM_LIMIT = 56 * 1024 * 1024

NT_DIMS = (((1,), (1,)), ((), ()))
TN_DIMS = (((0,), (0,)), ((), ()))


def _params(*sem):
    return pltpu.CompilerParams(dimension_semantics=sem, vmem_limit_bytes=VMEM_LIMIT)


def _silu(v):
    return v * (1.0 / (1.0 + jnp.exp(-v)))


def _rms_mod(x, nw, sh, sc):
    ms = jnp.mean(x * x, axis=-1, keepdims=True)
    y = x * lax.rsqrt(ms + EPS) * nw
    return y * (1.0 + sc) + sh


def _mod_row(i, tm):
    return jnp.where(i < NX // tm, i // (SEQ // tm), CTX_ROW)


def _mod_spec(tm, chunk, width=D, ntile=None):
    per = D // width
    if ntile is None:
        return pl.BlockSpec((None, 1, width), lambda i, n: (_mod_row(i, tm), 0, chunk * per))
    return pl.BlockSpec((None, 1, width), lambda i, n: (_mod_row(i, tm), 0, chunk * per + n))


def _rope(seg, cos, sa, sb):
    hd = seg.shape[-1]
    return seg * cos + pltpu.roll(seg, hd - 1, 1) * sa + pltpu.roll(seg, 1, 1) * sb


MOD_TN = 1024


def _mod_kernel(c_ref, w_ref, b_ref, o_ref):
    a = _silu(c_ref[...]).astype(BF16)
    o_ref[...] = jnp.dot(a, w_ref[...].astype(BF16), preferred_element_type=F32) + b_ref[...]


def _modulation(cmat, ada_w, ada_b):
    depth = ada_w.shape[0]
    return pl.pallas_call(
        _mod_kernel,
        grid=(depth, 6 * D // MOD_TN),
        in_specs=[
            pl.BlockSpec((MOD_ROWS, D), lambda l, n: (0, 0)),
            pl.BlockSpec((None, D, MOD_TN), lambda l, n: (l, 0, n)),
            pl.BlockSpec((None, 1, MOD_TN), lambda l, n: (l, 0, n)),
        ],
        out_specs=pl.BlockSpec((None, MOD_ROWS, MOD_TN), lambda l, n: (l, 0, n)),
        out_shape=jax.ShapeDtypeStruct((depth, MOD_ROWS, 6 * D), F32),
        compiler_params=_params("arbitrary", "arbitrary"),
        name="adaln_mod",
    )(cmat, ada_w, ada_b.reshape(depth, 1, 6 * D))


RIN_TM = 1024
RIN_TN = 512


def _ret_in_kernel(x_ref, nw_ref, sh_ref, sc_ref, w_ref, cos_ref, sa_ref, sb_ref, o_ref, h_ref):
    i = pl.program_id(0)
    n = pl.program_id(1)

    @pl.when(n == 0)
    def _():
        h_ref[...] = _rms_mod(x_ref[...], nw_ref[...], sh_ref[...], sc_ref[...]).astype(BF16)

    acc = jnp.dot(h_ref[...], w_ref[...], preferred_element_type=F32)
    qk_tiles = 2 * D // RIN_TN
    is_qk = n < qk_tiles
    is_x = i < NX // RIN_TM
    kscale = jnp.where(n >= D // RIN_TN, RET_DK ** -0.5, 1.0).astype(F32)

    @pl.when(jnp.logical_and(is_qk, is_x))
    def _():
        cos = cos_ref[...]
        sa = sa_ref[...]
        sb = sb_ref[...]
        for s in range(RIN_TN // RET_DK):
            seg = acc[:, s * RET_DK:(s + 1) * RET_DK] * kscale
            o_ref[:, s * RET_DK:(s + 1) * RET_DK] = _rope(seg, cos, sa, sb).astype(BF16)

    @pl.when(jnp.logical_and(is_qk, jnp.logical_not(is_x)))
    def _():
        o_ref[...] = (acc * kscale).astype(BF16)

    @pl.when(jnp.logical_not(is_qk))
    def _():
        o_ref[...] = acc.astype(BF16)


def _ret_in(xs, nw, mod, w, tabs):
    tm, tn = RIN_TM, RIN_TN
    tps = SEQ // tm
    tab_spec = pl.BlockSpec((tm, RET_DK), lambda i, n: (jnp.where(i < NX // tm, i % tps, 0), 0))
    return pl.pallas_call(
        _ret_in_kernel,
        grid=(NT // tm, 6 * D // tn),
        in_specs=[
            pl.BlockSpec((tm, D), lambda i, n: (i, 0)),
            pl.BlockSpec((1, D), lambda i, n: (0, 0)),
            _mod_spec(tm, 0),
            _mod_spec(tm, 1),
            pl.BlockSpec((D, tn), lambda i, n: (0, n)),
            tab_spec, tab_spec, tab_spec,
        ],
        out_specs=pl.BlockSpec((tm, tn), lambda i, n: (i, n)),
        out_shape=jax.ShapeDtypeStruct((NT, 6 * D), BF16),
        scratch_shapes=[pltpu.VMEM((tm, D), BF16)],
        compiler_params=_params("arbitrary", "arbitrary"),
        name="ret_in_proj",
    )(xs, nw, mod, mod, w, *tabs)


def _ret_scan_kernel(ld_ref, qx_ref, kx_ref, vx_ref, gx_ref, qc_ref, kc_ref, vc_ref, gc_ref, gnw_ref,
                     yx_ref, yc_ref, ob_ref, st_ref):
    h = pl.program_id(1)
    c = CHUNK
    nn = lax.broadcasted_iota(jnp.int32, (c, c), 0).astype(F32)
    mm = lax.broadcasted_iota(jnp.int32, (c, c), 1).astype(F32)
    idx = lax.broadcasted_iota(jnp.int32, (c, 1), 0).astype(F32)

    def log_gamma(direction, shape):
        return -jnp.exp(jnp.full(shape, ld_ref[direction, h], F32))

    lgf, lgb = log_gamma(0, (c, c)), log_gamma(1, (c, c))
    lgf1, lgb1 = log_gamma(0, (c, 1)), log_gamma(1, (c, 1))
    lgf0, lgb0 = log_gamma(0, (1, 1)), log_gamma(1, (1, 1))
    decay_f = jnp.where(nn >= mm, jnp.exp(lgf * jnp.where(nn >= mm, nn - mm, 0.0)), 0.0)
    decay_b = jnp.where(mm >= nn, jnp.exp(lgb * jnp.where(mm >= nn, mm - nn, 0.0)), 0.0)
    qd_f = jnp.exp(lgf1 * (idx + 1.0))
    kd_f = jnp.exp(lgf1 * (c - 1.0 - idx))
    qd_b = jnp.exp(lgb1 * (c - idx))
    kd_b = jnp.exp(lgb1 * idx)
    cd_f = jnp.exp(lgf0 * c)
    cd_b = jnp.exp(lgb0 * c)

    def chunk_step(q, k, v, decay, qd, kd, cd):
        state = st_ref[...]
        scores = lax.dot_general(q, k, NT_DIMS, preferred_element_type=F32) * decay
        out = jnp.dot(scores.astype(BF16), v, preferred_element_type=F32)
        out = out + jnp.dot(q, state.astype(BF16), preferred_element_type=F32) * qd
        kdec = (k.astype(F32) * kd).astype(BF16)
        st_ref[...] = state * cd + lax.dot_general(kdec, v, TN_DIMS, preferred_element_type=F32)
        return out

    def backward(q_ref, k_ref, v_ref, ob_base, n_chunks):
        def body(j, carry):
            r = pl.multiple_of((n_chunks - 1 - j) * c, c)
            rows = pl.ds(r, c)
            out = chunk_step(q_ref[rows, :], k_ref[rows, :], v_ref[rows, :], decay_b, qd_b, kd_b, cd_b)
            ob_ref[pl.ds(pl.multiple_of(ob_base + r, c), c), :] = out
            return carry
        lax.fori_loop(0, n_chunks, body, 0)

    def forward(q_ref, k_ref, v_ref, g_ref, y_ref, ob_base, n_chunks):
        gnw = gnw_ref[...]

        def body(j, carry):
            r = pl.multiple_of(j * c, c)
            rows = pl.ds(r, c)
            out = chunk_step(q_ref[rows, :], k_ref[rows, :], v_ref[rows, :], decay_f, qd_f, kd_f, cd_f)
            tot = out + ob_ref[pl.ds(pl.multiple_of(ob_base + r, c), c), :]
            mu = jnp.mean(tot, axis=-1, keepdims=True)
            cen = tot - mu
            var = jnp.mean(cen * cen, axis=-1, keepdims=True)
            yn = cen * lax.rsqrt(var + EPS) * gnw
            y_ref[rows, :] = (_silu(g_ref[rows, :].astype(F32)) * yn).astype(BF16)
            return carry
        lax.fori_loop(0, n_chunks, body, 0)

    st_ref[...] = jnp.zeros_like(st_ref)
    backward(qc_ref, kc_ref, vc_ref, SEQ, CTX // c)
    backward(qx_ref, kx_ref, vx_ref, 0, SEQ // c)
    st_ref[...] = jnp.zeros_like(st_ref)
    forward(qc_ref, kc_ref, vc_ref, gc_ref, yc_ref, SEQ, CTX // c)
    forward(qx_ref, kx_ref, vx_ref, gx_ref, yx_ref, 0, SEQ // c)


def _ret_scan(qkvg, log_decay, gn_w):
    hq = D // RET_DK
    hv = 2 * D // RET_DV
    ctx0 = NX // CTX
    x_specs = [
        pl.BlockSpec((SEQ, RET_DK), lambda b, h: (b, h)),
        pl.BlockSpec((SEQ, RET_DK), lambda b, h: (b, hq + h)),
        pl.BlockSpec((SEQ, RET_DV), lambda b, h: (b, hv + h)),
        pl.BlockSpec((SEQ, RET_DV), lambda b, h: (b, 2 * hv + h)),
    ]
    c_specs = [
        pl.BlockSpec((CTX, RET_DK), lambda b, h: (ctx0 + b, h)),
        pl.BlockSpec((CTX, RET_DK), lambda b, h: (ctx0 + b, hq + h)),
        pl.BlockSpec((CTX, RET_DV), lambda b, h: (ctx0 + b, hv + h)),
        pl.BlockSpec((CTX, RET_DV), lambda b, h: (ctx0 + b, 2 * hv + h)),
    ]
    return pl.pallas_call(
        _ret_scan_kernel,
        grid=(B, RET_HEADS),
        in_specs=[pl.BlockSpec(memory_space=pltpu.SMEM)] + x_specs + c_specs
                 + [pl.BlockSpec((1, RET_DV), lambda b, h: (0, h))],
        out_specs=[
            pl.BlockSpec((SEQ, RET_DV), lambda b, h: (b, h)),
            pl.BlockSpec((CTX, RET_DV), lambda b, h: (b, h)),
        ],
        out_shape=[jax.ShapeDtypeStruct((NX, 2 * D), BF16), jax.ShapeDtypeStruct((NC, 2 * D), BF16)],
        scratch_shapes=[pltpu.VMEM((SEQ + CTX, RET_DV), F32), pltpu.VMEM((RET_DK, RET_DV), F32)],
        compiler_params=_params("arbitrary", "arbitrary"),
        name="ret_scan",
    )(log_decay, qkvg, qkvg, qkvg, qkvg, qkvg, qkvg, qkvg, qkvg, gn_w)


PROJ_TM = 1024
PROJ_TN = 512


def _proj_res_kernel(y_ref, w_ref, x_ref, g_ref, o_ref):
    acc = jnp.dot(y_ref[...], w_ref[...], preferred_element_type=F32)
    o_ref[...] = x_ref[...] + g_ref[...] * acc


def _proj_res(y, w, xs, mod, n_rows):
    tm, tn = PROJ_TM, PROJ_TN
    ky = y.shape[1]
    return pl.pallas_call(
        _proj_res_kernel,
        grid=(n_rows // tm, D // tn),
        in_specs=[
            pl.BlockSpec((tm, ky), lambda i, n: (i, 0)),
            pl.BlockSpec((ky, tn), lambda i, n: (0, n)),
            pl.BlockSpec((tm, tn), lambda i, n: (i, n)),
            _mod_spec(tm, 2, width=tn, ntile=True),
        ],
        out_specs=pl.BlockSpec((tm, tn), lambda i, n: (i, n)),
        out_shape=jax.ShapeDtypeStruct((n_rows, D), F32),
        compiler_params=_params("arbitrary", "arbitrary"),
        name="proj_residual",
    )(y, w, xs, mod)


FFN_TM = 512
FFN_TF = 512
HALO = 16


def _ffn_kernel(xp_ref, x_ref, xn_ref, nw_ref, sh_ref, sc_ref, g_ref, wa_ref, wb_ref,
                cwa_ref, cwb_ref, cba_ref, cbb_ref, wd_ref, fnw_ref, o_ref,
                h_ref, ua_ref, ub_ref, *, final_norm):
    tm = FFN_TM
    i = pl.program_id(0)
    f = pl.program_id(1)

    @pl.when(f == 0)
    def _():
        nw, sh, sc = nw_ref[...], sh_ref[...], sc_ref[...]
        h_ref[0:HALO, :] = _rms_mod(xp_ref[...], nw, sh, sc).astype(BF16)
        h_ref[HALO:HALO + tm, :] = _rms_mod(x_ref[...], nw, sh, sc).astype(BF16)
        h_ref[HALO + tm:, :] = _rms_mod(xn_ref[...], nw, sh, sc).astype(BF16)
        o_ref[...] = jnp.zeros_like(o_ref)

    hh = h_ref[...]
    ua_ref[...] = jnp.dot(hh, wa_ref[...], preferred_element_type=F32)
    ub_ref[...] = jnp.dot(hh, wb_ref[...], preferred_element_type=F32)

    t = lax.broadcasted_iota(jnp.int32, (tm, 1), 0)
    is_ctx = i >= NX // tm
    pos = jnp.where(is_ctx, t % CTX, (i % (SEQ // tm)) * tm + t)
    last = jnp.where(is_ctx, CTX - 1, SEQ - 1)
    has_prev = pos > 0
    has_next = pos < last

    def conv(u_ref, cw_ref, cb_ref):
        prev = jnp.where(has_prev, u_ref[pl.ds(HALO - 1, tm), :], 0.0)
        cur = u_ref[pl.ds(HALO, tm), :]
        nxt = jnp.where(has_next, u_ref[pl.ds(HALO + 1, tm), :], 0.0)
        return prev * cw_ref[0:1, :] + cur * cw_ref[1:2, :] + nxt * cw_ref[2:3, :] + cb_ref[...]

    a = conv(ua_ref, cwa_ref, cba_ref)
    b = conv(ub_ref, cwb_ref, cbb_ref)
    act = (_silu(a) * b).astype(BF16)
    o_ref[...] += jnp.dot(act, wd_ref[...], preferred_element_type=F32)

    @pl.when(f == pl.num_programs(1) - 1)
    def _():
        res = x_ref[...] + g_ref[...] * o_ref[...]
        if final_norm:
            ms = jnp.mean(res * res, axis=-1, keepdims=True)
            res = res * lax.rsqrt(ms + EPS) * fnw_ref[...]
        o_ref[...] = res


def _conv_ffn(xs, nw, mod, w_up, conv_w, conv_b, w_down, fnw, n_rows, final_norm):
    tm, tf = FFN_TM, FFN_TF
    nf = FFN // tf
    hb = tm // HALO
    last_hb = xs.shape[0] // HALO - 1
    return pl.pallas_call(
        functools.partial(_ffn_kernel, final_norm=final_norm),
        grid=(n_rows // tm, nf),
        in_specs=[
            pl.BlockSpec((HALO, D), lambda i, f: (jnp.maximum(i * hb - 1, 0), 0)),
            pl.BlockSpec((tm, D), lambda i, f: (i, 0)),
            pl.BlockSpec((HALO, D), lambda i, f: (jnp.minimum((i + 1) * hb, last_hb), 0)),
            pl.BlockSpec((1, D), lambda i, f: (0, 0)),
            _mod_spec(tm, 3),
            _mod_spec(tm, 4),
            _mod_spec(tm, 5),
            pl.BlockSpec((D, tf), lambda i, f: (0, f)),
            pl.BlockSpec((D, tf), lambda i, f: (0, nf + f)),
            pl.BlockSpec((3, tf), lambda i, f: (0, f)),
            pl.BlockSpec((3, tf), lambda i, f: (0, nf + f)),
            pl.BlockSpec((1, tf), lambda i, f: (0, f)),
            pl.BlockSpec((1, tf), lambda i, f: (0, nf + f)),
            pl.BlockSpec((tf, D), lambda i, f: (f, 0)),
            pl.BlockSpec((1, D), lambda i, f: (0, 0)),
        ],
        out_specs=pl.BlockSpec((tm, D), lambda i, f: (i, 0)),
        out_shape=jax.ShapeDtypeStruct((n_rows, D), F32),
        scratch_shapes=[
            pltpu.VMEM((tm + 2 * HALO, D), BF16),
            pltpu.VMEM((tm + 2 * HALO, tf), F32),
            pltpu.VMEM((tm + 2 * HALO, tf), F32),
        ],
        compiler_params=_params("arbitrary", "arbitrary"),
        name="conv_ffn",
    )(xs, xs, xs, nw, mod, mod, mod, w_up, w_up, conv_w, conv_w, conv_b, conv_b, w_down, fnw)


AIN_TM = 1024
AIN_TN = 512
ATT_IN = (ATT_HEADS + 2 * ATT_KV) * ATT_HD


def _attn_in_kernel(x_ref, nw_ref, sh_ref, sc_ref, w_ref, qn_ref, kn_ref, cos_ref, sa_ref, sb_ref,
                    o_ref, h_ref):
    i = pl.program_id(0)
    n = pl.program_id(1)

    @pl.when(n == 0)
    def _():
        h_ref[...] = _rms_mod(x_ref[...], nw_ref[...], sh_ref[...], sc_ref[...]).astype(BF16)

    acc = jnp.dot(h_ref[...], w_ref[...], preferred_element_type=F32)
    q_tiles = D // AIN_TN
    is_q = n < q_tiles
    is_qk = n <= q_tiles
    is_x = i < NX // AIN_TM
    hw = jnp.where(is_q, qn_ref[...], kn_ref[...])
    qscale = jnp.where(is_q, ATT_HD ** -0.5, 1.0).astype(F32)

    def normed(s):
        seg = acc[:, s * ATT_HD:(s + 1) * ATT_HD]
        ms = jnp.mean(seg * seg, axis=-1, keepdims=True)
        return seg * lax.rsqrt(ms + EPS) * hw

    @pl.when(jnp.logical_and(is_qk, is_x))
    def _():
        cos = cos_ref[...]
        sa = sa_ref[...]
        sb = sb_ref[...]
        for s in range(AIN_TN // ATT_HD):
            o_ref[:, s * ATT_HD:(s + 1) * ATT_HD] = (_rope(normed(s), cos, sa, sb) * qscale).astype(BF16)

    @pl.when(jnp.logical_and(is_qk, jnp.logical_not(is_x)))
    def _():
        for s in range(AIN_TN // ATT_HD):
            o_ref[:, s * ATT_HD:(s + 1) * ATT_HD] = (normed(s) * qscale).astype(BF16)

    @pl.when(jnp.logical_not(is_qk))
    def _():
        o_ref[...] = acc.astype(BF16)


def _attn_in(xs, nw, mod, w, qn, kn, tabs):
    tm, tn = AIN_TM, AIN_TN
    tps = SEQ // tm
    tab_spec = pl.BlockSpec((tm, ATT_HD), lambda i, n: (jnp.where(i < NX // tm, i % tps, 0), 0))
    vec_spec = pl.BlockSpec((1, ATT_HD), lambda i, n: (0, 0))
    return pl.pallas_call(
        _attn_in_kernel,
        grid=(NT // tm, ATT_IN // tn),
        in_specs=[
            pl.BlockSpec((tm, D), lambda i, n: (i, 0)),
            pl.BlockSpec((1, D), lambda i, n: (0, 0)),
            _mod_spec(tm, 0),
            _mod_spec(tm, 1),
            pl.BlockSpec((D, tn), lambda i, n: (0, n)),
            vec_spec, vec_spec,
            tab_spec, tab_spec, tab_spec,
        ],
        out_specs=pl.BlockSpec((tm, tn), lambda i, n: (i, n)),
        out_shape=jax.ShapeDtypeStruct((NT, ATT_IN), BF16),
        scratch_shapes=[pltpu.VMEM((tm, D), BF16)],
        compiler_params=_params("arbitrary", "arbitrary"),
        name="attn_in_proj",
    )(xs, nw, mod, mod, w, qn, kn, *tabs)


ATT_TQ = 128


def _attn_kernel(q_ref, kx_ref, kc_ref, vx_ref, vc_ref, o_ref):
    tq = ATT_TQ
    q = q_ref[...]
    qs = jnp.concatenate([q[:, j * ATT_HD:(j + 1) * ATT_HD] for j in range(ATT_GROUP)], axis=0)
    sx = lax.dot_general(qs, kx_ref[...], NT_DIMS, preferred_element_type=F32)
    sc = lax.dot_general(qs, kc_ref[...], NT_DIMS, preferred_element_type=F32)
    m = jnp.maximum(jnp.max(sx, axis=-1, keepdims=True), jnp.max(sc, axis=-1, keepdims=True))
    px = jnp.exp(sx - m)
    pc = jnp.exp(sc - m)
    denom = jnp.sum(px, axis=-1, keepdims=True) + jnp.sum(pc, axis=-1, keepdims=True)
    out = jnp.dot(px.astype(BF16), vx_ref[...], preferred_element_type=F32)
    out = out + jnp.dot(pc.astype(BF16), vc_ref[...], preferred_element_type=F32)
    out = out * (1.0 / denom)
    for j in range(ATT_GROUP):
        o_ref[:, j * ATT_HD:(j + 1) * ATT_HD] = out[j * tq:(j + 1) * tq, :].astype(BF16)


def _attention(qkv):
    tq = ATT_TQ
    qb = SEQ // tq
    k0 = D // ATT_HD
    v0 = k0 + ATT_KV
    ctx0 = NX // CTX
    return pl.pallas_call(
        _attn_kernel,
        grid=(B, ATT_KV, qb),
        in_specs=[
            pl.BlockSpec((tq, ATT_GROUP * ATT_HD), lambda b, g, t: (b * qb + t, g)),
            pl.BlockSpec((SEQ, ATT_HD), lambda b, g, t: (b, k0 + g)),
            pl.BlockSpec((CTX, ATT_HD), lambda b, g, t: (ctx0 + b, k0 + g)),
            pl.BlockSpec((SEQ, ATT_HD), lambda b, g, t: (b, v0 + g)),
            pl.BlockSpec((CTX, ATT_HD), lambda b, g, t: (ctx0 + b, v0 + g)),
        ],
        out_specs=pl.BlockSpec((tq, ATT_GROUP * ATT_HD), lambda b, g, t: (b * qb + t, g)),
        out_shape=jax.ShapeDtypeStruct((NX, D), BF16),
        compiler_params=_params("arbitrary", "arbitrary", "arbitrary"),
        name="gqa_attention",
    )(qkv, qkv, qkv, qkv, qkv)


def _rope_tables(head_dim):
    rows = SEQ // GRID_W
    row = jnp.repeat(jnp.arange(rows, dtype=F32), GRID_W)
    col = jnp.tile(jnp.arange(GRID_W, dtype=F32), rows)
    n_freq = head_dim // 4
    inv = ROPE_THETA ** (-jnp.arange(n_freq, dtype=F32) / n_freq)
    ang = jnp.concatenate([row[:, None] * inv, col[:, None] * inv], axis=-1)
    cos = jnp.repeat(jnp.cos(ang), 2, axis=-1)
    sin = jnp.repeat(jnp.sin(ang), 2, axis=-1)
    even = (jnp.arange(head_dim) % 2) == 0
    return cos, jnp.where(even, -sin, 0.0), jnp.where(even, 0.0, sin)


def kernel(x, c, ctx, c_ctx, ada_w, ada_b, norm_w, ret_w_in, ret_w_out, ret_log_decay, ret_gn_w,
           attn_w_in, attn_w_out, attn_q_norm, attn_k_norm, ffn_w_up, ffn_conv_w, ffn_conv_b,
           ffn_w_down, final_norm_w):
    xs = jnp.concatenate([x.reshape(NX, D), ctx.reshape(NC, D)], axis=0)
    cmat = jnp.concatenate([c, c_ctx[None, :], jnp.zeros((MOD_ROWS - B - 1, D), F32)], axis=0)
    mod = _modulation(cmat, ada_w, ada_b)
    mod0 = mod[0].reshape(MOD_ROWS, 1, 6 * D)
    mod1 = mod[1].reshape(MOD_ROWS, 1, 6 * D)
    fnw = final_norm_w.reshape(1, D)

    qkvg = _ret_in(xs, norm_w[0, 0].reshape(1, D), mod0, ret_w_in[0].astype(BF16), _rope_tables(RET_DK))
    yx, yc = _ret_scan(qkvg, ret_log_decay[0], ret_gn_w[0].reshape(1, 2 * D))
    y = jnp.concatenate([yx, yc], axis=0)
    xs = _proj_res(y, ret_w_out[0].astype(BF16), xs, mod0, NT)
    xs = _conv_ffn(xs, norm_w[0, 1].reshape(1, D), mod0, ffn_w_up[0].astype(BF16), ffn_conv_w[0],
                   ffn_conv_b[0].reshape(1, 2 * FFN), ffn_w_down[0].astype(BF16), fnw, NT, False)

    qkv = _attn_in(xs, norm_w[1, 0].reshape(1, D), mod1, attn_w_in[0].astype(BF16),
                   attn_q_norm[0].reshape(1, ATT_HD), attn_k_norm[0].reshape(1, ATT_HD),
                   _rope_tables(ATT_HD))
    ya = _attention(qkv)
    xl = _proj_res(ya, attn_w_out[0].astype(BF16), xs, mod1, NX)
    out = _conv_ffn(xl, norm_w[1, 1].reshape(1, D), mod1, ffn_w_up[1].astype(BF16), ffn_conv_w[1],
                    ffn_conv_b[1].reshape(1, 2 * FFN), ffn_w_down[1].astype(BF16), fnw, NX, True)
    return out.reshape(B, SEQ, D)
```

```python
import functools
import math

import jax
import jax.numpy as jnp
from jax import lax
from jax.experimental import pallas as pl
from jax.experimental.pallas import tpu as pltpu

D = 2048
B = 4
SEQ = 2048
CTX = 256
GRID_W = 64
RET_HEADS = 8
RET_DK = D // RET_HEADS
RET_DV = 2 * D // RET_HEADS
CHUNK = 256
ATT_HEADS = 16
ATT_KV = 4
ATT_HD = D // ATT_HEADS
ATT_GROUP = ATT_HEADS // ATT_KV
ATT_IN = (ATT_HEADS + 2 * ATT_KV) * ATT_HD
FFN = 256 * ((8 * D // 3 + 255) // 256)
ROPE_THETA = 10000.0
EPS = 1e-6

NX = B * SEQ
NC = B * CTX
NT = NX + NC
CTX_ROW = B
MOD_ROWS = 8

F32 = jnp.float32
BF16 = jnp.bfloat16

V7X_VMEM_BYTES = 64 * 1024 * 1024
VMEM_LIMIT = V7X_VMEM_BYTES - 8 * 1024 * 1024
VMEM_LIMIT_FFN = V7X_VMEM_BYTES - 4 * 1024 * 1024

NT_DIMS = (((1,), (1,)), ((), ()))
TN_DIMS = (((0,), (0,)), ((), ()))


def _params(*sem, vmem=VMEM_LIMIT):
    return pltpu.CompilerParams(dimension_semantics=sem, vmem_limit_bytes=vmem)


def _silu(v):
    return v * (1.0 / (1.0 + jnp.exp(-v)))


def _rms_mod(x, nw, sh, sc):
    ms = jnp.mean(x * x, axis=-1, keepdims=True)
    y = x * lax.rsqrt(ms + EPS) * nw
    return y * (1.0 + sc) + sh


def _mod_row(i, tm):
    return jnp.where(i < NX // tm, i // (SEQ // tm), CTX_ROW)


def _mod_spec(tm, chunk):
    return pl.BlockSpec((None, 1, D), lambda i, n: (_mod_row(i, tm), 0, chunk))


def _rope(seg, cos, sa, sb):
    hd = seg.shape[-1]
    return seg * cos + pltpu.roll(seg, hd - 1, 1) * sa + pltpu.roll(seg, 1, 1) * sb


MOD_TN = 1024


def _mod_kernel(c_ref, w_ref, b_ref, o_ref):
    a = _silu(c_ref[...]).astype(BF16)
    o_ref[...] = jnp.dot(a, w_ref[...].astype(BF16), preferred_element_type=F32) + b_ref[...]


def _modulation(cmat, ada_w, ada_b):
    depth = ada_w.shape[0]
    return pl.pallas_call(
        _mod_kernel,
        grid=(depth, 6 * D // MOD_TN),
        in_specs=[
            pl.BlockSpec((MOD_ROWS, D), lambda l, n: (0, 0)),
            pl.BlockSpec((None, D, MOD_TN), lambda l, n: (l, 0, n)),
            pl.BlockSpec((None, 1, MOD_TN), lambda l, n: (l, 0, n)),
        ],
        out_specs=pl.BlockSpec((None, MOD_ROWS, MOD_TN), lambda l, n: (l, 0, n)),
        out_shape=jax.ShapeDtypeStruct((depth, MOD_ROWS, 6 * D), F32),
        compiler_params=_params("arbitrary", "arbitrary"),
        name="adaln_mod",
    )(cmat, ada_w, ada_b.reshape(depth, 1, 6 * D))


IN_TM = 1024
IN_TN = 512


def _ret_in_kernel(x_ref, nw_ref, sh_ref, sc_ref, w_ref, cos_ref, sa_ref, sb_ref, *rest, is_ctx):
    o_ref, h_ref = rest[-2:]
    n = pl.program_id(1)

    @pl.when(n == 0)
    def _():
        h_ref[...] = _rms_mod(x_ref[...], nw_ref[...], sh_ref[...], sc_ref[...]).astype(BF16)

    acc = jnp.dot(h_ref[...], w_ref[...].astype(BF16), preferred_element_type=F32)
    is_qk = n < 2 * D // IN_TN
    kscale = jnp.where(n >= D // IN_TN, RET_DK ** -0.5, 1.0).astype(F32)

    @pl.when(is_qk)
    def _():
        if is_ctx:
            o_ref[...] = (acc * kscale).astype(BF16)
        else:
            cos = cos_ref[...]
            sa = sa_ref[...]
            sb = sb_ref[...]
            for s in range(IN_TN // RET_DK):
                seg = acc[:, s * RET_DK:(s + 1) * RET_DK] * kscale
                o_ref[:, s * RET_DK:(s + 1) * RET_DK] = _rope(seg, cos, sa, sb).astype(BF16)

    @pl.when(jnp.logical_not(is_qk))
    def _():
        o_ref[...] = acc.astype(BF16)


ATT_QSCALE = ATT_HD ** -0.5 * math.log2(math.e)


def _attn_in_kernel(x_ref, nw_ref, sh_ref, sc_ref, w_ref, cos_ref, sa_ref, sb_ref, qn_ref, kn_ref,
                    *rest, is_ctx):
    o_ref, h_ref = rest[-2:]
    n = pl.program_id(1)

    @pl.when(n == 0)
    def _():
        h_ref[...] = _rms_mod(x_ref[...], nw_ref[...], sh_ref[...], sc_ref[...]).astype(BF16)

    acc = jnp.dot(h_ref[...], w_ref[...].astype(BF16), preferred_element_type=F32)
    q_tiles = D // IN_TN
    is_q = n < q_tiles
    is_qk = n <= q_tiles
    hw = jnp.where(is_q, qn_ref[...], kn_ref[...])
    qscale = jnp.where(is_q, ATT_QSCALE, 1.0).astype(F32)

    @pl.when(is_qk)
    def _():
        cos = cos_ref[...]
        sa = sa_ref[...]
        sb = sb_ref[...]
        for s in range(IN_TN // ATT_HD):
            seg = acc[:, s * ATT_HD:(s + 1) * ATT_HD]
            ms = jnp.mean(seg * seg, axis=-1, keepdims=True)
            seg = seg * (lax.rsqrt(ms + EPS) * qscale) * hw
            if not is_ctx:
                seg = _rope(seg, cos, sa, sb)
            o_ref[:, s * ATT_HD:(s + 1) * ATT_HD] = seg.astype(BF16)

    @pl.when(jnp.logical_not(is_qk))
    def _():
        o_ref[...] = acc.astype(BF16)


def _in_proj(body, name, src, src_t0, is_ctx, n_cols, hd, nw, mod, w, tabs, extra, prev_out):
    tm, tn = IN_TM, IN_TN
    tps = SEQ // tm
    out_t0 = NX // tm if is_ctx else 0
    n_tiles = (NC if is_ctx else NX) // tm
    mrow = (lambda i: CTX_ROW) if is_ctx else (lambda i: i // tps)
    tab_spec = pl.BlockSpec((tm, hd), lambda i, n: (i % tps, 0))
    in_specs = [
        pl.BlockSpec((tm, D), lambda i, n: (src_t0 + i, 0)),
        pl.BlockSpec((1, D), lambda i, n: (0, 0)),
        pl.BlockSpec((None, 1, D), lambda i, n: (mrow(i), 0, 0)),
        pl.BlockSpec((None, 1, D), lambda i, n: (mrow(i), 0, 1)),
        pl.BlockSpec((D, tn), lambda i, n: (0, n)),
        tab_spec, tab_spec, tab_spec,
    ] + [pl.BlockSpec(e.shape, lambda i, n: (0, 0)) for e in extra]
    args = [src, nw, mod, mod, w, *tabs, *extra]
    aliases = {}
    if prev_out is not None:
        in_specs.append(pl.BlockSpec(memory_space=pl.ANY))
        args.append(prev_out)
        aliases = {len(args) - 1: 0}
    return pl.pallas_call(
        functools.partial(body, is_ctx=is_ctx),
        grid=(n_tiles, n_cols // tn),
        in_specs=in_specs,
        out_specs=pl.BlockSpec((tm, tn), lambda i, n: (out_t0 + i, n)),
        out_shape=jax.ShapeDtypeStruct((NT, n_cols), BF16),
        scratch_shapes=[pltpu.VMEM((tm, D), BF16)],
        input_output_aliases=aliases,
        compiler_params=_params("arbitrary", "arbitrary"),
        name=name,
    )(*args)


def _ret_scan_kernel(ld_ref, qx_ref, kx_ref, vx_ref, gx_ref, qc_ref, kc_ref, vc_ref, gc_ref, gnw_ref,
                     yx_ref, yc_ref, o_ref, sf_ref, sb_ref):
    h = pl.program_id(1)
    c = CHUNK
    nn = lax.broadcasted_iota(jnp.int32, (c, c), 0).astype(F32)
    mm = lax.broadcasted_iota(jnp.int32, (c, c), 1).astype(F32)
    idx = lax.broadcasted_iota(jnp.int32, (c, 1), 0).astype(F32)

    def log_gamma(direction, shape):
        return -jnp.exp(jnp.full(shape, ld_ref[direction, h], F32))

    lgf, lgb = log_gamma(0, (c, c)), log_gamma(1, (c, c))
    lgf1, lgb1 = log_gamma(0, (c, 1)), log_gamma(1, (c, 1))
    lgf0, lgb0 = log_gamma(0, (1, 1)), log_gamma(1, (1, 1))
    fwd = (sf_ref,
           jnp.where(nn >= mm, jnp.exp(lgf * jnp.where(nn >= mm, nn - mm, 0.0)), 0.0),
           jnp.exp(lgf1 * (idx + 1.0)), jnp.exp(lgf1 * (c - 1.0 - idx)), jnp.exp(lgf0 * c))
    bwd = (sb_ref,
           jnp.where(mm >= nn, jnp.exp(lgb * jnp.where(mm >= nn, mm - nn, 0.0)), 0.0),
           jnp.exp(lgb1 * (c - idx)), jnp.exp(lgb1 * idx), jnp.exp(lgb0 * c))
    gnw = gnw_ref[...]

    def chunk_step(direction, q_ref, k_ref, v_ref, rows):
        st_ref, decay, qd, kd, cd = direction
        q, k, v = q_ref[rows, :], k_ref[rows, :], v_ref[rows, :]
        state = st_ref[...]
        scores = lax.dot_general(q, k, NT_DIMS, preferred_element_type=F32) * decay
        out = jnp.dot(scores.astype(BF16), v, preferred_element_type=F32)
        out = out + jnp.dot(q, state.astype(BF16), preferred_element_type=F32) * qd
        kdec = (k.astype(F32) * kd).astype(BF16)
        st_ref[...] = state * cd + lax.dot_general(kdec, v, TN_DIMS, preferred_element_type=F32)
        return out

    def finish(tot, g_ref, y_ref, rows):
        mu = jnp.mean(tot, axis=-1, keepdims=True)
        cen = tot - mu
        var = jnp.mean(cen * cen, axis=-1, keepdims=True)
        yn = cen * lax.rsqrt(var + EPS) * gnw
        y_ref[rows, :] = (_silu(g_ref[rows, :].astype(F32)) * yn).astype(BF16)

    def scan(q_ref, k_ref, v_ref, g_ref, y_ref, base, n_chunks):
        def rows_of(j):
            r = pl.multiple_of(j * c, c)
            return pl.ds(r, c), pl.ds(pl.multiple_of(base + r, c), c)

        def first_half(j, carry):
            for direction, jj in ((fwd, j), (bwd, n_chunks - 1 - j)):
                rows, orows = rows_of(jj)
                o_ref[orows, :] = chunk_step(direction, q_ref, k_ref, v_ref, rows)
            return carry

        def second_half(j, carry):
            for direction, jj in ((fwd, j), (bwd, n_chunks - 1 - j)):
                rows, orows = rows_of(jj)
                tot = chunk_step(direction, q_ref, k_ref, v_ref, rows) + o_ref[orows, :]
                finish(tot, g_ref, y_ref, rows)
            return carry

        if n_chunks == 1:
            rows = pl.ds(0, c)
            tot = chunk_step(fwd, q_ref, k_ref, v_ref, rows) + chunk_step(bwd, q_ref, k_ref, v_ref, rows)
            finish(tot, g_ref, y_ref, rows)
        else:
            lax.fori_loop(0, n_chunks // 2, first_half, 0)
            lax.fori_loop(n_chunks // 2, n_chunks, second_half, 0)

    sf_ref[...] = jnp.zeros_like(sf_ref)
    sb_ref[...] = jnp.zeros_like(sb_ref)
    scan(qc_ref, kc_ref, vc_ref, gc_ref, yc_ref, SEQ, CTX // c)
    scan(qx_ref, kx_ref, vx_ref, gx_ref, yx_ref, 0, SEQ // c)


def _ret_scan(qkvg, log_decay, gn_w):
    hq = D // RET_DK
    hv = 2 * D // RET_DV
    ctx0 = NX // CTX
    x_specs = [
        pl.BlockSpec((SEQ, RET_DK), lambda b, h: (b, h)),
        pl.BlockSpec((SEQ, RET_DK), lambda b, h: (b, hq + h)),
        pl.BlockSpec((SEQ, RET_DV), lambda b, h: (b, hv + h)),
        pl.BlockSpec((SEQ, RET_DV), lambda b, h: (b, 2 * hv + h)),
    ]
    c_specs = [
        pl.BlockSpec((CTX, RET_DK), lambda b, h: (ctx0 + b, h)),
        pl.BlockSpec((CTX, RET_DK), lambda b, h: (ctx0 + b, hq + h)),
        pl.BlockSpec((CTX, RET_DV), lambda b, h: (ctx0 + b, hv + h)),
        pl.BlockSpec((CTX, RET_DV), lambda b, h: (ctx0 + b, 2 * hv + h)),
    ]
    state = pltpu.VMEM((RET_DK, RET_DV), F32)
    return pl.pallas_call(
        _ret_scan_kernel,
        grid=(B, RET_HEADS),
        in_specs=[pl.BlockSpec(memory_space=pltpu.SMEM)] + x_specs + c_specs
                 + [pl.BlockSpec((1, RET_DV), lambda b, h: (0, h))],
        out_specs=[
            pl.BlockSpec((SEQ, RET_DV), lambda b, h: (b, h)),
            pl.BlockSpec((CTX, RET_DV), lambda b, h: (b, h)),
        ],
        out_shape=[jax.ShapeDtypeStruct((NX, 2 * D), BF16), jax.ShapeDtypeStruct((NC, 2 * D), BF16)],
        scratch_shapes=[pltpu.VMEM((SEQ + CTX, RET_DV), F32), state, state],
        compiler_params=_params("arbitrary", "arbitrary"),
        name="ret_scan",
    )(log_decay, qkvg, qkvg, qkvg, qkvg, qkvg, qkvg, qkvg, qkvg, gn_w)


PROJ_TM = 1024
PROJ_TN = 512


def _proj_res_kernel(y_ref, w_ref, x_ref, g_ref, *rest):
    o_ref = rest[-1]
    acc = jnp.dot(y_ref[...], w_ref[...].astype(BF16), preferred_element_type=F32)
    o_ref[...] = x_ref[...] + g_ref[...] * acc


def _proj_res(name, y, w, res, res_t0, mod, is_ctx, out_rows, prev_out=None):
    tm, tn = PROJ_TM, PROJ_TN
    ky = y.shape[1]
    out_t0 = NX // tm if is_ctx else 0
    mrow = (lambda i: CTX_ROW) if is_ctx else (lambda i: i // (SEQ // tm))
    gate0 = 2 * D // tn
    in_specs = [
        pl.BlockSpec((tm, ky), lambda i, n: (i, 0)),
        pl.BlockSpec((ky, tn), lambda i, n: (0, n)),
        pl.BlockSpec((tm, tn), lambda i, n: (res_t0 + i, n)),
        pl.BlockSpec((None, 1, tn), lambda i, n: (mrow(i), 0, gate0 + n)),
    ]
    args = [y, w, res, mod]
    aliases = {}
    if prev_out is not None:
        in_specs.append(pl.BlockSpec(memory_space=pl.ANY))
        args.append(prev_out)
        aliases = {len(args) - 1: 0}
    return pl.pallas_call(
        _proj_res_kernel,
        grid=(y.shape[0] // tm, D // tn),
        in_specs=in_specs,
        out_specs=pl.BlockSpec((tm, tn), lambda i, n: (out_t0 + i, n)),
        out_shape=jax.ShapeDtypeStruct((out_rows, D), F32),
        input_output_aliases=aliases,
        compiler_params=_params("arbitrary", "arbitrary"),
        name=name,
    )(*args)


FFN_TM = 1024
FFN_TF = 512
FFN_SUB = 256
FFN_DN = 512
FFN_SLAB = 256
HALO = 16


def _ffn_kernel(xp_ref, x_ref, xn_ref, nw_ref, sh_ref, sc_ref, g_ref, wu_ref, cw_ref, cb_ref,
                wd_ref, fnw_ref, o_ref, h_ref, *slab_refs, final_norm):
    tm = FFN_TM
    i = pl.program_id(0)
    f = pl.program_id(1)

    @pl.when(f == 0)
    def _():
        nw, sh, sc = nw_ref[...], sh_ref[...], sc_ref[...]
        h_ref[0:HALO, :] = _rms_mod(xp_ref[...], nw, sh, sc).astype(BF16)
        for r0 in range(0, tm, FFN_SUB):
            h_ref[HALO + r0:HALO + r0 + FFN_SUB, :] = _rms_mod(
                x_ref[r0:r0 + FFN_SUB, :], nw, sh, sc).astype(BF16)
        h_ref[HALO + tm:, :] = _rms_mod(xn_ref[...], nw, sh, sc).astype(BF16)
        o_ref[...] = jnp.zeros_like(o_ref)

    hh = h_ref[...]
    is_ctx = i >= NX // tm
    last = jnp.where(is_ctx, CTX - 1, SEQ - 1)
    sub, slab = FFN_SUB, FFN_SLAB
    n_slabs = FFN_TF // slab

    def up_proj(s):
        slab_refs[2 * s][...] = jnp.dot(hh, wu_ref[:, 2 * slab * s:2 * slab * (s + 1)],
                                        preferred_element_type=F32)

    up_proj(0)
    for s in range(n_slabs):
        if s + 1 < n_slabs:
            up_proj(s + 1)
        u_ref, act_ref = slab_refs[2 * s:2 * s + 2]
        cols = slice(2 * slab * s, 2 * slab * (s + 1))
        cw0, cw1, cw2, cb = cw_ref[0:1, cols], cw_ref[1:2, cols], cw_ref[2:3, cols], cb_ref[:, cols]
        for r0 in range(0, tm, sub):
            t = r0 + lax.broadcasted_iota(jnp.int32, (sub, 1), 0)
            pos = jnp.where(is_ctx, t % CTX, (i % (SEQ // tm)) * tm + t)
            lo = HALO + r0
            prev = jnp.where(pos > 0, u_ref[lo - 1:lo - 1 + sub, :], 0.0)
            cur = u_ref[lo:lo + sub, :]
            nxt = jnp.where(pos < last, u_ref[lo + 1:lo + 1 + sub, :], 0.0)
            u = prev * cw0 + cur * cw1 + nxt * cw2 + cb
            act_ref[r0:r0 + sub, :] = (_silu(u[:, :slab]) * u[:, slab:]).astype(BF16)
        act = act_ref[...]
        for n0 in range(0, D, FFN_DN):
            o_ref[:, n0:n0 + FFN_DN] += jnp.dot(act, wd_ref[slab * s:slab * (s + 1), n0:n0 + FFN_DN],
                                                preferred_element_type=F32)

    @pl.when(f == pl.num_programs(1) - 1)
    def _():
        for r0 in range(0, tm, FFN_SUB):
            rows = slice(r0, r0 + FFN_SUB)
            res = x_ref[rows, :] + g_ref[...] * o_ref[rows, :]
            if final_norm:
                ms = jnp.mean(res * res, axis=-1, keepdims=True)
                res = res * lax.rsqrt(ms + EPS) * fnw_ref[...]
            o_ref[rows, :] = res


def _slab_interleave(a):
    lead = a.shape[:-1]
    a = a.reshape(*lead, 2, FFN // FFN_SLAB, FFN_SLAB)
    return jnp.swapaxes(a, -3, -2).reshape(*lead, 2 * FFN)


def _conv_ffn(xs, nw, mod, w_up, conv_w, conv_b, w_down, fnw, n_rows, final_norm):
    tm, tf = FFN_TM, FFN_TF
    nf = FFN // tf
    hb = tm // HALO
    last_hb = xs.shape[0] // HALO - 1
    w_up = _slab_interleave(w_up).astype(BF16)
    conv_w = _slab_interleave(conv_w)
    conv_b = _slab_interleave(conv_b.reshape(1, 2 * FFN))
    return pl.pallas_call(
        functools.partial(_ffn_kernel, final_norm=final_norm),
        grid=(n_rows // tm, nf),
        in_specs=[
            pl.BlockSpec((HALO, D), lambda i, f: (jnp.maximum(i * hb - 1, 0), 0)),
            pl.BlockSpec((tm, D), lambda i, f: (i, 0), pipeline_mode=pl.Buffered(1)),
            pl.BlockSpec((HALO, D), lambda i, f: (jnp.minimum((i + 1) * hb, last_hb), 0)),
            pl.BlockSpec((1, D), lambda i, f: (0, 0)),
            _mod_spec(tm, 3),
            _mod_spec(tm, 4),
            _mod_spec(tm, 5),
            pl.BlockSpec((D, 2 * tf), lambda i, f: (0, f)),
            pl.BlockSpec((3, 2 * tf), lambda i, f: (0, f)),
            pl.BlockSpec((1, 2 * tf), lambda i, f: (0, f)),
            pl.BlockSpec((tf, D), lambda i, f: (f, 0)),
            pl.BlockSpec((1, D), lambda i, f: (0, 0)),
        ],
        out_specs=pl.BlockSpec((tm, D), lambda i, f: (i, 0)),
        out_shape=jax.ShapeDtypeStruct((n_rows, D), F32),
        scratch_shapes=[
            pltpu.VMEM((tm + 2 * HALO, D), BF16),
        ] + [
            pltpu.VMEM((tm + 2 * HALO, 2 * FFN_SLAB), F32),
            pltpu.VMEM((tm, FFN_SLAB), BF16),
        ] * (tf // FFN_SLAB),
        compiler_params=_params("arbitrary", "arbitrary", vmem=VMEM_LIMIT_FFN),
        name="conv_ffn",
    )(xs, xs, xs, nw, mod, mod, mod, w_up, conv_w, conv_b, w_down.astype(BF16), fnw)


ATT_TQ = 256
ATT_UNIT = 128


def _attn_kernel(q_ref, kx_ref, kc_ref, vx_ref, vc_ref, o_ref):
    kx, kc, vx, vc = kx_ref[...], kc_ref[...], vx_ref[...], vc_ref[...]
    for u in range(ATT_TQ // ATT_UNIT):
        r0 = u * ATT_UNIT
        q = q_ref[r0:r0 + ATT_UNIT, :]
        qs = jnp.concatenate([q[:, j * ATT_HD:(j + 1) * ATT_HD] for j in range(ATT_GROUP)], axis=0)
        sx = lax.dot_general(qs, kx, NT_DIMS, preferred_element_type=F32)
        sc = lax.dot_general(qs, kc, NT_DIMS, preferred_element_type=F32)
        m = jnp.maximum(jnp.max(sx, axis=-1, keepdims=True), jnp.max(sc, axis=-1, keepdims=True))
        px = jnp.exp2(sx - m)
        pc = jnp.exp2(sc - m)
        denom = jnp.sum(px, axis=-1, keepdims=True) + jnp.sum(pc, axis=-1, keepdims=True)
        out = jnp.dot(px.astype(BF16), vx, preferred_element_type=F32)
        out = out + jnp.dot(pc.astype(BF16), vc, preferred_element_type=F32)
        out = out * (1.0 / denom)
        for j in range(ATT_GROUP):
            o_ref[r0:r0 + ATT_UNIT, j * ATT_HD:(j + 1) * ATT_HD] = (
                out[j * ATT_UNIT:(j + 1) * ATT_UNIT, :].astype(BF16))


def _attention(qkv):
    tq = ATT_TQ
    qb = SEQ // tq
    k0 = D // ATT_HD
    v0 = k0 + ATT_KV
    ctx0 = NX // CTX
    return pl.pallas_call(
        _attn_kernel,
        grid=(B, ATT_KV, qb),
        in_specs=[
            pl.BlockSpec((tq, ATT_GROUP * ATT_HD), lambda b, g, t: (b * qb + t, g)),
            pl.BlockSpec((SEQ, ATT_HD), lambda b, g, t: (b, k0 + g)),
            pl.BlockSpec((CTX, ATT_HD), lambda b, g, t: (ctx0 + b, k0 + g)),
            pl.BlockSpec((SEQ, ATT_HD), lambda b, g, t: (b, v0 + g)),
            pl.BlockSpec((CTX, ATT_HD), lambda b, g, t: (ctx0 + b, v0 + g)),
        ],
        out_specs=pl.BlockSpec((tq, ATT_GROUP * ATT_HD), lambda b, g, t: (b * qb + t, g)),
        out_shape=jax.ShapeDtypeStruct((NX, D), BF16),
        compiler_params=_params("arbitrary", "arbitrary", "arbitrary"),
        name="gqa_attention",
    )(qkv, qkv, qkv, qkv, qkv)


def _rope_tables(head_dim):
    rows = SEQ // GRID_W
    row = jnp.repeat(jnp.arange(rows, dtype=F32), GRID_W)
    col = jnp.tile(jnp.arange(GRID_W, dtype=F32), rows)
    n_freq = head_dim // 4
    inv = ROPE_THETA ** (-jnp.arange(n_freq, dtype=F32) / n_freq)
    ang = jnp.concatenate([row[:, None] * inv, col[:, None] * inv], axis=-1)
    cos = jnp.repeat(jnp.cos(ang), 2, axis=-1)
    sin = jnp.repeat(jnp.sin(ang), 2, axis=-1)
    even = (jnp.arange(head_dim) % 2) == 0
    return cos, jnp.where(even, -sin, 0.0), jnp.where(even, 0.0, sin)


def kernel(x, c, ctx, c_ctx, ada_w, ada_b, norm_w, ret_w_in, ret_w_out, ret_log_decay, ret_gn_w,
           attn_w_in, attn_w_out, attn_q_norm, attn_k_norm, ffn_w_up, ffn_conv_w, ffn_conv_b,
           ffn_w_down, final_norm_w):
    x2 = x.reshape(NX, D)
    c2 = ctx.reshape(NC, D)
    cmat = jnp.concatenate([c, c_ctx[None, :], jnp.zeros((MOD_ROWS - B - 1, D), F32)], axis=0)
    mod = _modulation(cmat, ada_w, ada_b)
    mod0 = mod[0].reshape(MOD_ROWS, 1, 6 * D)
    mod1 = mod[1].reshape(MOD_ROWS, 1, 6 * D)
    fnw = final_norm_w.reshape(1, D)
    ctx_t0 = NX // IN_TM

    nw = norm_w[0, 0].reshape(1, D)
    tabs = _rope_tables(RET_DK)
    qkvg = _in_proj(_ret_in_kernel, "ret_in_x", x2, 0, False, 6 * D, RET_DK, nw, mod0, ret_w_in[0], tabs, [], None)
    qkvg = _in_proj(_ret_in_kernel, "ret_in_ctx", c2, 0, True, 6 * D, RET_DK, nw, mod0, ret_w_in[0], tabs, [], qkvg)
    yx, yc = _ret_scan(qkvg, ret_log_decay[0], ret_gn_w[0].reshape(1, 2 * D))
    xs = _proj_res("ret_out_x", yx, ret_w_out[0], x2, 0, mod0, False, NT)
    xs = _proj_res("ret_out_ctx", yc, ret_w_out[0], c2, 0, mod0, True, NT, prev_out=xs)
    xs = _conv_ffn(xs, norm_w[0, 1].reshape(1, D), mod0, ffn_w_up[0], ffn_conv_w[0], ffn_conv_b[0],
                   ffn_w_down[0], fnw, NT, False)

    nw = norm_w[1, 0].reshape(1, D)
    tabs = _rope_tables(ATT_HD)
    heads = [attn_q_norm[0].reshape(1, ATT_HD), attn_k_norm[0].reshape(1, ATT_HD)]
    qkv = _in_proj(_attn_in_kernel, "attn_in_x", xs, 0, False, ATT_IN, ATT_HD, nw, mod1, attn_w_in[0], tabs,
                   heads, None)
    qkv = _in_proj(_attn_in_kernel, "attn_in_ctx", xs, ctx_t0, True, ATT_IN, ATT_HD, nw, mod1, attn_w_in[0], tabs,
                   heads, qkv)
    ya = _attention(qkv)
    xl = _proj_res("attn_out", ya, attn_w_out[0], xs, 0, mod1, False, NX)
    out = _conv_ffn(xl, norm_w[1, 1].reshape(1, D), mod1, ffn_w_up[1], ffn_conv_w[1], ffn_conv_b[1],
                    ffn_w_down[1], fnw, NX, True)
    return out.reshape(B, SEQ, D)
```

```python
import functools
import math

import jax
import jax.numpy as jnp
from jax import lax
from jax.experimental import pallas as pl
from jax.experimental.pallas import tpu as pltpu

D = 2048
B = 4
SEQ = 2048
CTX = 256
GRID_W = 64
RET_HEADS = 8
RET_DK = D // RET_HEADS
RET_DV = 2 * D // RET_HEADS
CHUNK = 256
ATT_HEADS = 16
ATT_KV = 4
ATT_HD = D // ATT_HEADS
ATT_GROUP = ATT_HEADS // ATT_KV
ATT_IN = (ATT_HEADS + 2 * ATT_KV) * ATT_HD
FFN = 256 * ((8 * D // 3 + 255) // 256)
ROPE_THETA = 10000.0
EPS = 1e-6

NX = B * SEQ
NC = B * CTX
CTX_ROW = B
MOD_ROWS = 8

F32 = jnp.float32
BF16 = jnp.bfloat16
BF16_ROWS = 16
LANES = 128

V7X_VMEM_BYTES = 64 * 1024 * 1024
VMEM_LIMIT = V7X_VMEM_BYTES - 8 * 1024 * 1024
VMEM_LIMIT_FFN = V7X_VMEM_BYTES - 4 * 1024 * 1024

NT_DIMS = (((1,), (1,)), ((), ()))
TN_DIMS = (((0,), (0,)), ((), ()))


def _params(*sem, vmem=VMEM_LIMIT):
    return pltpu.CompilerParams(dimension_semantics=sem, vmem_limit_bytes=vmem)


def _silu(v):
    return v * (1.0 / (1.0 + jnp.exp(-v)))


def _norm_rows(x_ref, h_ref, r_ref, h_row0, n_rows, nw, sh, sc):
    gain = nw * (1.0 + sc)
    lanes = r_ref.shape[1]

    def stats(j, carry):
        r = pl.multiple_of(j * BF16_ROWS, BF16_ROWS)
        x = x_ref[pl.ds(r, BF16_ROWS), :]
        sq = x * x
        part = sq[:, 0:lanes]
        for t in range(1, x.shape[1] // lanes):
            part = part + sq[:, t * lanes:(t + 1) * lanes]
        r_ref[pl.ds(r, BF16_ROWS), :] = part
        return carry

    def finish_stats():
        rows = slice(0, n_rows)
        ms = jnp.sum(r_ref[rows, :], axis=-1, keepdims=True) * (1.0 / x_ref.shape[1])
        r_ref[rows, :] = jnp.broadcast_to(lax.rsqrt(ms + EPS), (n_rows, lanes))

    def apply(j, carry):
        r = pl.multiple_of(j * BF16_ROWS, BF16_ROWS)
        x = x_ref[pl.ds(r, BF16_ROWS), :]
        inv = pltpu.repeat(r_ref[pl.ds(r, BF16_ROWS), :], x.shape[1] // lanes, axis=1)
        dst = pl.ds(pl.multiple_of(h_row0 + r, BF16_ROWS), BF16_ROWS)
        h_ref[dst, :] = (x * inv * gain + sh).astype(BF16)
        return carry

    trips = n_rows // BF16_ROWS
    if trips == 1:
        stats(0, 0)
        finish_stats()
        apply(0, 0)
    else:
        lax.fori_loop(0, trips, stats, 0, unroll=4)
        finish_stats()
        lax.fori_loop(0, trips, apply, 0, unroll=2)


def _mod_row_fn(is_ctx, tm):
    if is_ctx:
        return lambda i: CTX_ROW
    return lambda i: i // (SEQ // tm)


def _rope(seg, cos, sa, sb):
    hd = seg.shape[-1]
    return seg * cos + pltpu.roll(seg, hd - 1, 1) * sa + pltpu.roll(seg, 1, 1) * sb


MOD_TN = 1024


def _mod_kernel(c_ref, w_ref, b_ref, o_ref):
    a = _silu(c_ref[...]).astype(BF16)
    o_ref[...] = jnp.dot(a, w_ref[...].astype(BF16), preferred_element_type=F32) + b_ref[...]


def _modulation(cmat, ada_w, ada_b):
    depth = ada_w.shape[0]
    return pl.pallas_call(
        _mod_kernel,
        grid=(depth, 6 * D // MOD_TN),
        in_specs=[
            pl.BlockSpec((MOD_ROWS, D), lambda l, n: (0, 0)),
            pl.BlockSpec((None, D, MOD_TN), lambda l, n: (l, 0, n)),
            pl.BlockSpec((None, 1, MOD_TN), lambda l, n: (l, 0, n)),
        ],
        out_specs=pl.BlockSpec((None, MOD_ROWS, MOD_TN), lambda l, n: (l, 0, n)),
        out_shape=jax.ShapeDtypeStruct((depth, MOD_ROWS, 6 * D), F32),
        compiler_params=_params("arbitrary", "arbitrary"),
        name="adaln_mod",
    )(cmat, ada_w, ada_b.reshape(depth, 1, 6 * D))


IN_TM = 1024
IN_TN = 512


def _ret_in_kernel(x_ref, nw_ref, sh_ref, sc_ref, w_ref, cos_ref, sa_ref, sb_ref, o_ref, h_ref,
                   r_ref, *, is_ctx, n_off):
    n = pl.program_id(1) + n_off

    @pl.when(pl.program_id(1) == 0)
    def _():
        _norm_rows(x_ref, h_ref, r_ref, 0, IN_TM, nw_ref[...], sh_ref[...], sc_ref[...])

    acc = jnp.dot(h_ref[...], w_ref[...].astype(BF16), preferred_element_type=F32)
    is_qk = n < 2 * D // IN_TN
    kscale = jnp.where(n >= D // IN_TN, RET_DK ** -0.5, 1.0).astype(F32)

    @pl.when(is_qk)
    def _():
        if is_ctx:
            o_ref[...] = (acc * kscale).astype(BF16)
        else:
            cos = cos_ref[...]
            sa = sa_ref[...]
            sb = sb_ref[...]
            for s in range(IN_TN // RET_DK):
                seg = acc[:, s * RET_DK:(s + 1) * RET_DK] * kscale
                o_ref[:, s * RET_DK:(s + 1) * RET_DK] = _rope(seg, cos, sa, sb).astype(BF16)

    @pl.when(jnp.logical_not(is_qk))
    def _():
        o_ref[...] = acc.astype(BF16)


ATT_QSCALE = ATT_HD ** -0.5 * math.log2(math.e)
ATT_Q_TILES = D // IN_TN


def _attn_in_kernel(x_ref, nw_ref, sh_ref, sc_ref, w_ref, cos_ref, sa_ref, sb_ref, qn_ref, kn_ref,
                    o_ref, h_ref, r_ref, *, is_ctx, n_off):
    n = pl.program_id(1) + n_off

    @pl.when(pl.program_id(1) == 0)
    def _():
        _norm_rows(x_ref, h_ref, r_ref, 0, IN_TM, nw_ref[...], sh_ref[...], sc_ref[...])

    acc = jnp.dot(h_ref[...], w_ref[...].astype(BF16), preferred_element_type=F32)
    is_q = n < ATT_Q_TILES
    is_qk = n <= ATT_Q_TILES
    hw = jnp.where(is_q, qn_ref[...], kn_ref[...])
    qscale = jnp.where(is_q, ATT_QSCALE, 1.0).astype(F32)

    @pl.when(is_qk)
    def _():
        cos = cos_ref[...]
        sa = sa_ref[...]
        sb = sb_ref[...]
        for s in range(IN_TN // ATT_HD):
            seg = acc[:, s * ATT_HD:(s + 1) * ATT_HD]
            ms = jnp.mean(seg * seg, axis=-1, keepdims=True)
            seg = seg * (lax.rsqrt(ms + EPS) * qscale) * hw
            if not is_ctx:
                seg = _rope(seg, cos, sa, sb)
            o_ref[:, s * ATT_HD:(s + 1) * ATT_HD] = seg.astype(BF16)

    @pl.when(jnp.logical_not(is_qk))
    def _():
        o_ref[...] = acc.astype(BF16)


def _in_proj(body, name, src, is_ctx, n_off, n_cols, hd, nw, mod, w, tabs, extra):
    tm, tn = IN_TM, IN_TN
    tps = SEQ // tm
    mrow = _mod_row_fn(is_ctx, tm)
    tab_spec = pl.BlockSpec((tm, hd), lambda i, n: (i % tps, 0))
    return pl.pallas_call(
        functools.partial(body, is_ctx=is_ctx, n_off=n_off),
        grid=(src.shape[0] // tm, n_cols // tn),
        in_specs=[
            pl.BlockSpec((tm, D), lambda i, n: (i, 0)),
            pl.BlockSpec((1, D), lambda i, n: (0, 0)),
            pl.BlockSpec((None, 1, D), lambda i, n: (mrow(i), 0, 0)),
            pl.BlockSpec((None, 1, D), lambda i, n: (mrow(i), 0, 1)),
            pl.BlockSpec((D, tn), lambda i, n: (0, n_off + n)),
            tab_spec, tab_spec, tab_spec,
        ] + [pl.BlockSpec(e.shape, lambda i, n: (0, 0)) for e in extra],
        out_specs=pl.BlockSpec((tm, tn), lambda i, n: (i, n)),
        out_shape=jax.ShapeDtypeStruct((src.shape[0], n_cols), BF16),
        scratch_shapes=[pltpu.VMEM((tm, D), BF16), pltpu.VMEM((tm, LANES), F32)],
        compiler_params=_params("arbitrary", "arbitrary"),
        name=name,
    )(src, nw, mod, mod, w, *tabs, *extra)


def _ret_scan_kernel(ld_ref, qx_ref, kx_ref, vx_ref, gx_ref, qc_ref, kc_ref, vc_ref, gc_ref, gnw_ref,
                     yx_ref, yc_ref, o_ref, sf_ref, sb_ref):
    h = pl.program_id(1)
    c = CHUNK
    nn = lax.broadcasted_iota(jnp.int32, (c, c), 0).astype(F32)
    mm = lax.broadcasted_iota(jnp.int32, (c, c), 1).astype(F32)
    idx = lax.broadcasted_iota(jnp.int32, (c, 1), 0).astype(F32)

    def log_gamma(direction, shape):
        return -jnp.exp(jnp.full(shape, ld_ref[direction, h], F32))

    lgf, lgb = log_gamma(0, (c, c)), log_gamma(1, (c, c))
    lgf1, lgb1 = log_gamma(0, (c, 1)), log_gamma(1, (c, 1))
    lgf0, lgb0 = log_gamma(0, (1, 1)), log_gamma(1, (1, 1))
    fwd = (sf_ref,
           jnp.where(nn >= mm, jnp.exp(lgf * jnp.where(nn >= mm, nn - mm, 0.0)), 0.0),
           jnp.exp(lgf1 * (idx + 1.0)), jnp.exp(lgf1 * (c - 1.0 - idx)), jnp.exp(lgf0 * c))
    bwd = (sb_ref,
           jnp.where(mm >= nn, jnp.exp(lgb * jnp.where(mm >= nn, mm - nn, 0.0)), 0.0),
           jnp.exp(lgb1 * (c - idx)), jnp.exp(lgb1 * idx), jnp.exp(lgb0 * c))
    gnw = gnw_ref[...]

    def chunk_step(direction, q_ref, k_ref, v_ref, rows):
        st_ref, decay, qd, kd, cd = direction
        q, k, v = q_ref[rows, :], k_ref[rows, :], v_ref[rows, :]
        state = st_ref[...]
        scores = lax.dot_general(q, k, NT_DIMS, preferred_element_type=F32) * decay
        out = jnp.dot(scores.astype(BF16), v, preferred_element_type=F32)
        out = out + jnp.dot(q, state.astype(BF16), preferred_element_type=F32) * qd
        kdec = (k.astype(F32) * kd).astype(BF16)
        st_ref[...] = state * cd + lax.dot_general(kdec, v, TN_DIMS, preferred_element_type=F32)
        return out

    def finish(tot, g_ref, y_ref, rows):
        mu = jnp.mean(tot, axis=-1, keepdims=True)
        cen = tot - mu
        var = jnp.mean(cen * cen, axis=-1, keepdims=True)
        yn = cen * lax.rsqrt(var + EPS) * gnw
        y_ref[rows, :] = (_silu(g_ref[rows, :].astype(F32)) * yn).astype(BF16)

    def scan(q_ref, k_ref, v_ref, g_ref, y_ref, base, n_chunks):
        def rows_of(j):
            r = pl.multiple_of(j * c, c)
            return pl.ds(r, c), pl.ds(pl.multiple_of(base + r, c), c)

        def first_half(j, carry):
            for direction, jj in ((fwd, j), (bwd, n_chunks - 1 - j)):
                rows, orows = rows_of(jj)
                o_ref[orows, :] = chunk_step(direction, q_ref, k_ref, v_ref, rows)
            return carry

        def second_half(j, carry):
            for direction, jj in ((fwd, j), (bwd, n_chunks - 1 - j)):
                rows, orows = rows_of(jj)
                tot = chunk_step(direction, q_ref, k_ref, v_ref, rows) + o_ref[orows, :]
                finish(tot, g_ref, y_ref, rows)
            return carry

        if n_chunks == 1:
            rows = pl.ds(0, c)
            tot = chunk_step(fwd, q_ref, k_ref, v_ref, rows) + chunk_step(bwd, q_ref, k_ref, v_ref, rows)
            finish(tot, g_ref, y_ref, rows)
        else:
            lax.fori_loop(0, n_chunks // 2, first_half, 0)
            lax.fori_loop(n_chunks // 2, n_chunks, second_half, 0)

    sf_ref[...] = jnp.zeros_like(sf_ref)
    sb_ref[...] = jnp.zeros_like(sb_ref)
    scan(qc_ref, kc_ref, vc_ref, gc_ref, yc_ref, SEQ, CTX // c)
    scan(qx_ref, kx_ref, vx_ref, gx_ref, yx_ref, 0, SEQ // c)


def _ret_scan(qkvg_x, qkvg_c, log_decay, gn_w):
    hq = D // RET_DK
    hv = 2 * D // RET_DV

    def specs(rows):
        return [
            pl.BlockSpec((rows, RET_DK), lambda b, h: (b, h)),
            pl.BlockSpec((rows, RET_DK), lambda b, h: (b, hq + h)),
            pl.BlockSpec((rows, RET_DV), lambda b, h: (b, hv + h)),
            pl.BlockSpec((rows, RET_DV), lambda b, h: (b, 2 * hv + h)),
        ]

    state = pltpu.VMEM((RET_DK, RET_DV), F32)
    return pl.pallas_call(
        _ret_scan_kernel,
        grid=(B, RET_HEADS),
        in_specs=[pl.BlockSpec(memory_space=pltpu.SMEM)] + specs(SEQ) + specs(CTX)
                 + [pl.BlockSpec((1, RET_DV), lambda b, h: (0, h))],
        out_specs=[
            pl.BlockSpec((SEQ, RET_DV), lambda b, h: (b, h)),
            pl.BlockSpec((CTX, RET_DV), lambda b, h: (b, h)),
        ],
        out_shape=[jax.ShapeDtypeStruct((NX, 2 * D), BF16), jax.ShapeDtypeStruct((NC, 2 * D), BF16)],
        scratch_shapes=[pltpu.VMEM((SEQ + CTX, RET_DV), F32), state, state],
        compiler_params=_params("arbitrary", "arbitrary"),
        name="ret_scan",
    )(log_decay, qkvg_x, qkvg_x, qkvg_x, qkvg_x, qkvg_c, qkvg_c, qkvg_c, qkvg_c, gn_w)


PROJ_TM = 1024
PROJ_TN = 512


def _proj_res_kernel(y_ref, w_ref, x_ref, g_ref, o_ref):
    acc = jnp.dot(y_ref[...], w_ref[...].astype(BF16), preferred_element_type=F32)
    o_ref[...] = x_ref[...] + g_ref[...] * acc


def _proj_res(name, y, w, res, mod, is_ctx):
    tm, tn = PROJ_TM, PROJ_TN
    rows, ky = y.shape
    mrow = _mod_row_fn(is_ctx, tm)
    gate0 = 2 * D // tn
    return pl.pallas_call(
        _proj_res_kernel,
        grid=(rows // tm, D // tn),
        in_specs=[
            pl.BlockSpec((tm, ky), lambda i, n: (i, 0)),
            pl.BlockSpec((ky, tn), lambda i, n: (0, n)),
            pl.BlockSpec((tm, tn), lambda i, n: (i, n)),
            pl.BlockSpec((None, 1, tn), lambda i, n: (mrow(i), 0, gate0 + n)),
        ],
        out_specs=pl.BlockSpec((tm, tn), lambda i, n: (i, n)),
        out_shape=jax.ShapeDtypeStruct((rows, D), F32),
        compiler_params=_params("arbitrary", "arbitrary"),
        name=name,
    )(y, w, res, mod)


FFN_TM = 1024
FFN_TF = 512
FFN_SUB = 256
FFN_DN = 512
FFN_SLAB = 256
HALO = BF16_ROWS


def _ffn_kernel(xp_ref, x_ref, xn_ref, nw_ref, sh_ref, sc_ref, g_ref, wa_ref, wb_ref,
                cwa_ref, cwb_ref, cba_ref, cbb_ref, wd_ref, fnw_ref, o_ref,
                h_ref, r_ref, *slab_refs, is_ctx, final_norm):
    tm = FFN_TM
    i = pl.program_id(0)
    f = pl.program_id(1)

    @pl.when(f == 0)
    def _():
        nw, sh, sc = nw_ref[...], sh_ref[...], sc_ref[...]
        _norm_rows(xp_ref, h_ref, r_ref, 0, HALO, nw, sh, sc)
        _norm_rows(x_ref, h_ref, r_ref, HALO, tm, nw, sh, sc)
        _norm_rows(xn_ref, h_ref, r_ref, HALO + tm, HALO, nw, sh, sc)
        o_ref[...] = jnp.zeros_like(o_ref)

    hh = h_ref[...]
    sub, slab = FFN_SUB, FFN_SLAB
    n_slabs = FFN_TF // slab
    seq_len = CTX if is_ctx else SEQ

    def up_proj(s):
        cols = slice(slab * s, slab * (s + 1))
        slab_refs[3 * s][...] = jnp.dot(hh, wa_ref[:, cols], preferred_element_type=F32)
        slab_refs[3 * s + 1][...] = jnp.dot(hh, wb_ref[:, cols], preferred_element_type=F32)

    up_proj(0)
    for s in range(n_slabs):
        if s + 1 < n_slabs:
            up_proj(s + 1)
        ua_ref, ub_ref, act_ref = slab_refs[3 * s:3 * s + 3]
        cols = slice(slab * s, slab * (s + 1))
        for r0 in range(0, tm, sub):
            t = r0 + lax.broadcasted_iota(jnp.int32, (sub, 1), 0)
            pos = t % CTX if is_ctx else (i % (SEQ // tm)) * tm + t
            has_prev = pos > 0
            has_next = pos < seq_len - 1
            lo = HALO + r0

            def conv(u_ref, cw_ref, cb_ref):
                prev = jnp.where(has_prev, u_ref[lo - 1:lo - 1 + sub, :], 0.0)
                cur = u_ref[lo:lo + sub, :]
                nxt = jnp.where(has_next, u_ref[lo + 1:lo + 1 + sub, :], 0.0)
                return (prev * cw_ref[0:1, cols] + cur * cw_ref[1:2, cols] + nxt * cw_ref[2:3, cols]
                        + cb_ref[:, cols])

            a = conv(ua_ref, cwa_ref, cba_ref)
            b = conv(ub_ref, cwb_ref, cbb_ref)
            act_ref[r0:r0 + sub, :] = (_silu(a) * b).astype(BF16)
        act = act_ref[...]
        for n0 in range(0, D, FFN_DN):
            o_ref[:, n0:n0 + FFN_DN] += jnp.dot(act, wd_ref[cols, n0:n0 + FFN_DN],
                                                preferred_element_type=F32)

    @pl.when(f == pl.num_programs(1) - 1)
    def _():
        for r0 in range(0, tm, sub):
            rows = slice(r0, r0 + sub)
            res = x_ref[rows, :] + g_ref[...] * o_ref[rows, :]
            if final_norm:
                ms = jnp.mean(res * res, axis=-1, keepdims=True)
                res = res * lax.rsqrt(ms + EPS) * fnw_ref[...]
            o_ref[rows, :] = res


def _conv_ffn(name, xs, nw, mod, w_up, conv_w, conv_b, w_down, fnw, is_ctx, final_norm):
    tm, tf = FFN_TM, FFN_TF
    n_rows = xs.shape[0]
    nf = FFN // tf
    hb = tm // HALO
    last_hb = n_rows // HALO - 1
    mrow = _mod_row_fn(is_ctx, tm)

    def mod_spec(chunk):
        return pl.BlockSpec((None, 1, D), lambda i, f: (mrow(i), 0, chunk))

    return pl.pallas_call(
        functools.partial(_ffn_kernel, is_ctx=is_ctx, final_norm=final_norm),
        grid=(n_rows // tm, nf),
        in_specs=[
            pl.BlockSpec((HALO, D), lambda i, f: (jnp.maximum(i * hb - 1, 0), 0)),
            pl.BlockSpec((tm, D), lambda i, f: (i, 0), pipeline_mode=pl.Buffered(1)),
            pl.BlockSpec((HALO, D), lambda i, f: (jnp.minimum((i + 1) * hb, last_hb), 0)),
            pl.BlockSpec((1, D), lambda i, f: (0, 0)),
            mod_spec(3), mod_spec(4), mod_spec(5),
            pl.BlockSpec((D, tf), lambda i, f: (0, f)),
            pl.BlockSpec((D, tf), lambda i, f: (0, nf + f)),
            pl.BlockSpec((3, tf), lambda i, f: (0, f)),
            pl.BlockSpec((3, tf), lambda i, f: (0, nf + f)),
            pl.BlockSpec((1, tf), lambda i, f: (0, f)),
            pl.BlockSpec((1, tf), lambda i, f: (0, nf + f)),
            pl.BlockSpec((tf, D), lambda i, f: (f, 0)),
            pl.BlockSpec((1, D), lambda i, f: (0, 0)),
        ],
        out_specs=pl.BlockSpec((tm, D), lambda i, f: (i, 0)),
        out_shape=jax.ShapeDtypeStruct((n_rows, D), F32),
        scratch_shapes=[
            pltpu.VMEM((tm + 2 * HALO, D), BF16),
            pltpu.VMEM((tm, LANES), F32),
        ] + [
            pltpu.VMEM((tm + 2 * HALO, FFN_SLAB), F32),
            pltpu.VMEM((tm + 2 * HALO, FFN_SLAB), F32),
            pltpu.VMEM((tm, FFN_SLAB), BF16),
        ] * (tf // FFN_SLAB),
        compiler_params=_params("arbitrary", "arbitrary", vmem=VMEM_LIMIT_FFN),
        name=name,
    )(xs, xs, xs, nw, mod, mod, mod, w_up, w_up, conv_w, conv_w, conv_b, conv_b, w_down, fnw)


ATT_TQ = 1024
ATT_UNIT = 128


def _attn_kernel(q_ref, kx_ref, kc_ref, vx_ref, vc_ref, o_ref):
    kx, kc, vx, vc = kx_ref[...], kc_ref[...], vx_ref[...], vc_ref[...]
    for u in range(ATT_TQ // ATT_UNIT):
        r0 = u * ATT_UNIT
        q = q_ref[r0:r0 + ATT_UNIT, :]
        qs = jnp.concatenate([q[:, j * ATT_HD:(j + 1) * ATT_HD] for j in range(ATT_GROUP)], axis=0)
        sx = lax.dot_general(qs, kx, NT_DIMS, preferred_element_type=F32)
        sc = lax.dot_general(qs, kc, NT_DIMS, preferred_element_type=F32)
        m = jnp.maximum(jnp.max(sx, axis=-1, keepdims=True), jnp.max(sc, axis=-1, keepdims=True))
        px = jnp.exp2(sx - m)
        pc = jnp.exp2(sc - m)
        denom = jnp.sum(px, axis=-1, keepdims=True) + jnp.sum(pc, axis=-1, keepdims=True)
        out = jnp.dot(px.astype(BF16), vx, preferred_element_type=F32)
        out = out + jnp.dot(pc.astype(BF16), vc, preferred_element_type=F32)
        out = out * (1.0 / denom)
        for j in range(ATT_GROUP):
            o_ref[r0:r0 + ATT_UNIT, j * ATT_HD:(j + 1) * ATT_HD] = (
                out[j * ATT_UNIT:(j + 1) * ATT_UNIT, :].astype(BF16))


def _attention(qkv_x, kv_c):
    tq = ATT_TQ
    qb = SEQ // tq
    k0 = D // ATT_HD
    v0 = k0 + ATT_KV
    return pl.pallas_call(
        _attn_kernel,
        grid=(B, ATT_KV, qb),
        in_specs=[
            pl.BlockSpec((tq, ATT_GROUP * ATT_HD), lambda b, g, t: (b * qb + t, g)),
            pl.BlockSpec((SEQ, ATT_HD), lambda b, g, t: (b, k0 + g)),
            pl.BlockSpec((CTX, ATT_HD), lambda b, g, t: (b, g)),
            pl.BlockSpec((SEQ, ATT_HD), lambda b, g, t: (b, v0 + g)),
            pl.BlockSpec((CTX, ATT_HD), lambda b, g, t: (b, ATT_KV + g)),
        ],
        out_specs=pl.BlockSpec((tq, ATT_GROUP * ATT_HD), lambda b, g, t: (b * qb + t, g)),
        out_shape=jax.ShapeDtypeStruct((NX, D), BF16),
        compiler_params=_params("arbitrary", "arbitrary", "arbitrary"),
        name="gqa_attention",
    )(qkv_x, qkv_x, kv_c, qkv_x, kv_c)


def _rope_tables(head_dim):
    rows = SEQ // GRID_W
    row = jnp.repeat(jnp.arange(rows, dtype=F32), GRID_W)
    col = jnp.tile(jnp.arange(GRID_W, dtype=F32), rows)
    n_freq = head_dim // 4
    inv = ROPE_THETA ** (-jnp.arange(n_freq, dtype=F32) / n_freq)
    ang = jnp.concatenate([row[:, None] * inv, col[:, None] * inv], axis=-1)
    cos = jnp.repeat(jnp.cos(ang), 2, axis=-1)
    sin = jnp.repeat(jnp.sin(ang), 2, axis=-1)
    even = (jnp.arange(head_dim) % 2) == 0
    return cos, jnp.where(even, -sin, 0.0), jnp.where(even, 0.0, sin)


def kernel(x, c, ctx, c_ctx, ada_w, ada_b, norm_w, ret_w_in, ret_w_out, ret_log_decay, ret_gn_w,
           attn_w_in, attn_w_out, attn_q_norm, attn_k_norm, ffn_w_up, ffn_conv_w, ffn_conv_b,
           ffn_w_down, final_norm_w):
    xx = x.reshape(NX, D)
    xc = ctx.reshape(NC, D)
    cmat = jnp.concatenate([c, c_ctx[None, :], jnp.zeros((MOD_ROWS - B - 1, D), F32)], axis=0)
    mod = _modulation(cmat, ada_w, ada_b)
    mod0 = mod[0].reshape(MOD_ROWS, 1, 6 * D)
    mod1 = mod[1].reshape(MOD_ROWS, 1, 6 * D)
    fnw = final_norm_w.reshape(1, D)

    nw = norm_w[0, 0].reshape(1, D)
    tabs = _rope_tables(RET_DK)
    qkvg_x = _in_proj(_ret_in_kernel, "ret_in_x", xx, False, 0, 6 * D, RET_DK, nw, mod0, ret_w_in[0], tabs, [])
    qkvg_c = _in_proj(_ret_in_kernel, "ret_in_ctx", xc, True, 0, 6 * D, RET_DK, nw, mod0, ret_w_in[0], tabs, [])
    yx, yc = _ret_scan(qkvg_x, qkvg_c, ret_log_decay[0], ret_gn_w[0].reshape(1, 2 * D))
    xx = _proj_res("ret_out_x", yx, ret_w_out[0], xx, mod0, False)
    xc = _proj_res("ret_out_ctx", yc, ret_w_out[0], xc, mod0, True)
    nw = norm_w[0, 1].reshape(1, D)
    ffn = (ffn_w_up[0].astype(BF16), ffn_conv_w[0], ffn_conv_b[0].reshape(1, 2 * FFN), ffn_w_down[0].astype(BF16))
    xx = _conv_ffn("conv_ffn_x", xx, nw, mod0, *ffn, fnw, False, False)
    xc = _conv_ffn("conv_ffn_ctx", xc, nw, mod0, *ffn, fnw, True, False)

    nw = norm_w[1, 0].reshape(1, D)
    tabs = _rope_tables(ATT_HD)
    heads = [attn_q_norm[0].reshape(1, ATT_HD), attn_k_norm[0].reshape(1, ATT_HD)]
    qkv_x = _in_proj(_attn_in_kernel, "attn_in_x", xx, False, 0, ATT_IN, ATT_HD, nw, mod1, attn_w_in[0],
                     tabs, heads)
    kv_c = _in_proj(_attn_in_kernel, "attn_in_ctx", xc, True, ATT_Q_TILES, ATT_IN - D, ATT_HD, nw, mod1,
                    attn_w_in[0], tabs, heads)
    ya = _attention(qkv_x, kv_c)
    xx = _proj_res("attn_out", ya, attn_w_out[0], xx, mod1, False)
    ffn = (ffn_w_up[1].astype(BF16), ffn_conv_w[1], ffn_conv_b[1].reshape(1, 2 * FFN), ffn_w_down[1].astype(BF16))
    out = _conv_ffn("conv_ffn_out", xx, norm_w[1, 1].reshape(1, D), mod1, *ffn, fnw, False, True)
    return out.reshape(B, SEQ, D)
```

```python
import functools
import math

import jax
import jax.numpy as jnp
from jax import lax
from jax.experimental import pallas as pl
from jax.experimental.pallas import tpu as pltpu

D = 2048
B = 4
SEQ = 2048
CTX = 256
GRID_W = 64
RET_HEADS = 8
RET_DK = D // RET_HEADS
RET_DV = 2 * D // RET_HEADS
CHUNK = 256
ATT_HEADS = 16
ATT_KV = 4
ATT_HD = D // ATT_HEADS
ATT_GROUP = ATT_HEADS // ATT_KV
ATT_IN = (ATT_HEADS + 2 * ATT_KV) * ATT_HD
FFN = 256 * ((8 * D // 3 + 255) // 256)
ROPE_THETA = 10000.0
EPS = 1e-6

NX = B * SEQ
NC = B * CTX
CTX_ROW = B
MOD_ROWS = 8

F32 = jnp.float32
BF16 = jnp.bfloat16
BF16_ROWS = 16
LANES = 128

V7X_VMEM_BYTES = 64 * 1024 * 1024
VMEM_LIMIT = V7X_VMEM_BYTES - 8 * 1024 * 1024
VMEM_LIMIT_FFN = V7X_VMEM_BYTES - 4 * 1024 * 1024

NT_DIMS = (((1,), (1,)), ((), ()))
TN_DIMS = (((0,), (0,)), ((), ()))


def _params(*sem, vmem=VMEM_LIMIT):
    return pltpu.CompilerParams(dimension_semantics=sem, vmem_limit_bytes=vmem)


def _silu(v):
    return v * (1.0 / (1.0 + jnp.exp(-v)))


def _for_row_tiles(n_rows, body, unroll):
    trips = n_rows // BF16_ROWS
    if trips == 1:
        body(0)
    else:
        def step(j, carry):
            body(pl.multiple_of(j * BF16_ROWS, BF16_ROWS))
            return carry
        lax.fori_loop(0, trips, step, 0, unroll=unroll)


def _inv_rms_rows(x_ref, r_ref, n_rows):
    lanes = r_ref.shape[1]
    width = x_ref.shape[1]

    def stats(r):
        x = x_ref[pl.ds(r, BF16_ROWS), :]
        sq = x * x
        part = sq[:, 0:lanes]
        for t in range(1, width // lanes):
            part = part + sq[:, t * lanes:(t + 1) * lanes]
        r_ref[pl.ds(r, BF16_ROWS), :] = part

    _for_row_tiles(n_rows, stats, 4)
    rows = slice(0, n_rows)
    ms = jnp.sum(r_ref[rows, :], axis=-1, keepdims=True) * (1.0 / width)
    r_ref[rows, :] = jnp.broadcast_to(lax.rsqrt(ms + EPS), (n_rows, lanes))


def _row_scale(r_ref, r, width):
    inv = r_ref[pl.ds(r, BF16_ROWS), :]
    return jnp.concatenate([inv] * (width // r_ref.shape[1]), axis=1)


def _norm_rows(x_ref, h_ref, r_ref, h_row0, n_rows, nw, sh, sc):
    gain = nw * (1.0 + sc)
    _inv_rms_rows(x_ref, r_ref, n_rows)

    def apply(r):
        x = x_ref[pl.ds(r, BF16_ROWS), :]
        dst = pl.ds(pl.multiple_of(h_row0 + r, BF16_ROWS), BF16_ROWS)
        h_ref[dst, :] = (x * _row_scale(r_ref, r, x.shape[1]) * gain + sh).astype(BF16)

    _for_row_tiles(n_rows, apply, 2)


def _scale_rows_by_inv_rms(o_ref, r_ref, n_rows, w):
    _inv_rms_rows(o_ref, r_ref, n_rows)

    def apply(r):
        rows = pl.ds(r, BF16_ROWS)
        x = o_ref[rows, :]
        o_ref[rows, :] = x * _row_scale(r_ref, r, x.shape[1]) * w

    _for_row_tiles(n_rows, apply, 2)


def _mod_row_fn(is_ctx, tm):
    if is_ctx:
        return lambda i: CTX_ROW
    return lambda i: i // (SEQ // tm)


def _rope(seg, cos, sa, sb):
    hd = seg.shape[-1]
    return seg * cos + pltpu.roll(seg, hd - 1, 1) * sa + pltpu.roll(seg, 1, 1) * sb


MOD_TN = 1024


def _mod_kernel(c_ref, w_ref, b_ref, o_ref):
    a = _silu(c_ref[...]).astype(BF16)
    o_ref[...] = jnp.dot(a, w_ref[...].astype(BF16), preferred_element_type=F32) + b_ref[...]


def _modulation(cmat, ada_w, ada_b):
    depth = ada_w.shape[0]
    return pl.pallas_call(
        _mod_kernel,
        grid=(depth, 6 * D // MOD_TN),
        in_specs=[
            pl.BlockSpec((MOD_ROWS, D), lambda l, n: (0, 0)),
            pl.BlockSpec((None, D, MOD_TN), lambda l, n: (l, 0, n)),
            pl.BlockSpec((None, 1, MOD_TN), lambda l, n: (l, 0, n)),
        ],
        out_specs=pl.BlockSpec((None, MOD_ROWS, MOD_TN), lambda l, n: (l, 0, n)),
        out_shape=jax.ShapeDtypeStruct((depth, MOD_ROWS, 6 * D), F32),
        compiler_params=_params("arbitrary", "arbitrary"),
        name="adaln_mod",
    )(cmat, ada_w, ada_b.reshape(depth, 1, 6 * D))


IN_TM = 1024
IN_TN = 512


IN_PIECES = 4


def _ret_in_epilogue(acc, rows, n, o_ref, cos_ref, sa_ref, sb_ref, extra, is_ctx):
    is_k = jnp.logical_and(n >= D // IN_TN, n < 2 * D // IN_TN)
    kscale = jnp.where(is_k, RET_DK ** -0.5, 1.0).astype(F32)
    if is_ctx:
        o_ref[rows, :] = (acc * kscale).astype(BF16)
        return
    is_qk = n < 2 * D // IN_TN
    cos, sa, sb = cos_ref[rows, :], sa_ref[rows, :], sb_ref[rows, :]
    for s in range(IN_TN // RET_DK):
        cols = slice(s * RET_DK, (s + 1) * RET_DK)
        seg = acc[:, cols]
        o_ref[rows, cols] = jnp.where(is_qk, _rope(seg * kscale, cos, sa, sb), seg).astype(BF16)


ATT_QSCALE = ATT_HD ** -0.5 * math.log2(math.e)
ATT_Q_TILES = D // IN_TN


def _attn_in_epilogue(acc, rows, n, o_ref, cos_ref, sa_ref, sb_ref, extra, is_ctx):
    qn_ref, kn_ref = extra
    is_q = n < ATT_Q_TILES
    is_qk = n <= ATT_Q_TILES
    hw = jnp.where(is_q, qn_ref[...], kn_ref[...])
    qscale = jnp.where(is_q, ATT_QSCALE, 1.0).astype(F32)
    if not is_ctx:
        cos, sa, sb = cos_ref[rows, :], sa_ref[rows, :], sb_ref[rows, :]
    for s in range(IN_TN // ATT_HD):
        cols = slice(s * ATT_HD, (s + 1) * ATT_HD)
        seg = acc[:, cols]
        ms = jnp.mean(seg * seg, axis=-1, keepdims=True)
        head = seg * (lax.rsqrt(ms + EPS) * qscale) * hw
        if not is_ctx:
            head = _rope(head, cos, sa, sb)
        o_ref[rows, cols] = jnp.where(is_qk, head, seg).astype(BF16)


def _in_proj_kernel(x_ref, nw_ref, sh_ref, sc_ref, w_ref, cos_ref, sa_ref, sb_ref, *rest,
                    epilogue, is_ctx, n_off, n_col_tiles, n_steps):
    *extra, o_ref, h_ref, r_ref, acc_a, acc_b = rest
    s = pl.program_id(0)

    @pl.when(jnp.logical_and(s % n_col_tiles == 0, s < n_steps - 1))
    def _():
        _norm_rows(x_ref, h_ref, r_ref, 0, IN_TM, nw_ref[...], sh_ref[...], sc_ref[...])

    @pl.when(s == 0)
    def _():
        acc_b[...] = jnp.zeros_like(acc_b)

    n_prev = n_off + jnp.maximum(s - 1, 0) % n_col_tiles
    pr = IN_TM // IN_PIECES

    def run(acc_w, acc_r):
        w = w_ref[...].astype(BF16)
        for p in range(IN_PIECES):
            rows = slice(p * pr, (p + 1) * pr)
            acc_w[rows, :] = jnp.dot(h_ref[rows, :], w, preferred_element_type=F32)
            epilogue(acc_r[rows, :], rows, n_prev, o_ref, cos_ref, sa_ref, sb_ref, extra, is_ctx)

    @pl.when(s % 2 == 0)
    def _():
        run(acc_a, acc_b)

    @pl.when(s % 2 == 1)
    def _():
        run(acc_b, acc_a)


def _in_proj(epilogue, name, src, is_ctx, n_off, n_cols, hd, nw, mod, w, tabs, extra):
    tm, tn = IN_TM, IN_TN
    tps = SEQ // tm
    n_row_tiles = src.shape[0] // tm
    nct = n_cols // tn
    n_steps = n_row_tiles * nct + 1
    mrow = _mod_row_fn(is_ctx, tm)

    def row_tile(s):
        return jnp.minimum(s // nct, n_row_tiles - 1)

    def prev(s):
        t = jnp.maximum(s - 1, 0)
        return t // nct, t % nct

    tab_spec = pl.BlockSpec((tm, hd), lambda s: (prev(s)[0] % tps, 0))
    return pl.pallas_call(
        functools.partial(_in_proj_kernel, epilogue=epilogue, is_ctx=is_ctx, n_off=n_off,
                          n_col_tiles=nct, n_steps=n_steps),
        grid=(n_steps,),
        in_specs=[
            pl.BlockSpec((tm, D), lambda s: (row_tile(s), 0)),
            pl.BlockSpec((1, D), lambda s: (0, 0)),
            pl.BlockSpec((None, 1, D), lambda s: (mrow(row_tile(s)), 0, 0)),
            pl.BlockSpec((None, 1, D), lambda s: (mrow(row_tile(s)), 0, 1)),
            pl.BlockSpec((D, tn), lambda s: (0, n_off + s % nct)),
            tab_spec, tab_spec, tab_spec,
        ] + [pl.BlockSpec(e.shape, lambda s: (0, 0)) for e in extra],
        out_specs=pl.BlockSpec((tm, tn), lambda s: prev(s)),
        out_shape=jax.ShapeDtypeStruct((src.shape[0], n_cols), BF16),
        scratch_shapes=[pltpu.VMEM((tm, D), BF16), pltpu.VMEM((tm, LANES), F32),
                        pltpu.VMEM((tm, tn), F32), pltpu.VMEM((tm, tn), F32)],
        compiler_params=_params("arbitrary"),
        name=name,
    )(src, nw, mod, mod, w, *tabs, *extra)


def _ret_scan_kernel(ld_ref, qx_ref, kx_ref, vx_ref, gx_ref, qc_ref, kc_ref, vc_ref, gc_ref, gnw_ref,
                     yx_ref, yc_ref, o_ref, sf_ref, sb_ref):
    h = pl.program_id(1)
    c = CHUNK
    nn = lax.broadcasted_iota(jnp.int32, (c, c), 0).astype(F32)
    mm = lax.broadcasted_iota(jnp.int32, (c, c), 1).astype(F32)
    idx = lax.broadcasted_iota(jnp.int32, (c, 1), 0).astype(F32)

    def log_gamma(direction, shape):
        return -jnp.exp(jnp.full(shape, ld_ref[direction, h], F32))

    lgf, lgb = log_gamma(0, (c, c)), log_gamma(1, (c, c))
    lgf1, lgb1 = log_gamma(0, (c, 1)), log_gamma(1, (c, 1))
    lgf0, lgb0 = log_gamma(0, (1, 1)), log_gamma(1, (1, 1))
    fwd = (sf_ref,
           jnp.where(nn >= mm, jnp.exp(lgf * jnp.where(nn >= mm, nn - mm, 0.0)), 0.0),
           jnp.exp(lgf1 * (idx + 1.0)), jnp.exp(lgf1 * (c - 1.0 - idx)), jnp.exp(lgf0 * c))
    bwd = (sb_ref,
           jnp.where(mm >= nn, jnp.exp(lgb * jnp.where(mm >= nn, mm - nn, 0.0)), 0.0),
           jnp.exp(lgb1 * (c - idx)), jnp.exp(lgb1 * idx), jnp.exp(lgb0 * c))
    gnw = gnw_ref[...]

    def chunk_step(direction, q_ref, k_ref, v_ref, rows):
        st_ref, decay, qd, kd, cd = direction
        q, k, v = q_ref[rows, :], k_ref[rows, :], v_ref[rows, :]
        state = st_ref[...]
        scores = lax.dot_general(q, k, NT_DIMS, preferred_element_type=F32) * decay
        out = jnp.dot(scores.astype(BF16), v, preferred_element_type=F32)
        out = out + jnp.dot(q, state.astype(BF16), preferred_element_type=F32) * qd
        kdec = (k.astype(F32) * kd).astype(BF16)
        st_ref[...] = state * cd + lax.dot_general(kdec, v, TN_DIMS, preferred_element_type=F32)
        return out

    def finish(tot, g_ref, y_ref, rows):
        mu = jnp.mean(tot, axis=-1, keepdims=True)
        cen = tot - mu
        var = jnp.mean(cen * cen, axis=-1, keepdims=True)
        yn = cen * lax.rsqrt(var + EPS) * gnw
        y_ref[rows, :] = (_silu(g_ref[rows, :].astype(F32)) * yn).astype(BF16)

    def scan(q_ref, k_ref, v_ref, g_ref, y_ref, base, n_chunks):
        def rows_of(j):
            r = pl.multiple_of(j * c, c)
            return pl.ds(r, c), pl.ds(pl.multiple_of(base + r, c), c)

        def first_half(j, carry):
            for direction, jj in ((fwd, j), (bwd, n_chunks - 1 - j)):
                rows, orows = rows_of(jj)
                o_ref[orows, :] = chunk_step(direction, q_ref, k_ref, v_ref, rows)
            return carry

        def second_half(j, carry):
            for direction, jj in ((fwd, j), (bwd, n_chunks - 1 - j)):
                rows, orows = rows_of(jj)
                tot = chunk_step(direction, q_ref, k_ref, v_ref, rows) + o_ref[orows, :]
                finish(tot, g_ref, y_ref, rows)
            return carry

        if n_chunks == 1:
            rows = pl.ds(0, c)
            tot = chunk_step(fwd, q_ref, k_ref, v_ref, rows) + chunk_step(bwd, q_ref, k_ref, v_ref, rows)
            finish(tot, g_ref, y_ref, rows)
        else:
            lax.fori_loop(0, n_chunks // 2, first_half, 0)
            lax.fori_loop(n_chunks // 2, n_chunks, second_half, 0)

    sf_ref[...] = jnp.zeros_like(sf_ref)
    sb_ref[...] = jnp.zeros_like(sb_ref)
    scan(qc_ref, kc_ref, vc_ref, gc_ref, yc_ref, SEQ, CTX // c)
    scan(qx_ref, kx_ref, vx_ref, gx_ref, yx_ref, 0, SEQ // c)


def _ret_scan(qkvg_x, qkvg_c, log_decay, gn_w):
    hq = D // RET_DK
    hv = 2 * D // RET_DV

    def specs(rows):
        return [
            pl.BlockSpec((rows, RET_DK), lambda b, h: (b, h)),
            pl.BlockSpec((rows, RET_DK), lambda b, h: (b, hq + h)),
            pl.BlockSpec((rows, RET_DV), lambda b, h: (b, hv + h)),
            pl.BlockSpec((rows, RET_DV), lambda b, h: (b, 2 * hv + h)),
        ]

    state = pltpu.VMEM((RET_DK, RET_DV), F32)
    return pl.pallas_call(
        _ret_scan_kernel,
        grid=(B, RET_HEADS),
        in_specs=[pl.BlockSpec(memory_space=pltpu.SMEM)] + specs(SEQ) + specs(CTX)
                 + [pl.BlockSpec((1, RET_DV), lambda b, h: (0, h))],
        out_specs=[
            pl.BlockSpec((SEQ, RET_DV), lambda b, h: (b, h)),
            pl.BlockSpec((CTX, RET_DV), lambda b, h: (b, h)),
        ],
        out_shape=[jax.ShapeDtypeStruct((NX, 2 * D), BF16), jax.ShapeDtypeStruct((NC, 2 * D), BF16)],
        scratch_shapes=[pltpu.VMEM((SEQ + CTX, RET_DV), F32), state, state],
        compiler_params=_params("arbitrary", "arbitrary"),
        name="ret_scan",
    )(log_decay, qkvg_x, qkvg_x, qkvg_x, qkvg_x, qkvg_c, qkvg_c, qkvg_c, qkvg_c, gn_w)


PROJ_TM = 1024
PROJ_TN = 512


def _proj_res_kernel(y_ref, w_ref, x_ref, g_ref, o_ref):
    acc = jnp.dot(y_ref[...], w_ref[...].astype(BF16), preferred_element_type=F32)
    o_ref[...] = x_ref[...] + g_ref[...] * acc


def _proj_res(name, y, w, res, mod, is_ctx):
    tm, tn = PROJ_TM, PROJ_TN
    rows, ky = y.shape
    mrow = _mod_row_fn(is_ctx, tm)
    gate0 = 2 * D // tn
    return pl.pallas_call(
        _proj_res_kernel,
        grid=(rows // tm, D // tn),
        in_specs=[
            pl.BlockSpec((tm, ky), lambda i, n: (i, 0)),
            pl.BlockSpec((ky, tn), lambda i, n: (0, n)),
            pl.BlockSpec((tm, tn), lambda i, n: (i, n)),
            pl.BlockSpec((None, 1, tn), lambda i, n: (mrow(i), 0, gate0 + n)),
        ],
        out_specs=pl.BlockSpec((tm, tn), lambda i, n: (i, n)),
        out_shape=jax.ShapeDtypeStruct((rows, D), F32),
        compiler_params=_params("arbitrary", "arbitrary"),
        name=name,
    )(y, w, res, mod)


FFN_TM = 1024
FFN_TF = 512
FFN_SUB = 256
FFN_DN = 512
FFN_SLAB = 256
HALO = BF16_ROWS


def _ffn_kernel(xp_ref, x_ref, xn_ref, nw_ref, sh_ref, sc_ref, g_ref, wa_ref, wb_ref,
                cwa_ref, cwb_ref, cba_ref, cbb_ref, wd_ref, fnw_ref, o_ref,
                h_ref, r_ref, *slab_refs, is_ctx, final_norm):
    tm = FFN_TM
    i = pl.program_id(0)
    f = pl.program_id(1)

    @pl.when(f == 0)
    def _():
        nw, sh, sc = nw_ref[...], sh_ref[...], sc_ref[...]
        _norm_rows(xp_ref, h_ref, r_ref, 0, HALO, nw, sh, sc)
        _norm_rows(x_ref, h_ref, r_ref, HALO, tm, nw, sh, sc)
        _norm_rows(xn_ref, h_ref, r_ref, HALO + tm, HALO, nw, sh, sc)
        o_ref[...] = jnp.zeros_like(o_ref)

    hh = h_ref[...]
    sub, slab = FFN_SUB, FFN_SLAB
    n_slabs = FFN_TF // slab
    seq_len = CTX if is_ctx else SEQ

    def up_proj(s):
        cols = slice(slab * s, slab * (s + 1))
        slab_refs[3 * s][...] = jnp.dot(hh, wa_ref[:, cols], preferred_element_type=F32)
        slab_refs[3 * s + 1][...] = jnp.dot(hh, wb_ref[:, cols], preferred_element_type=F32)

    up_proj(0)
    for s in range(n_slabs):
        if s + 1 < n_slabs:
            up_proj(s + 1)
        ua_ref, ub_ref, act_ref = slab_refs[3 * s:3 * s + 3]
        cols = slice(slab * s, slab * (s + 1))
        for r0 in range(0, tm, sub):
            t = r0 + lax.broadcasted_iota(jnp.int32, (sub, 1), 0)
            pos = t % CTX if is_ctx else (i % (SEQ // tm)) * tm + t
            has_prev = pos > 0
            has_next = pos < seq_len - 1
            lo = HALO + r0

            def conv(u_ref, cw_ref, cb_ref):
                prev = jnp.where(has_prev, u_ref[lo - 1:lo - 1 + sub, :], 0.0)
                cur = u_ref[lo:lo + sub, :]
                nxt = jnp.where(has_next, u_ref[lo + 1:lo + 1 + sub, :], 0.0)
                return (prev * cw_ref[0:1, cols] + cur * cw_ref[1:2, cols] + nxt * cw_ref[2:3, cols]
                        + cb_ref[:, cols])

            a = conv(ua_ref, cwa_ref, cba_ref)
            b = conv(ub_ref, cwb_ref, cbb_ref)
            act_ref[r0:r0 + sub, :] = (_silu(a) * b).astype(BF16)
        act = act_ref[...]
        for n0 in range(0, D, FFN_DN):
            o_ref[:, n0:n0 + FFN_DN] += jnp.dot(act, wd_ref[cols, n0:n0 + FFN_DN].astype(BF16),
                                                preferred_element_type=F32)

    @pl.when(f == pl.num_programs(1) - 1)
    def _():
        gate = g_ref[...]
        for r0 in range(0, tm, sub):
            rows = slice(r0, r0 + sub)
            o_ref[rows, :] = x_ref[rows, :] + gate * o_ref[rows, :]
        if final_norm:
            _scale_rows_by_inv_rms(o_ref, r_ref, tm, fnw_ref[...])


def _conv_ffn(name, xs, nw, mod, w_up, conv_w, conv_b, w_down, fnw, is_ctx, final_norm):
    tm, tf = FFN_TM, FFN_TF
    n_rows = xs.shape[0]
    nf = FFN // tf
    hb = tm // HALO
    last_hb = n_rows // HALO - 1
    mrow = _mod_row_fn(is_ctx, tm)

    def mod_spec(chunk):
        return pl.BlockSpec((None, 1, D), lambda i, f: (mrow(i), 0, chunk))

    return pl.pallas_call(
        functools.partial(_ffn_kernel, is_ctx=is_ctx, final_norm=final_norm),
        grid=(n_rows // tm, nf),
        in_specs=[
            pl.BlockSpec((HALO, D), lambda i, f: (jnp.maximum(i * hb - 1, 0), 0)),
            pl.BlockSpec((tm, D), lambda i, f: (i, 0), pipeline_mode=pl.Buffered(1)),
            pl.BlockSpec((HALO, D), lambda i, f: (jnp.minimum((i + 1) * hb, last_hb), 0)),
            pl.BlockSpec((1, D), lambda i, f: (0, 0)),
            mod_spec(3), mod_spec(4), mod_spec(5),
            pl.BlockSpec((D, tf), lambda i, f: (0, f)),
            pl.BlockSpec((D, tf), lambda i, f: (0, nf + f)),
            pl.BlockSpec((3, tf), lambda i, f: (0, f)),
            pl.BlockSpec((3, tf), lambda i, f: (0, nf + f)),
            pl.BlockSpec((1, tf), lambda i, f: (0, f)),
            pl.BlockSpec((1, tf), lambda i, f: (0, nf + f)),
            pl.BlockSpec((tf, D), lambda i, f: (f, 0)),
            pl.BlockSpec((1, D), lambda i, f: (0, 0)),
        ],
        out_specs=pl.BlockSpec((tm, D), lambda i, f: (i, 0)),
        out_shape=jax.ShapeDtypeStruct((n_rows, D), F32),
        scratch_shapes=[
            pltpu.VMEM((tm + 2 * HALO, D), BF16),
            pltpu.VMEM((tm, LANES), F32),
        ] + [
            pltpu.VMEM((tm + 2 * HALO, FFN_SLAB), F32),
            pltpu.VMEM((tm + 2 * HALO, FFN_SLAB), F32),
            pltpu.VMEM((tm, FFN_SLAB), BF16),
        ] * (tf // FFN_SLAB),
        compiler_params=_params("arbitrary", "arbitrary", vmem=VMEM_LIMIT_FFN),
        name=name,
    )(xs, xs, xs, nw, mod, mod, mod, w_up, w_up, conv_w, conv_w, conv_b, conv_b, w_down, fnw)


ATT_TQ = 1024
ATT_UNIT = 128


def _attn_kernel(q_ref, kx_ref, kc_ref, vx_ref, vc_ref, o_ref):
    kx, kc, vx, vc = kx_ref[...], kc_ref[...], vx_ref[...], vc_ref[...]
    for u in range(ATT_TQ // ATT_UNIT):
        r0 = u * ATT_UNIT
        q = q_ref[r0:r0 + ATT_UNIT, :]
        qs = jnp.concatenate([q[:, j * ATT_HD:(j + 1) * ATT_HD] for j in range(ATT_GROUP)], axis=0)
        sx = lax.dot_general(qs, kx, NT_DIMS, preferred_element_type=F32)
        sc = lax.dot_general(qs, kc, NT_DIMS, preferred_element_type=F32)
        m = jnp.maximum(jnp.max(sx, axis=-1, keepdims=True), jnp.max(sc, axis=-1, keepdims=True))
        px = jnp.exp2(sx - m)
        pc = jnp.exp2(sc - m)
        denom = jnp.sum(px, axis=-1, keepdims=True) + jnp.sum(pc, axis=-1, keepdims=True)
        out = jnp.dot(px.astype(BF16), vx, preferred_element_type=F32)
        out = out + jnp.dot(pc.astype(BF16), vc, preferred_element_type=F32)
        out = out * (1.0 / denom)
        for j in range(ATT_GROUP):
            o_ref[r0:r0 + ATT_UNIT, j * ATT_HD:(j + 1) * ATT_HD] = (
                out[j * ATT_UNIT:(j + 1) * ATT_UNIT, :].astype(BF16))


def _attention(qkv_x, kv_c):
    tq = ATT_TQ
    qb = SEQ // tq
    k0 = D // ATT_HD
    v0 = k0 + ATT_KV
    return pl.pallas_call(
        _attn_kernel,
        grid=(B, ATT_KV, qb),
        in_specs=[
            pl.BlockSpec((tq, ATT_GROUP * ATT_HD), lambda b, g, t: (b * qb + t, g)),
            pl.BlockSpec((SEQ, ATT_HD), lambda b, g, t: (b, k0 + g)),
            pl.BlockSpec((CTX, ATT_HD), lambda b, g, t: (b, g)),
            pl.BlockSpec((SEQ, ATT_HD), lambda b, g, t: (b, v0 + g)),
            pl.BlockSpec((CTX, ATT_HD), lambda b, g, t: (b, ATT_KV + g)),
        ],
        out_specs=pl.BlockSpec((tq, ATT_GROUP * ATT_HD), lambda b, g, t: (b * qb + t, g)),
        out_shape=jax.ShapeDtypeStruct((NX, D), BF16),
        compiler_params=_params("arbitrary", "arbitrary", "arbitrary"),
        name="gqa_attention",
    )(qkv_x, qkv_x, kv_c, qkv_x, kv_c)


def _rope_tables(head_dim):
    rows = SEQ // GRID_W
    row = jnp.repeat(jnp.arange(rows, dtype=F32), GRID_W)
    col = jnp.tile(jnp.arange(GRID_W, dtype=F32), rows)
    n_freq = head_dim // 4
    inv = ROPE_THETA ** (-jnp.arange(n_freq, dtype=F32) / n_freq)
    ang = jnp.concatenate([row[:, None] * inv, col[:, None] * inv], axis=-1)
    cos = jnp.repeat(jnp.cos(ang), 2, axis=-1)
    sin = jnp.repeat(jnp.sin(ang), 2, axis=-1)
    even = (jnp.arange(head_dim) % 2) == 0
    return cos, jnp.where(even, -sin, 0.0), jnp.where(even, 0.0, sin)


def kernel(x, c, ctx, c_ctx, ada_w, ada_b, norm_w, ret_w_in, ret_w_out, ret_log_decay, ret_gn_w,
           attn_w_in, attn_w_out, attn_q_norm, attn_k_norm, ffn_w_up, ffn_conv_w, ffn_conv_b,
           ffn_w_down, final_norm_w):
    xx = x.reshape(NX, D)
    xc = ctx.reshape(NC, D)
    cmat = jnp.concatenate([c, c_ctx[None, :], jnp.zeros((MOD_ROWS - B - 1, D), F32)], axis=0)
    mod = _modulation(cmat, ada_w, ada_b)
    mod0 = mod[0].reshape(MOD_ROWS, 1, 6 * D)
    mod1 = mod[1].reshape(MOD_ROWS, 1, 6 * D)
    fnw = final_norm_w.reshape(1, D)

    nw = norm_w[0, 0].reshape(1, D)
    tabs = _rope_tables(RET_DK)
    qkvg_x = _in_proj(_ret_in_epilogue, "ret_in_x", xx, False, 0, 6 * D, RET_DK, nw, mod0, ret_w_in[0], tabs, [])
    qkvg_c = _in_proj(_ret_in_epilogue, "ret_in_ctx", xc, True, 0, 6 * D, RET_DK, nw, mod0, ret_w_in[0], tabs, [])
    yx, yc = _ret_scan(qkvg_x, qkvg_c, ret_log_decay[0], ret_gn_w[0].reshape(1, 2 * D))
    xx = _proj_res("ret_out_x", yx, ret_w_out[0], xx, mod0, False)
    xc = _proj_res("ret_out_ctx", yc, ret_w_out[0], xc, mod0, True)
    nw = norm_w[0, 1].reshape(1, D)
    ffn = (ffn_w_up[0].astype(BF16), ffn_conv_w[0], ffn_conv_b[0].reshape(1, 2 * FFN), ffn_w_down[0])
    xx = _conv_ffn("conv_ffn_x", xx, nw, mod0, *ffn, fnw, False, False)
    xc = _conv_ffn("conv_ffn_ctx", xc, nw, mod0, *ffn, fnw, True, False)

    nw = norm_w[1, 0].reshape(1, D)
    tabs = _rope_tables(ATT_HD)
    heads = [attn_q_norm[0].reshape(1, ATT_HD), attn_k_norm[0].reshape(1, ATT_HD)]
    qkv_x = _in_proj(_attn_in_epilogue, "attn_in_x", xx, False, 0, ATT_IN, ATT_HD, nw, mod1, attn_w_in[0],
                     tabs, heads)
    kv_c = _in_proj(_attn_in_epilogue, "attn_in_ctx", xc, True, ATT_Q_TILES, ATT_IN - D, ATT_HD, nw, mod1,
                    attn_w_in[0], tabs, heads)
    ya = _attention(qkv_x, kv_c)
    xx = _proj_res("attn_out", ya, attn_w_out[0], xx, mod1, False)
    ffn = (ffn_w_up[1].astype(BF16), ffn_conv_w[1], ffn_conv_b[1].reshape(1, 2 * FFN), ffn_w_down[1])
    out = _conv_ffn("conv_ffn_out", xx, norm_w[1, 1].reshape(1, D), mod1, *ffn, fnw, False, True)
    return out.reshape(B, SEQ, D)
```

```python
import functools
import math

import jax
import jax.numpy as jnp
from jax import lax
from jax.experimental import pallas as pl
from jax.experimental.pallas import tpu as pltpu

D = 2048
B = 4
SEQ = 2048
CTX = 256
GRID_W = 64
RET_HEADS = 8
RET_DK = D // RET_HEADS
RET_DV = 2 * D // RET_HEADS
CHUNK = 256
ATT_HEADS = 16
ATT_KV = 4
ATT_HD = D // ATT_HEADS
ATT_GROUP = ATT_HEADS // ATT_KV
ATT_IN = (ATT_HEADS + 2 * ATT_KV) * ATT_HD
FFN = 256 * ((8 * D // 3 + 255) // 256)
ROPE_THETA = 10000.0
EPS = 1e-6

NX = B * SEQ
NC = B * CTX
CTX_ROW = B
MOD_ROWS = 8

F32 = jnp.float32
BF16 = jnp.bfloat16
BF16_ROWS = 16
LANES = 128

V7X_VMEM_BYTES = 64 * 1024 * 1024
VMEM_LIMIT = V7X_VMEM_BYTES - 8 * 1024 * 1024
VMEM_LIMIT_FFN = V7X_VMEM_BYTES - 4 * 1024 * 1024

NT_DIMS = (((1,), (1,)), ((), ()))
TN_DIMS = (((0,), (0,)), ((), ()))


def _params(*sem, vmem=VMEM_LIMIT):
    return pltpu.CompilerParams(dimension_semantics=sem, vmem_limit_bytes=vmem)


def _silu(v):
    return v * (1.0 / (1.0 + jnp.exp(-v)))


def _for_row_tiles(n_rows, body, unroll):
    trips = n_rows // BF16_ROWS
    if trips == 1:
        body(0)
    else:
        def step(j, carry):
            body(pl.multiple_of(j * BF16_ROWS, BF16_ROWS))
            return carry
        lax.fori_loop(0, trips, step, 0, unroll=unroll)


def _inv_rms_rows(x_ref, r_ref, n_rows):
    lanes = r_ref.shape[1]
    width = x_ref.shape[1]

    def stats(r):
        x = x_ref[pl.ds(r, BF16_ROWS), :]
        sq = x * x
        part = sq[:, 0:lanes]
        for t in range(1, width // lanes):
            part = part + sq[:, t * lanes:(t + 1) * lanes]
        r_ref[pl.ds(r, BF16_ROWS), :] = part

    _for_row_tiles(n_rows, stats, 4)
    rows = slice(0, n_rows)
    ms = jnp.sum(r_ref[rows, :], axis=-1, keepdims=True) * (1.0 / width)
    r_ref[rows, :] = jnp.broadcast_to(lax.rsqrt(ms + EPS), (n_rows, lanes))


def _row_scale(r_ref, r, width):
    inv = r_ref[pl.ds(r, BF16_ROWS), :]
    return jnp.concatenate([inv] * (width // r_ref.shape[1]), axis=1)


def _norm_rows(x_ref, h_ref, r_ref, h_row0, n_rows, nw, sh, sc):
    gain = nw * (1.0 + sc)
    _inv_rms_rows(x_ref, r_ref, n_rows)

    def apply(r):
        x = x_ref[pl.ds(r, BF16_ROWS), :]
        dst = pl.ds(pl.multiple_of(h_row0 + r, BF16_ROWS), BF16_ROWS)
        h_ref[dst, :] = (x * _row_scale(r_ref, r, x.shape[1]) * gain + sh).astype(BF16)

    _for_row_tiles(n_rows, apply, 2)


def _scale_rows_by_inv_rms(o_ref, r_ref, n_rows, w):
    _inv_rms_rows(o_ref, r_ref, n_rows)

    def apply(r):
        rows = pl.ds(r, BF16_ROWS)
        x = o_ref[rows, :]
        o_ref[rows, :] = x * _row_scale(r_ref, r, x.shape[1]) * w

    _for_row_tiles(n_rows, apply, 2)


def _mod_row_fn(is_ctx, tm):
    if is_ctx:
        return lambda i: CTX_ROW
    return lambda i: i // (SEQ // tm)


def _rope_coeffs(cos, sin):
    even = lax.broadcasted_iota(jnp.int32, cos.shape, 1) % 2 == 0
    return cos, jnp.where(even, -sin, 0.0), jnp.where(even, 0.0, sin)


def _rope(seg, cos, sa, sb):
    hd = seg.shape[-1]
    return seg * cos + pltpu.roll(seg, hd - 1, 1) * sa + pltpu.roll(seg, 1, 1) * sb


MOD_TN = 1024


def _mod_kernel(c_ref, w_ref, b_ref, o_ref):
    a = _silu(c_ref[...]).astype(BF16)
    o_ref[...] = jnp.dot(a, w_ref[...].astype(BF16), preferred_element_type=F32) + b_ref[...]


def _modulation(cmat, ada_w, ada_b):
    depth = ada_w.shape[0]
    return pl.pallas_call(
        _mod_kernel,
        grid=(depth, 6 * D // MOD_TN),
        in_specs=[
            pl.BlockSpec((MOD_ROWS, D), lambda l, n: (0, 0)),
            pl.BlockSpec((None, D, MOD_TN), lambda l, n: (l, 0, n)),
            pl.BlockSpec((None, 1, MOD_TN), lambda l, n: (l, 0, n)),
        ],
        out_specs=pl.BlockSpec((None, MOD_ROWS, MOD_TN), lambda l, n: (l, 0, n)),
        out_shape=jax.ShapeDtypeStruct((depth, MOD_ROWS, 6 * D), F32),
        compiler_params=_params("arbitrary", "arbitrary"),
        name="adaln_mod",
    )(cmat, ada_w, ada_b.reshape(depth, 1, 6 * D))


IN_TM = 1024
IN_TN = 1024


IN_PIECES = 4


def _ret_in_epilogue(acc, rows, n, o_ref, cos_ref, sin_ref, extra, is_ctx):
    is_k = jnp.logical_and(n >= D // IN_TN, n < 2 * D // IN_TN)
    kscale = jnp.where(is_k, RET_DK ** -0.5, 1.0).astype(F32)
    if is_ctx:
        o_ref[rows, :] = (acc * kscale).astype(BF16)
        return
    is_qk = n < 2 * D // IN_TN
    tabs = _rope_coeffs(cos_ref[rows, :], sin_ref[rows, :])
    for s in range(IN_TN // RET_DK):
        cols = slice(s * RET_DK, (s + 1) * RET_DK)
        seg = acc[:, cols]
        o_ref[rows, cols] = jnp.where(is_qk, _rope(seg * kscale, *tabs), seg).astype(BF16)


ATT_QSCALE = ATT_HD ** -0.5 * math.log2(math.e)
ATT_Q_TILES = D // IN_TN


def _attn_in_epilogue(acc, rows, n, o_ref, cos_ref, sin_ref, extra, is_ctx):
    qn_ref, kn_ref = extra
    if not is_ctx:
        tabs = _rope_coeffs(cos_ref[rows, :], sin_ref[rows, :])
    for s in range(IN_TN // ATT_HD):
        cols = slice(s * ATT_HD, (s + 1) * ATT_HD)
        head_idx = n * (IN_TN // ATT_HD) + s
        is_q = head_idx < ATT_HEADS
        is_qk = head_idx < ATT_HEADS + ATT_KV
        hw = jnp.where(is_q, qn_ref[...], kn_ref[...])
        qscale = jnp.where(is_q, ATT_QSCALE, 1.0).astype(F32)
        seg = acc[:, cols]
        ms = jnp.mean(seg * seg, axis=-1, keepdims=True)
        head = seg * (lax.rsqrt(ms + EPS) * qscale) * hw
        if not is_ctx:
            head = _rope(head, *tabs)
        o_ref[rows, cols] = jnp.where(is_qk, head, seg).astype(BF16)


def _in_proj_kernel(x_ref, nw_ref, sh_ref, sc_ref, w_ref, cos_ref, sin_ref, *rest,
                    epilogue, is_ctx, n_off, n_col_tiles, n_steps):
    *extra, o_ref, h_ref, r_ref, acc_a, acc_b = rest
    s = pl.program_id(0)

    @pl.when(jnp.logical_and(s % n_col_tiles == 0, s < n_steps - 1))
    def _():
        _norm_rows(x_ref, h_ref, r_ref, 0, IN_TM, nw_ref[...], sh_ref[...], sc_ref[...])

    @pl.when(s == 0)
    def _():
        acc_b[...] = jnp.zeros_like(acc_b)

    n_prev = n_off + jnp.maximum(s - 1, 0) % n_col_tiles
    pr = IN_TM // IN_PIECES

    def run(acc_w, acc_r):
        w = w_ref[...].astype(BF16)
        for p in range(IN_PIECES):
            rows = slice(p * pr, (p + 1) * pr)
            acc_w[rows, :] = jnp.dot(h_ref[rows, :], w, preferred_element_type=F32)
            epilogue(acc_r[rows, :], rows, n_prev, o_ref, cos_ref, sin_ref, extra, is_ctx)

    @pl.when(s % 2 == 0)
    def _():
        run(acc_a, acc_b)

    @pl.when(s % 2 == 1)
    def _():
        run(acc_b, acc_a)


def _in_proj(epilogue, name, src, is_ctx, n_off, n_cols, hd, nw, mod, w, tabs, extra):
    tm, tn = IN_TM, IN_TN
    tps = SEQ // tm
    n_row_tiles = src.shape[0] // tm
    nct = n_cols // tn
    n_steps = n_row_tiles * nct + 1
    mrow = _mod_row_fn(is_ctx, tm)

    def row_tile(s):
        return jnp.minimum(s // nct, n_row_tiles - 1)

    def prev(s):
        t = jnp.maximum(s - 1, 0)
        return t // nct, t % nct

    tab_spec = pl.BlockSpec((tm, hd), lambda s: (prev(s)[0] % tps, 0))
    return pl.pallas_call(
        functools.partial(_in_proj_kernel, epilogue=epilogue, is_ctx=is_ctx, n_off=n_off,
                          n_col_tiles=nct, n_steps=n_steps),
        grid=(n_steps,),
        in_specs=[
            pl.BlockSpec((tm, D), lambda s: (row_tile(s), 0), pipeline_mode=pl.Buffered(1)),
            pl.BlockSpec((1, D), lambda s: (0, 0)),
            pl.BlockSpec((None, 1, D), lambda s: (mrow(row_tile(s)), 0, 0)),
            pl.BlockSpec((None, 1, D), lambda s: (mrow(row_tile(s)), 0, 1)),
            pl.BlockSpec((D, tn), lambda s: (0, n_off + s % nct)),
            tab_spec, tab_spec,
        ] + [pl.BlockSpec(e.shape, lambda s: (0, 0)) for e in extra],
        out_specs=pl.BlockSpec((tm, tn), lambda s: prev(s)),
        out_shape=jax.ShapeDtypeStruct((src.shape[0], n_cols), BF16),
        scratch_shapes=[pltpu.VMEM((tm, D), BF16), pltpu.VMEM((tm, LANES), F32),
                        pltpu.VMEM((tm, tn), F32), pltpu.VMEM((tm, tn), F32)],
        compiler_params=_params("arbitrary"),
        name=name,
    )(src, nw, mod, mod, w, *tabs, *extra)


def _ret_scan_kernel(ld_ref, qx_ref, kx_ref, vx_ref, gx_ref, qc_ref, kc_ref, vc_ref, gc_ref, gnw_ref,
                     yx_ref, yc_ref, o_ref, sf_ref, sb_ref):
    h = pl.program_id(1)
    c = CHUNK
    nn = lax.broadcasted_iota(jnp.int32, (c, c), 0).astype(F32)
    mm = lax.broadcasted_iota(jnp.int32, (c, c), 1).astype(F32)
    idx = lax.broadcasted_iota(jnp.int32, (c, 1), 0).astype(F32)

    def log_gamma(direction, shape):
        return -jnp.exp(jnp.full(shape, ld_ref[direction, h], F32))

    lgf, lgb = log_gamma(0, (c, c)), log_gamma(1, (c, c))
    lgf1, lgb1 = log_gamma(0, (c, 1)), log_gamma(1, (c, 1))
    lgf0, lgb0 = log_gamma(0, (1, 1)), log_gamma(1, (1, 1))
    fwd = (sf_ref,
           jnp.where(nn >= mm, jnp.exp(lgf * jnp.where(nn >= mm, nn - mm, 0.0)), 0.0),
           jnp.exp(lgf1 * (idx + 1.0)), jnp.exp(lgf1 * (c - 1.0 - idx)), jnp.exp(lgf0 * c))
    bwd = (sb_ref,
           jnp.where(mm >= nn, jnp.exp(lgb * jnp.where(mm >= nn, mm - nn, 0.0)), 0.0),
           jnp.exp(lgb1 * (c - idx)), jnp.exp(lgb1 * idx), jnp.exp(lgb0 * c))
    gnw = gnw_ref[...]

    def chunk_step(direction, q_ref, k_ref, v_ref, rows):
        st_ref, decay, qd, kd, cd = direction
        q, k, v = q_ref[rows, :], k_ref[rows, :], v_ref[rows, :]
        state = st_ref[...]
        scores = lax.dot_general(q, k, NT_DIMS, preferred_element_type=F32) * decay
        out = jnp.dot(scores.astype(BF16), v, preferred_element_type=F32)
        out = out + jnp.dot(q, state.astype(BF16), preferred_element_type=F32) * qd
        kdec = (k.astype(F32) * kd).astype(BF16)
        st_ref[...] = state * cd + lax.dot_general(kdec, v, TN_DIMS, preferred_element_type=F32)
        return out

    def finish(tot, g_ref, y_ref, rows):
        mu = jnp.mean(tot, axis=-1, keepdims=True)
        cen = tot - mu
        var = jnp.mean(cen * cen, axis=-1, keepdims=True)
        yn = cen * lax.rsqrt(var + EPS) * gnw
        y_ref[rows, :] = (_silu(g_ref[rows, :].astype(F32)) * yn).astype(BF16)

    def scan(q_ref, k_ref, v_ref, g_ref, y_ref, base, n_chunks):
        def rows_of(j):
            r = pl.multiple_of(j * c, c)
            return pl.ds(r, c), pl.ds(pl.multiple_of(base + r, c), c)

        def first_half(j, carry):
            for direction, jj in ((fwd, j), (bwd, n_chunks - 1 - j)):
                rows, orows = rows_of(jj)
                o_ref[orows, :] = chunk_step(direction, q_ref, k_ref, v_ref, rows)
            return carry

        def second_half(j, carry):
            for direction, jj in ((fwd, j), (bwd, n_chunks - 1 - j)):
                rows, orows = rows_of(jj)
                tot = chunk_step(direction, q_ref, k_ref, v_ref, rows) + o_ref[orows, :]
                finish(tot, g_ref, y_ref, rows)
            return carry

        if n_chunks == 1:
            rows = pl.ds(0, c)
            tot = chunk_step(fwd, q_ref, k_ref, v_ref, rows) + chunk_step(bwd, q_ref, k_ref, v_ref, rows)
            finish(tot, g_ref, y_ref, rows)
        else:
            lax.fori_loop(0, n_chunks // 2, first_half, 0)
            lax.fori_loop(n_chunks // 2, n_chunks, second_half, 0)

    sf_ref[...] = jnp.zeros_like(sf_ref)
    sb_ref[...] = jnp.zeros_like(sb_ref)
    scan(qc_ref, kc_ref, vc_ref, gc_ref, yc_ref, SEQ, CTX // c)
    scan(qx_ref, kx_ref, vx_ref, gx_ref, yx_ref, 0, SEQ // c)


def _ret_scan(qkvg_x, qkvg_c, log_decay, gn_w):
    hq = D // RET_DK
    hv = 2 * D // RET_DV

    def specs(rows):
        return [
            pl.BlockSpec((rows, RET_DK), lambda b, h: (b, h)),
            pl.BlockSpec((rows, RET_DK), lambda b, h: (b, hq + h)),
            pl.BlockSpec((rows, RET_DV), lambda b, h: (b, hv + h)),
            pl.BlockSpec((rows, RET_DV), lambda b, h: (b, 2 * hv + h)),
        ]

    state = pltpu.VMEM((RET_DK, RET_DV), F32)
    return pl.pallas_call(
        _ret_scan_kernel,
        grid=(B, RET_HEADS),
        in_specs=[pl.BlockSpec(memory_space=pltpu.SMEM)] + specs(SEQ) + specs(CTX)
                 + [pl.BlockSpec((1, RET_DV), lambda b, h: (0, h))],
        out_specs=[
            pl.BlockSpec((SEQ, RET_DV), lambda b, h: (b, h)),
            pl.BlockSpec((CTX, RET_DV), lambda b, h: (b, h)),
        ],
        out_shape=[jax.ShapeDtypeStruct((NX, 2 * D), BF16), jax.ShapeDtypeStruct((NC, 2 * D), BF16)],
        scratch_shapes=[pltpu.VMEM((SEQ + CTX, RET_DV), F32), state, state],
        compiler_params=_params("arbitrary", "arbitrary"),
        name="ret_scan",
    )(log_decay, qkvg_x, qkvg_x, qkvg_x, qkvg_x, qkvg_c, qkvg_c, qkvg_c, qkvg_c, gn_w)


PROJ_TM = 1024
PROJ_TN = 512


def _proj_res_kernel(y_ref, w_ref, x_ref, g_ref, o_ref):
    acc = jnp.dot(y_ref[...], w_ref[...].astype(BF16), preferred_element_type=F32)
    o_ref[...] = x_ref[...] + g_ref[...] * acc


def _proj_res(name, y, w, res, mod, is_ctx):
    tm, tn = PROJ_TM, PROJ_TN
    rows, ky = y.shape
    mrow = _mod_row_fn(is_ctx, tm)
    gate0 = 2 * D // tn
    return pl.pallas_call(
        _proj_res_kernel,
        grid=(rows // tm, D // tn),
        in_specs=[
            pl.BlockSpec((tm, ky), lambda i, n: (i, 0)),
            pl.BlockSpec((ky, tn), lambda i, n: (0, n)),
            pl.BlockSpec((tm, tn), lambda i, n: (i, n)),
            pl.BlockSpec((None, 1, tn), lambda i, n: (mrow(i), 0, gate0 + n)),
        ],
        out_specs=pl.BlockSpec((tm, tn), lambda i, n: (i, n)),
        out_shape=jax.ShapeDtypeStruct((rows, D), F32),
        compiler_params=_params("arbitrary", "arbitrary"),
        name=name,
    )(y, w, res, mod)


FFN_TM = 1024
FFN_TF = 512
FFN_SUB = 256
FFN_DN = 512
FFN_SLAB = 256
HALO = BF16_ROWS


def _ffn_kernel(xp_ref, x_ref, xn_ref, nw_ref, sh_ref, sc_ref, g_ref, wa_ref, wb_ref,
                cwa_ref, cwb_ref, cba_ref, cbb_ref, wd_ref, fnw_ref, o_ref,
                h_ref, r_ref, *slab_refs, is_ctx, final_norm):
    tm = FFN_TM
    i = pl.program_id(0)
    f = pl.program_id(1)

    @pl.when(f == 0)
    def _():
        nw, sh, sc = nw_ref[...], sh_ref[...], sc_ref[...]
        _norm_rows(xp_ref, h_ref, r_ref, 0, HALO, nw, sh, sc)
        _norm_rows(x_ref, h_ref, r_ref, HALO, tm, nw, sh, sc)
        _norm_rows(xn_ref, h_ref, r_ref, HALO + tm, HALO, nw, sh, sc)
        o_ref[...] = jnp.zeros_like(o_ref)

    sub, slab = FFN_SUB, FFN_SLAB
    n_slabs = FFN_TF // slab
    n_blocks = tm // sub
    seq_len = CTX if is_ctx else SEQ
    bounds = [0] + [2 * HALO + sub * (q + 1) for q in range(n_blocks - 1)] + [tm + 2 * HALO]

    def up_proj(s, q=None):
        rows = slice(0, tm + 2 * HALO) if q is None else slice(bounds[q], bounds[q + 1])
        cols = slice(slab * s, slab * (s + 1))
        hh = h_ref[rows, :]
        slab_refs[3 * s][rows, :] = jnp.dot(hh, wa_ref[:, cols], preferred_element_type=F32)
        slab_refs[3 * s + 1][rows, :] = jnp.dot(hh, wb_ref[:, cols], preferred_element_type=F32)

    def conv_act(s, q):
        ua_ref, ub_ref, act_ref = slab_refs[3 * s:3 * s + 3]
        cols = slice(slab * s, slab * (s + 1))
        r0 = sub * q
        t = r0 + lax.broadcasted_iota(jnp.int32, (sub, 1), 0)
        pos = t % CTX if is_ctx else (i % (SEQ // tm)) * tm + t
        has_prev = pos > 0
        has_next = pos < seq_len - 1
        lo = HALO + r0

        def conv(u_ref, cw_ref, cb_ref):
            prev = jnp.where(has_prev, u_ref[lo - 1:lo - 1 + sub, :], 0.0)
            cur = u_ref[lo:lo + sub, :]
            nxt = jnp.where(has_next, u_ref[lo + 1:lo + 1 + sub, :], 0.0)
            return (prev * cw_ref[0:1, cols] + cur * cw_ref[1:2, cols] + nxt * cw_ref[2:3, cols]
                    + cb_ref[:, cols])

        a = conv(ua_ref, cwa_ref, cba_ref)
        b = conv(ub_ref, cwb_ref, cbb_ref)
        act_ref[r0:r0 + sub, :] = (_silu(a) * b).astype(BF16)

    def down_proj(s, q, wd):
        rows = slice(sub * q, sub * (q + 1))
        o_ref[rows, :] += jnp.dot(slab_refs[3 * s + 2][rows, :], wd, preferred_element_type=F32)

    up_proj(0)
    for q in range(n_blocks):
        if n_slabs > 1:
            up_proj(1, q)
        conv_act(0, q)
    for s in range(n_slabs):
        wd = wd_ref[slab * s:slab * (s + 1), :].astype(BF16)
        for q in range(n_blocks):
            down_proj(s, q, wd)
            if s + 1 < n_slabs:
                conv_act(s + 1, q)
        if s + 2 < n_slabs:
            for q in range(n_blocks):
                up_proj(s + 2, q)


    @pl.when(f == pl.num_programs(1) - 1)
    def _():
        gate = g_ref[...]
        for r0 in range(0, tm, sub):
            rows = slice(r0, r0 + sub)
            o_ref[rows, :] = x_ref[rows, :] + gate * o_ref[rows, :]
        if final_norm:
            _scale_rows_by_inv_rms(o_ref, r_ref, tm, fnw_ref[...])


def _conv_ffn(name, xs, nw, mod, w_up, conv_w, conv_b, w_down, fnw, is_ctx, final_norm):
    tm, tf = FFN_TM, FFN_TF
    n_rows = xs.shape[0]
    nf = FFN // tf
    hb = tm // HALO
    last_hb = n_rows // HALO - 1
    mrow = _mod_row_fn(is_ctx, tm)

    def mod_spec(chunk):
        return pl.BlockSpec((None, 1, D), lambda i, f: (mrow(i), 0, chunk))

    return pl.pallas_call(
        functools.partial(_ffn_kernel, is_ctx=is_ctx, final_norm=final_norm),
        grid=(n_rows // tm, nf),
        in_specs=[
            pl.BlockSpec((HALO, D), lambda i, f: (jnp.maximum(i * hb - 1, 0), 0)),
            pl.BlockSpec((tm, D), lambda i, f: (i, 0), pipeline_mode=pl.Buffered(1)),
            pl.BlockSpec((HALO, D), lambda i, f: (jnp.minimum((i + 1) * hb, last_hb), 0)),
            pl.BlockSpec((1, D), lambda i, f: (0, 0)),
            mod_spec(3), mod_spec(4), mod_spec(5),
            pl.BlockSpec((D, tf), lambda i, f: (0, f)),
            pl.BlockSpec((D, tf), lambda i, f: (0, nf + f)),
            pl.BlockSpec((3, tf), lambda i, f: (0, f)),
            pl.BlockSpec((3, tf), lambda i, f: (0, nf + f)),
            pl.BlockSpec((1, tf), lambda i, f: (0, f)),
            pl.BlockSpec((1, tf), lambda i, f: (0, nf + f)),
            pl.BlockSpec((tf, D), lambda i, f: (f, 0)),
            pl.BlockSpec((1, D), lambda i, f: (0, 0)),
        ],
        out_specs=pl.BlockSpec((tm, D), lambda i, f: (i, 0)),
        out_shape=jax.ShapeDtypeStruct((n_rows, D), F32),
        scratch_shapes=[
            pltpu.VMEM((tm + 2 * HALO, D), BF16),
            pltpu.VMEM((tm, LANES), F32),
        ] + [
            pltpu.VMEM((tm + 2 * HALO, FFN_SLAB), F32),
            pltpu.VMEM((tm + 2 * HALO, FFN_SLAB), F32),
            pltpu.VMEM((tm, FFN_SLAB), BF16),
        ] * (tf // FFN_SLAB),
        compiler_params=_params("arbitrary", "arbitrary", vmem=VMEM_LIMIT_FFN),
        name=name,
    )(xs, xs, xs, nw, mod, mod, mod, w_up, w_up, conv_w, conv_w, conv_b, conv_b, w_down, fnw)


ATT_TQ = 1024
ATT_UNIT = 128


def _attn_kernel(q_ref, kx_ref, kc_ref, vx_ref, vc_ref, o_ref):
    kx, kc, vx, vc = kx_ref[...], kc_ref[...], vx_ref[...], vc_ref[...]
    for u in range(ATT_TQ // ATT_UNIT):
        r0 = u * ATT_UNIT
        q = q_ref[r0:r0 + ATT_UNIT, :]
        qs = jnp.concatenate([q[:, j * ATT_HD:(j + 1) * ATT_HD] for j in range(ATT_GROUP)], axis=0)
        sx = lax.dot_general(qs, kx, NT_DIMS, preferred_element_type=F32)
        sc = lax.dot_general(qs, kc, NT_DIMS, preferred_element_type=F32)
        m = jnp.maximum(jnp.max(sx, axis=-1, keepdims=True), jnp.max(sc, axis=-1, keepdims=True))
        px = jnp.exp2(sx - m)
        pc = jnp.exp2(sc - m)
        denom = jnp.sum(px, axis=-1, keepdims=True) + jnp.sum(pc, axis=-1, keepdims=True)
        out = jnp.dot(px.astype(BF16), vx, preferred_element_type=F32)
        out = out + jnp.dot(pc.astype(BF16), vc, preferred_element_type=F32)
        out = out * (1.0 / denom)
        for j in range(ATT_GROUP):
            o_ref[r0:r0 + ATT_UNIT, j * ATT_HD:(j + 1) * ATT_HD] = (
                out[j * ATT_UNIT:(j + 1) * ATT_UNIT, :].astype(BF16))


def _attention(qkv_x, kv_c):
    tq = ATT_TQ
    qb = SEQ // tq
    k0 = D // ATT_HD
    v0 = k0 + ATT_KV
    return pl.pallas_call(
        _attn_kernel,
        grid=(B, ATT_KV, qb),
        in_specs=[
            pl.BlockSpec((tq, ATT_GROUP * ATT_HD), lambda b, g, t: (b * qb + t, g)),
            pl.BlockSpec((SEQ, ATT_HD), lambda b, g, t: (b, k0 + g)),
            pl.BlockSpec((CTX, ATT_HD), lambda b, g, t: (b, g)),
            pl.BlockSpec((SEQ, ATT_HD), lambda b, g, t: (b, v0 + g)),
            pl.BlockSpec((CTX, ATT_HD), lambda b, g, t: (b, ATT_KV + g)),
        ],
        out_specs=pl.BlockSpec((tq, ATT_GROUP * ATT_HD), lambda b, g, t: (b * qb + t, g)),
        out_shape=jax.ShapeDtypeStruct((NX, D), BF16),
        compiler_params=_params("arbitrary", "arbitrary", "arbitrary"),
        name="gqa_attention",
    )(qkv_x, qkv_x, kv_c, qkv_x, kv_c)


def _rope_tables(head_dim):
    rows = SEQ // GRID_W
    row = jnp.repeat(jnp.arange(rows, dtype=F32), GRID_W)
    col = jnp.tile(jnp.arange(GRID_W, dtype=F32), rows)
    n_freq = head_dim // 4
    inv = ROPE_THETA ** (-jnp.arange(n_freq, dtype=F32) / n_freq)
    ang = jnp.concatenate([row[:, None] * inv, col[:, None] * inv], axis=-1)
    return jnp.repeat(jnp.cos(ang), 2, axis=-1), jnp.repeat(jnp.sin(ang), 2, axis=-1)


def kernel(x, c, ctx, c_ctx, ada_w, ada_b, norm_w, ret_w_in, ret_w_out, ret_log_decay, ret_gn_w,
           attn_w_in, attn_w_out, attn_q_norm, attn_k_norm, ffn_w_up, ffn_conv_w, ffn_conv_b,
           ffn_w_down, final_norm_w):
    xx = x.reshape(NX, D)
    xc = ctx.reshape(NC, D)
    cmat = jnp.concatenate([c, c_ctx[None, :], jnp.zeros((MOD_ROWS - B - 1, D), F32)], axis=0)
    mod = _modulation(cmat, ada_w, ada_b)
    mod0 = mod[0].reshape(MOD_ROWS, 1, 6 * D)
    mod1 = mod[1].reshape(MOD_ROWS, 1, 6 * D)
    fnw = final_norm_w.reshape(1, D)

    nw = norm_w[0, 0].reshape(1, D)
    tabs = _rope_tables(RET_DK)
    qkvg_x = _in_proj(_ret_in_epilogue, "ret_in_x", xx, False, 0, 6 * D, RET_DK, nw, mod0, ret_w_in[0], tabs, [])
    qkvg_c = _in_proj(_ret_in_epilogue, "ret_in_ctx", xc, True, 0, 6 * D, RET_DK, nw, mod0, ret_w_in[0], tabs, [])
    yx, yc = _ret_scan(qkvg_x, qkvg_c, ret_log_decay[0], ret_gn_w[0].reshape(1, 2 * D))
    xx = _proj_res("ret_out_x", yx, ret_w_out[0], xx, mod0, False)
    xc = _proj_res("ret_out_ctx", yc, ret_w_out[0], xc, mod0, True)
    nw = norm_w[0, 1].reshape(1, D)
    ffn = (ffn_w_up[0].astype(BF16), ffn_conv_w[0], ffn_conv_b[0].reshape(1, 2 * FFN), ffn_w_down[0])
    xx = _conv_ffn("conv_ffn_x", xx, nw, mod0, *ffn, fnw, False, False)
    xc = _conv_ffn("conv_ffn_ctx", xc, nw, mod0, *ffn, fnw, True, False)

    nw = norm_w[1, 0].reshape(1, D)
    tabs = _rope_tables(ATT_HD)
    heads = [attn_q_norm[0].reshape(1, ATT_HD), attn_k_norm[0].reshape(1, ATT_HD)]
    qkv_x = _in_proj(_attn_in_epilogue, "attn_in_x", xx, False, 0, ATT_IN, ATT_HD, nw, mod1, attn_w_in[0],
                     tabs, heads)
    kv_c = _in_proj(_attn_in_epilogue, "attn_in_ctx", xc, True, ATT_Q_TILES, ATT_IN - D, ATT_HD, nw, mod1,
                    attn_w_in[0], tabs, heads)
    ya = _attention(qkv_x, kv_c)
    xx = _proj_res("attn_out", ya, attn_w_out[0], xx, mod1, False)
    ffn = (ffn_w_up[1].astype(BF16), ffn_conv_w[1], ffn_conv_b[1].reshape(1, 2 * FFN), ffn_w_down[1])
    out = _conv_ffn("conv_ffn_out", xx, norm_w[1, 1].reshape(1, D), mod1, *ffn, fnw, False, True)
    return out.reshape(B, SEQ, D)
```

```python
import functools
import math

import jax
import jax.numpy as jnp
from jax import lax
from jax.experimental import pallas as pl
from jax.experimental.pallas import tpu as pltpu

D = 2048
B = 4
SEQ = 2048
CTX = 256
GRID_W = 64
RET_HEADS = 8
RET_DK = D // RET_HEADS
RET_DV = 2 * D // RET_HEADS
CHUNK = 256
ATT_HEADS = 16
ATT_KV = 4
ATT_HD = D // ATT_HEADS
ATT_GROUP = ATT_HEADS // ATT_KV
ATT_IN = (ATT_HEADS + 2 * ATT_KV) * ATT_HD
FFN = 256 * ((8 * D // 3 + 255) // 256)
ROPE_THETA = 10000.0
EPS = 1e-6

NX = B * SEQ
NC = B * CTX
CTX_ROW = B
MOD_ROWS = 8

F32 = jnp.float32
BF16 = jnp.bfloat16
BF16_ROWS = 16
LANES = 128

V7X_VMEM_BYTES = 64 * 1024 * 1024
VMEM_LIMIT = V7X_VMEM_BYTES - 8 * 1024 * 1024
VMEM_LIMIT_FFN = V7X_VMEM_BYTES - 4 * 1024 * 1024

NT_DIMS = (((1,), (1,)), ((), ()))
TN_DIMS = (((0,), (0,)), ((), ()))


def _params(*sem, vmem=VMEM_LIMIT):
    return pltpu.CompilerParams(dimension_semantics=sem, vmem_limit_bytes=vmem)


def _silu(v):
    return v * (1.0 / (1.0 + jnp.exp(-v)))


def _for_row_tiles(n_rows, body, unroll):
    trips = n_rows // BF16_ROWS
    if trips == 1:
        body(0)
    else:
        def step(j, carry):
            body(pl.multiple_of(j * BF16_ROWS, BF16_ROWS))
            return carry
        lax.fori_loop(0, trips, step, 0, unroll=unroll)


def _inv_rms_rows(x_ref, r_ref, n_rows):
    lanes = r_ref.shape[1]
    width = x_ref.shape[1]

    def stats(r):
        x = x_ref[pl.ds(r, BF16_ROWS), :]
        sq = x * x
        part = sq[:, 0:lanes]
        for t in range(1, width // lanes):
            part = part + sq[:, t * lanes:(t + 1) * lanes]
        r_ref[pl.ds(r, BF16_ROWS), :] = part

    _for_row_tiles(n_rows, stats, 4)
    rows = slice(0, n_rows)
    ms = jnp.sum(r_ref[rows, :], axis=-1, keepdims=True) * (1.0 / width)
    r_ref[rows, :] = jnp.broadcast_to(lax.rsqrt(ms + EPS), (n_rows, lanes))


def _row_scale(r_ref, r, width):
    inv = r_ref[pl.ds(r, BF16_ROWS), :]
    return jnp.concatenate([inv] * (width // r_ref.shape[1]), axis=1)


def _norm_rows(x_ref, h_ref, r_ref, h_row0, n_rows, nw, sh, sc):
    gain = nw * (1.0 + sc)
    _inv_rms_rows(x_ref, r_ref, n_rows)

    def apply(r):
        x = x_ref[pl.ds(r, BF16_ROWS), :]
        dst = pl.ds(pl.multiple_of(h_row0 + r, BF16_ROWS), BF16_ROWS)
        h_ref[dst, :] = (x * _row_scale(r_ref, r, x.shape[1]) * gain + sh).astype(BF16)

    _for_row_tiles(n_rows, apply, 2)


def _scale_rows_by_inv_rms(o_ref, r_ref, n_rows, w):
    _inv_rms_rows(o_ref, r_ref, n_rows)

    def apply(r):
        rows = pl.ds(r, BF16_ROWS)
        x = o_ref[rows, :]
        o_ref[rows, :] = x * _row_scale(r_ref, r, x.shape[1]) * w

    _for_row_tiles(n_rows, apply, 2)


def _mod_row_fn(is_ctx, tm):
    if is_ctx:
        return lambda i: CTX_ROW
    return lambda i: i // (SEQ // tm)


def _rope_coeffs(cos, sin):
    even = lax.broadcasted_iota(jnp.int32, cos.shape, 1) % 2 == 0
    return cos, jnp.where(even, -sin, 0.0), jnp.where(even, 0.0, sin)


def _rope(seg, cos, sa, sb):
    hd = seg.shape[-1]
    return seg * cos + pltpu.roll(seg, hd - 1, 1) * sa + pltpu.roll(seg, 1, 1) * sb


MOD_TN = 1024


def _mod_kernel(c_ref, w_ref, b_ref, o_ref):
    a = _silu(c_ref[...]).astype(BF16)
    o_ref[...] = jnp.dot(a, w_ref[...].astype(BF16), preferred_element_type=F32) + b_ref[...]


def _modulation(cmat, ada_w, ada_b):
    depth = ada_w.shape[0]
    return pl.pallas_call(
        _mod_kernel,
        grid=(depth, 6 * D // MOD_TN),
        in_specs=[
            pl.BlockSpec((MOD_ROWS, D), lambda l, n: (0, 0)),
            pl.BlockSpec((None, D, MOD_TN), lambda l, n: (l, 0, n)),
            pl.BlockSpec((None, 1, MOD_TN), lambda l, n: (l, 0, n)),
        ],
        out_specs=pl.BlockSpec((None, MOD_ROWS, MOD_TN), lambda l, n: (l, 0, n)),
        out_shape=jax.ShapeDtypeStruct((depth, MOD_ROWS, 6 * D), F32),
        compiler_params=_params("arbitrary", "arbitrary"),
        name="adaln_mod",
    )(cmat, ada_w, ada_b.reshape(depth, 1, 6 * D))


IN_TM = 1024
IN_TN = 1024


IN_PIECES = 4


class _RetInEpilogue:
    @staticmethod
    def kinds(n, is_ctx):
        if is_ctx:
            return [("plain", None)]
        is_qk = n < 2 * D // IN_TN
        return [("rope", is_qk), ("plain", jnp.logical_not(is_qk))]

    @staticmethod
    def apply(kind, acc, rows, n, o_ref, cos_ref, sin_ref, extra):
        is_k = jnp.logical_and(n >= D // IN_TN, n < 2 * D // IN_TN)
        kscale = jnp.where(is_k, RET_DK ** -0.5, 1.0).astype(F32)
        if kind == "plain":
            o_ref[rows, :] = (acc * kscale).astype(BF16)
            return
        tabs = _rope_coeffs(cos_ref[rows, :], sin_ref[rows, :])
        for s in range(IN_TN // RET_DK):
            cols = slice(s * RET_DK, (s + 1) * RET_DK)
            o_ref[rows, cols] = _rope(acc[:, cols] * kscale, *tabs).astype(BF16)


ATT_QSCALE = ATT_HD ** -0.5 * math.log2(math.e)
ATT_Q_TILES = D // IN_TN


class _AttnInEpilogue:
    @staticmethod
    def kinds(n, is_ctx):
        if is_ctx:
            return [("kv_ctx", None)]
        is_q = n < ATT_Q_TILES
        return [("q", is_q), ("kv", jnp.logical_not(is_q))]

    @staticmethod
    def apply(kind, acc, rows, n, o_ref, cos_ref, sin_ref, extra):
        qn_ref, kn_ref = extra
        heads = IN_TN // ATT_HD
        normed = heads if kind == "q" else ATT_KV
        hw = qn_ref[...] if kind == "q" else kn_ref[...]
        scale = ATT_QSCALE if kind == "q" else 1.0
        if kind != "kv_ctx":
            tabs = _rope_coeffs(cos_ref[rows, :], sin_ref[rows, :])
        for s in range(normed):
            cols = slice(s * ATT_HD, (s + 1) * ATT_HD)
            seg = acc[:, cols]
            ms = jnp.mean(seg * seg, axis=-1, keepdims=True)
            head = seg * (lax.rsqrt(ms + EPS) * scale) * hw
            if kind != "kv_ctx":
                head = _rope(head, *tabs)
            o_ref[rows, cols] = head.astype(BF16)
        if normed < heads:
            cols = slice(normed * ATT_HD, heads * ATT_HD)
            o_ref[rows, cols] = acc[:, cols].astype(BF16)


def _in_proj_kernel(x_ref, nw_ref, sh_ref, sc_ref, w_ref, cos_ref, sin_ref, *rest,
                    epilogue, is_ctx, n_off, n_col_tiles, n_steps):
    *extra, o_ref, h_ref, r_ref, acc_a, acc_b = rest
    s = pl.program_id(0)

    @pl.when(jnp.logical_and(s % n_col_tiles == 0, s < n_steps - 1))
    def _():
        _norm_rows(x_ref, h_ref, r_ref, 0, IN_TM, nw_ref[...], sh_ref[...], sc_ref[...])

    @pl.when(s == 0)
    def _():
        acc_b[...] = jnp.zeros_like(acc_b)

    n_prev = n_off + jnp.maximum(s - 1, 0) % n_col_tiles
    pr = IN_TM // IN_PIECES

    def run(acc_w, acc_r, kind):
        w = w_ref[...].astype(BF16)
        for p in range(IN_PIECES):
            rows = slice(p * pr, (p + 1) * pr)
            acc_w[rows, :] = jnp.dot(h_ref[rows, :], w, preferred_element_type=F32)
            epilogue.apply(kind, acc_r[rows, :], rows, n_prev, o_ref, cos_ref, sin_ref, extra)

    for parity, (acc_w, acc_r) in enumerate(((acc_a, acc_b), (acc_b, acc_a))):
        for kind, cond in epilogue.kinds(n_prev, is_ctx):
            pred = s % 2 == parity
            if cond is not None:
                pred = jnp.logical_and(pred, cond)
            pl.when(pred)(functools.partial(run, acc_w, acc_r, kind))


def _in_proj(epilogue, name, src, is_ctx, n_off, n_cols, hd, nw, mod, w, tabs, extra):
    tm, tn = IN_TM, IN_TN
    tps = SEQ // tm
    n_row_tiles = src.shape[0] // tm
    nct = n_cols // tn
    n_steps = n_row_tiles * nct + 1
    mrow = _mod_row_fn(is_ctx, tm)

    def row_tile(s):
        return jnp.minimum(s // nct, n_row_tiles - 1)

    def prev(s):
        t = jnp.maximum(s - 1, 0)
        return t // nct, t % nct

    tab_spec = pl.BlockSpec((tm, hd), lambda s: (prev(s)[0] % tps, 0))
    return pl.pallas_call(
        functools.partial(_in_proj_kernel, epilogue=epilogue, is_ctx=is_ctx, n_off=n_off,
                          n_col_tiles=nct, n_steps=n_steps),
        grid=(n_steps,),
        in_specs=[
            pl.BlockSpec((tm, D), lambda s: (row_tile(s), 0), pipeline_mode=pl.Buffered(1)),
            pl.BlockSpec((1, D), lambda s: (0, 0)),
            pl.BlockSpec((None, 1, D), lambda s: (mrow(row_tile(s)), 0, 0)),
            pl.BlockSpec((None, 1, D), lambda s: (mrow(row_tile(s)), 0, 1)),
            pl.BlockSpec((D, tn), lambda s: (0, n_off + s % nct)),
            tab_spec, tab_spec,
        ] + [pl.BlockSpec(e.shape, lambda s: (0, 0)) for e in extra],
        out_specs=pl.BlockSpec((tm, tn), lambda s: prev(s)),
        out_shape=jax.ShapeDtypeStruct((src.shape[0], n_cols), BF16),
        scratch_shapes=[pltpu.VMEM((tm, D), BF16), pltpu.VMEM((tm, LANES), F32),
                        pltpu.VMEM((tm, tn), F32), pltpu.VMEM((tm, tn), F32)],
        compiler_params=_params("arbitrary"),
        name=name,
    )(src, nw, mod, mod, w, *tabs, *extra)


def _ret_scan_kernel(ld_ref, qx_ref, kx_ref, vx_ref, gx_ref, qc_ref, kc_ref, vc_ref, gc_ref, gnw_ref,
                     yx_ref, yc_ref, o_ref, sf_ref, sb_ref):
    h = pl.program_id(1)
    c = CHUNK
    nn = lax.broadcasted_iota(jnp.int32, (c, c), 0).astype(F32)
    mm = lax.broadcasted_iota(jnp.int32, (c, c), 1).astype(F32)
    idx = lax.broadcasted_iota(jnp.int32, (c, 1), 0).astype(F32)

    def log_gamma(direction, shape):
        return -jnp.exp(jnp.full(shape, ld_ref[direction, h], F32))

    lgf, lgb = log_gamma(0, (c, c)), log_gamma(1, (c, c))
    lgf1, lgb1 = log_gamma(0, (c, 1)), log_gamma(1, (c, 1))
    lgf0, lgb0 = log_gamma(0, (1, 1)), log_gamma(1, (1, 1))
    fwd = (sf_ref,
           jnp.where(nn >= mm, jnp.exp(lgf * jnp.where(nn >= mm, nn - mm, 0.0)), 0.0),
           jnp.exp(lgf1 * (idx + 1.0)), jnp.exp(lgf1 * (c - 1.0 - idx)), jnp.exp(lgf0 * c))
    bwd = (sb_ref,
           jnp.where(mm >= nn, jnp.exp(lgb * jnp.where(mm >= nn, mm - nn, 0.0)), 0.0),
           jnp.exp(lgb1 * (c - idx)), jnp.exp(lgb1 * idx), jnp.exp(lgb0 * c))
    gnw = gnw_ref[...]

    def chunk_step(direction, q_ref, k_ref, v_ref, rows):
        st_ref, decay, qd, kd, cd = direction
        q, k, v = q_ref[rows, :], k_ref[rows, :], v_ref[rows, :]
        state = st_ref[...]
        scores = lax.dot_general(q, k, NT_DIMS, preferred_element_type=F32) * decay
        out = jnp.dot(scores.astype(BF16), v, preferred_element_type=F32)
        out = out + jnp.dot(q, state.astype(BF16), preferred_element_type=F32) * qd
        kdec = (k.astype(F32) * kd).astype(BF16)
        st_ref[...] = state * cd + lax.dot_general(kdec, v, TN_DIMS, preferred_element_type=F32)
        return out

    def finish(tot, g_ref, y_ref, rows):
        mu = jnp.mean(tot, axis=-1, keepdims=True)
        cen = tot - mu
        var = jnp.mean(cen * cen, axis=-1, keepdims=True)
        yn = cen * lax.rsqrt(var + EPS) * gnw
        y_ref[rows, :] = (_silu(g_ref[rows, :].astype(F32)) * yn).astype(BF16)

    def scan(q_ref, k_ref, v_ref, g_ref, y_ref, base, n_chunks):
        def rows_of(j):
            r = pl.multiple_of(j * c, c)
            return pl.ds(r, c), pl.ds(pl.multiple_of(base + r, c), c)

        def first_half(j, carry):
            for direction, jj in ((fwd, j), (bwd, n_chunks - 1 - j)):
                rows, orows = rows_of(jj)
                o_ref[orows, :] = chunk_step(direction, q_ref, k_ref, v_ref, rows)
            return carry

        def second_half(j, carry):
            for direction, jj in ((fwd, j), (bwd, n_chunks - 1 - j)):
                rows, orows = rows_of(jj)
                tot = chunk_step(direction, q_ref, k_ref, v_ref, rows) + o_ref[orows, :]
                finish(tot, g_ref, y_ref, rows)
            return carry

        if n_chunks == 1:
            rows = pl.ds(0, c)
            tot = chunk_step(fwd, q_ref, k_ref, v_ref, rows) + chunk_step(bwd, q_ref, k_ref, v_ref, rows)
            finish(tot, g_ref, y_ref, rows)
        else:
            lax.fori_loop(0, n_chunks // 2, first_half, 0)
            lax.fori_loop(n_chunks // 2, n_chunks, second_half, 0)

    sf_ref[...] = jnp.zeros_like(sf_ref)
    sb_ref[...] = jnp.zeros_like(sb_ref)
    scan(qc_ref, kc_ref, vc_ref, gc_ref, yc_ref, SEQ, CTX // c)
    scan(qx_ref, kx_ref, vx_ref, gx_ref, yx_ref, 0, SEQ // c)


def _ret_scan(qkvg_x, qkvg_c, log_decay, gn_w):
    hq = D // RET_DK
    hv = 2 * D // RET_DV

    def specs(rows):
        return [
            pl.BlockSpec((rows, RET_DK), lambda b, h: (b, h)),
            pl.BlockSpec((rows, RET_DK), lambda b, h: (b, hq + h)),
            pl.BlockSpec((rows, RET_DV), lambda b, h: (b, hv + h)),
            pl.BlockSpec((rows, RET_DV), lambda b, h: (b, 2 * hv + h)),
        ]

    state = pltpu.VMEM((RET_DK, RET_DV), F32)
    return pl.pallas_call(
        _ret_scan_kernel,
        grid=(B, RET_HEADS),
        in_specs=[pl.BlockSpec(memory_space=pltpu.SMEM)] + specs(SEQ) + specs(CTX)
                 + [pl.BlockSpec((1, RET_DV), lambda b, h: (0, h))],
        out_specs=[
            pl.BlockSpec((SEQ, RET_DV), lambda b, h: (b, h)),
            pl.BlockSpec((CTX, RET_DV), lambda b, h: (b, h)),
        ],
        out_shape=[jax.ShapeDtypeStruct((NX, 2 * D), BF16), jax.ShapeDtypeStruct((NC, 2 * D), BF16)],
        scratch_shapes=[pltpu.VMEM((SEQ + CTX, RET_DV), F32), state, state],
        compiler_params=_params("arbitrary", "arbitrary"),
        name="ret_scan",
    )(log_decay, qkvg_x, qkvg_x, qkvg_x, qkvg_x, qkvg_c, qkvg_c, qkvg_c, qkvg_c, gn_w)


PROJ_TM = 1024
PROJ_TN = 512


def _proj_res_kernel(y_ref, w_ref, x_ref, g_ref, o_ref):
    acc = jnp.dot(y_ref[...], w_ref[...].astype(BF16), preferred_element_type=F32)
    o_ref[...] = x_ref[...] + g_ref[...] * acc


def _proj_res(name, y, w, res, mod, is_ctx):
    tm, tn = PROJ_TM, PROJ_TN
    rows, ky = y.shape
    mrow = _mod_row_fn(is_ctx, tm)
    gate0 = 2 * D // tn
    return pl.pallas_call(
        _proj_res_kernel,
        grid=(rows // tm, D // tn),
        in_specs=[
            pl.BlockSpec((tm, ky), lambda i, n: (i, 0)),
            pl.BlockSpec((ky, tn), lambda i, n: (0, n)),
            pl.BlockSpec((tm, tn), lambda i, n: (i, n)),
            pl.BlockSpec((None, 1, tn), lambda i, n: (mrow(i), 0, gate0 + n)),
        ],
        out_specs=pl.BlockSpec((tm, tn), lambda i, n: (i, n)),
        out_shape=jax.ShapeDtypeStruct((rows, D), F32),
        compiler_params=_params("arbitrary", "arbitrary"),
        name=name,
    )(y, w, res, mod)


FFN_TM = 1024
FFN_TF = 512
FFN_SUB = 256
FFN_DN = 512
FFN_SLAB = 256
HALO = BF16_ROWS


def _ffn_kernel(xp_ref, x_ref, xn_ref, nw_ref, sh_ref, sc_ref, g_ref, wa_ref, wb_ref,
                cwa_ref, cwb_ref, cba_ref, cbb_ref, wd_ref, fnw_ref, o_ref,
                h_ref, r_ref, *slab_refs, is_ctx, final_norm):
    tm = FFN_TM
    i = pl.program_id(0)
    f = pl.program_id(1)

    @pl.when(f == 0)
    def _():
        nw, sh, sc = nw_ref[...], sh_ref[...], sc_ref[...]
        _norm_rows(xp_ref, h_ref, r_ref, 0, HALO, nw, sh, sc)
        _norm_rows(x_ref, h_ref, r_ref, HALO, tm, nw, sh, sc)
        _norm_rows(xn_ref, h_ref, r_ref, HALO + tm, HALO, nw, sh, sc)
        o_ref[...] = jnp.zeros_like(o_ref)

    sub, slab = FFN_SUB, FFN_SLAB
    n_slabs = FFN_TF // slab
    n_blocks = tm // sub
    seq_len = CTX if is_ctx else SEQ
    bounds = [0] + [2 * HALO + sub * (q + 1) for q in range(n_blocks - 1)] + [tm + 2 * HALO]

    def up_proj(s, q=None):
        rows = slice(0, tm + 2 * HALO) if q is None else slice(bounds[q], bounds[q + 1])
        cols = slice(slab * s, slab * (s + 1))
        hh = h_ref[rows, :]
        slab_refs[3 * s][rows, :] = jnp.dot(hh, wa_ref[:, cols], preferred_element_type=F32)
        slab_refs[3 * s + 1][rows, :] = jnp.dot(hh, wb_ref[:, cols], preferred_element_type=F32)

    def conv_act(s, q):
        ua_ref, ub_ref, act_ref = slab_refs[3 * s:3 * s + 3]
        cols = slice(slab * s, slab * (s + 1))
        r0 = sub * q
        t = r0 + lax.broadcasted_iota(jnp.int32, (sub, 1), 0)
        pos = t % CTX if is_ctx else (i % (SEQ // tm)) * tm + t
        has_prev = pos > 0
        has_next = pos < seq_len - 1
        lo = HALO + r0

        def conv(u_ref, cw_ref, cb_ref):
            prev = jnp.where(has_prev, u_ref[lo - 1:lo - 1 + sub, :], 0.0)
            cur = u_ref[lo:lo + sub, :]
            nxt = jnp.where(has_next, u_ref[lo + 1:lo + 1 + sub, :], 0.0)
            return (prev * cw_ref[0:1, cols] + cur * cw_ref[1:2, cols] + nxt * cw_ref[2:3, cols]
                    + cb_ref[:, cols])

        a = conv(ua_ref, cwa_ref, cba_ref)
        b = conv(ub_ref, cwb_ref, cbb_ref)
        act_ref[r0:r0 + sub, :] = (_silu(a) * b).astype(BF16)

    def down_proj(s, q, wd):
        rows = slice(sub * q, sub * (q + 1))
        o_ref[rows, :] += jnp.dot(slab_refs[3 * s + 2][rows, :], wd, preferred_element_type=F32)

    up_proj(0)
    for q in range(n_blocks):
        if n_slabs > 1:
            up_proj(1, q)
        conv_act(0, q)
    for s in range(n_slabs):
        wd = wd_ref[slab * s:slab * (s + 1), :].astype(BF16)
        for q in range(n_blocks):
            down_proj(s, q, wd)
            if s + 1 < n_slabs:
                conv_act(s + 1, q)
        if s + 2 < n_slabs:
            for q in range(n_blocks):
                up_proj(s + 2, q)


    @pl.when(f == pl.num_programs(1) - 1)
    def _():
        gate = g_ref[...]
        for r0 in range(0, tm, sub):
            rows = slice(r0, r0 + sub)
            o_ref[rows, :] = x_ref[rows, :] + gate * o_ref[rows, :]
        if final_norm:
            _scale_rows_by_inv_rms(o_ref, r_ref, tm, fnw_ref[...])


def _conv_ffn(name, xs, nw, mod, layer, w_up, conv_w, conv_b, w_down, fnw, is_ctx, final_norm):
    tm, tf = FFN_TM, FFN_TF
    n_rows = xs.shape[0]
    nf = FFN // tf
    hb = tm // HALO
    last_hb = n_rows // HALO - 1
    mrow = _mod_row_fn(is_ctx, tm)

    def mod_spec(chunk):
        return pl.BlockSpec((None, 1, D), lambda i, f: (mrow(i), 0, chunk))

    return pl.pallas_call(
        functools.partial(_ffn_kernel, is_ctx=is_ctx, final_norm=final_norm),
        grid=(n_rows // tm, nf),
        in_specs=[
            pl.BlockSpec((HALO, D), lambda i, f: (jnp.maximum(i * hb - 1, 0), 0)),
            pl.BlockSpec((tm, D), lambda i, f: (i, 0), pipeline_mode=pl.Buffered(1)),
            pl.BlockSpec((HALO, D), lambda i, f: (jnp.minimum((i + 1) * hb, last_hb), 0)),
            pl.BlockSpec((1, D), lambda i, f: (0, 0)),
            mod_spec(3), mod_spec(4), mod_spec(5),
            pl.BlockSpec((None, D, tf), lambda i, f: (layer, 0, f)),
            pl.BlockSpec((None, D, tf), lambda i, f: (layer, 0, nf + f)),
            pl.BlockSpec((None, 3, tf), lambda i, f: (layer, 0, f)),
            pl.BlockSpec((None, 3, tf), lambda i, f: (layer, 0, nf + f)),
            pl.BlockSpec((None, 1, tf), lambda i, f: (layer, 0, f)),
            pl.BlockSpec((None, 1, tf), lambda i, f: (layer, 0, nf + f)),
            pl.BlockSpec((None, tf, D), lambda i, f: (layer, f, 0)),
            pl.BlockSpec((1, D), lambda i, f: (0, 0)),
        ],
        out_specs=pl.BlockSpec((tm, D), lambda i, f: (i, 0)),
        out_shape=jax.ShapeDtypeStruct((n_rows, D), F32),
        scratch_shapes=[
            pltpu.VMEM((tm + 2 * HALO, D), BF16),
            pltpu.VMEM((tm, LANES), F32),
        ] + [
            pltpu.VMEM((tm + 2 * HALO, FFN_SLAB), F32),
            pltpu.VMEM((tm + 2 * HALO, FFN_SLAB), F32),
            pltpu.VMEM((tm, FFN_SLAB), BF16),
        ] * (tf // FFN_SLAB),
        compiler_params=_params("arbitrary", "arbitrary", vmem=VMEM_LIMIT_FFN),
        name=name,
    )(xs, xs, xs, nw, mod, mod, mod, w_up, w_up, conv_w, conv_w, conv_b, conv_b, w_down, fnw)


ATT_TQ = 1024
ATT_UNIT = 128


def _attn_kernel(q_ref, kx_ref, kc_ref, vx_ref, vc_ref, o_ref):
    kx, kc, vx, vc = kx_ref[...], kc_ref[...], vx_ref[...], vc_ref[...]
    for u in range(ATT_TQ // ATT_UNIT):
        r0 = u * ATT_UNIT
        q = q_ref[r0:r0 + ATT_UNIT, :]
        qs = jnp.concatenate([q[:, j * ATT_HD:(j + 1) * ATT_HD] for j in range(ATT_GROUP)], axis=0)
        sx = lax.dot_general(qs, kx, NT_DIMS, preferred_element_type=F32)
        sc = lax.dot_general(qs, kc, NT_DIMS, preferred_element_type=F32)
        m = jnp.maximum(jnp.max(sx, axis=-1, keepdims=True), jnp.max(sc, axis=-1, keepdims=True))
        px = jnp.exp2(sx - m)
        pc = jnp.exp2(sc - m)
        denom = jnp.sum(px, axis=-1, keepdims=True) + jnp.sum(pc, axis=-1, keepdims=True)
        out = jnp.dot(px.astype(BF16), vx, preferred_element_type=F32)
        out = out + jnp.dot(pc.astype(BF16), vc, preferred_element_type=F32)
        out = out * (1.0 / denom)
        for j in range(ATT_GROUP):
            o_ref[r0:r0 + ATT_UNIT, j * ATT_HD:(j + 1) * ATT_HD] = (
                out[j * ATT_UNIT:(j + 1) * ATT_UNIT, :].astype(BF16))


def _attention(qkv_x, kv_c):
    tq = ATT_TQ
    qb = SEQ // tq
    k0 = D // ATT_HD
    v0 = k0 + ATT_KV
    return pl.pallas_call(
        _attn_kernel,
        grid=(B, ATT_KV, qb),
        in_specs=[
            pl.BlockSpec((tq, ATT_GROUP * ATT_HD), lambda b, g, t: (b * qb + t, g)),
            pl.BlockSpec((SEQ, ATT_HD), lambda b, g, t: (b, k0 + g)),
            pl.BlockSpec((CTX, ATT_HD), lambda b, g, t: (b, g)),
            pl.BlockSpec((SEQ, ATT_HD), lambda b, g, t: (b, v0 + g)),
            pl.BlockSpec((CTX, ATT_HD), lambda b, g, t: (b, ATT_KV + g)),
        ],
        out_specs=pl.BlockSpec((tq, ATT_GROUP * ATT_HD), lambda b, g, t: (b * qb + t, g)),
        out_shape=jax.ShapeDtypeStruct((NX, D), BF16),
        compiler_params=_params("arbitrary", "arbitrary", "arbitrary"),
        name="gqa_attention",
    )(qkv_x, qkv_x, kv_c, qkv_x, kv_c)


def _rope_tables(head_dim):
    rows = SEQ // GRID_W
    row = jnp.repeat(jnp.arange(rows, dtype=F32), GRID_W)
    col = jnp.tile(jnp.arange(GRID_W, dtype=F32), rows)
    n_freq = head_dim // 4
    inv = ROPE_THETA ** (-jnp.arange(n_freq, dtype=F32) / n_freq)
    ang = jnp.concatenate([row[:, None] * inv, col[:, None] * inv], axis=-1)
    return jnp.repeat(jnp.cos(ang), 2, axis=-1), jnp.repeat(jnp.sin(ang), 2, axis=-1)


def kernel(x, c, ctx, c_ctx, ada_w, ada_b, norm_w, ret_w_in, ret_w_out, ret_log_decay, ret_gn_w,
           attn_w_in, attn_w_out, attn_q_norm, attn_k_norm, ffn_w_up, ffn_conv_w, ffn_conv_b,
           ffn_w_down, final_norm_w):
    xx = x.reshape(NX, D)
    xc = ctx.reshape(NC, D)
    cmat = jnp.concatenate([c, c_ctx[None, :], jnp.zeros((MOD_ROWS - B - 1, D), F32)], axis=0)
    mod = _modulation(cmat, ada_w, ada_b)
    mod0 = mod[0].reshape(MOD_ROWS, 1, 6 * D)
    mod1 = mod[1].reshape(MOD_ROWS, 1, 6 * D)
    fnw = final_norm_w.reshape(1, D)

    nw = norm_w[0, 0].reshape(1, D)
    tabs = _rope_tables(RET_DK)
    qkvg_x = _in_proj(_RetInEpilogue, "ret_in_x", xx, False, 0, 6 * D, RET_DK, nw, mod0, ret_w_in[0], tabs, [])
    qkvg_c = _in_proj(_RetInEpilogue, "ret_in_ctx", xc, True, 0, 6 * D, RET_DK, nw, mod0, ret_w_in[0], tabs, [])
    yx, yc = _ret_scan(qkvg_x, qkvg_c, ret_log_decay[0], ret_gn_w[0].reshape(1, 2 * D))
    xx = _proj_res("ret_out_x", yx, ret_w_out[0], xx, mod0, False)
    xc = _proj_res("ret_out_ctx", yc, ret_w_out[0], xc, mod0, True)
    nw = norm_w[0, 1].reshape(1, D)
    ffn = (ffn_w_up.astype(BF16), ffn_conv_w, ffn_conv_b.reshape(-1, 1, 2 * FFN), ffn_w_down)
    xx = _conv_ffn("conv_ffn_x", xx, nw, mod0, 0, *ffn, fnw, False, False)
    xc = _conv_ffn("conv_ffn_ctx", xc, nw, mod0, 0, *ffn, fnw, True, False)

    nw = norm_w[1, 0].reshape(1, D)
    tabs = _rope_tables(ATT_HD)
    heads = [attn_q_norm[0].reshape(1, ATT_HD), attn_k_norm[0].reshape(1, ATT_HD)]
    qkv_x = _in_proj(_AttnInEpilogue, "attn_in_x", xx, False, 0, ATT_IN, ATT_HD, nw, mod1, attn_w_in[0],
                     tabs, heads)
    kv_c = _in_proj(_AttnInEpilogue, "attn_in_ctx", xc, True, ATT_Q_TILES, ATT_IN - D, ATT_HD, nw, mod1,
                    attn_w_in[0], tabs, heads)
    ya = _attention(qkv_x, kv_c)
    xx = _proj_res("attn_out", ya, attn_w_out[0], xx, mod1, False)
    out = _conv_ffn("conv_ffn_out", xx, norm_w[1, 1].reshape(1, D), mod1, 1, *ffn, fnw, False, True)
    return out.reshape(B, SEQ, D)
```

```python
import functools
import math

import jax
import jax.numpy as jnp
from jax import lax
from jax.experimental import pallas as pl
from jax.experimental.pallas import tpu as pltpu

D = 2048
B = 4
SEQ = 2048
CTX = 256
GRID_W = 64
RET_HEADS = 8
RET_DK = D // RET_HEADS
RET_DV = 2 * D // RET_HEADS
CHUNK = 256
ATT_HEADS = 16
ATT_KV = 4
ATT_HD = D // ATT_HEADS
ATT_GROUP = ATT_HEADS // ATT_KV
ATT_IN = (ATT_HEADS + 2 * ATT_KV) * ATT_HD
FFN = 256 * ((8 * D // 3 + 255) // 256)
ROPE_THETA = 10000.0
EPS = 1e-6

NX = B * SEQ
NC = B * CTX
CTX_ROW = B
MOD_ROWS = 8

F32 = jnp.float32
BF16 = jnp.bfloat16
BF16_ROWS = 16
LANES = 128

V7X_VMEM_BYTES = 64 * 1024 * 1024
VMEM_LIMIT = V7X_VMEM_BYTES - 8 * 1024 * 1024
VMEM_LIMIT_FFN = V7X_VMEM_BYTES - 4 * 1024 * 1024

NT_DIMS = (((1,), (1,)), ((), ()))
TN_DIMS = (((0,), (0,)), ((), ()))


def _params(*sem, vmem=VMEM_LIMIT):
    return pltpu.CompilerParams(dimension_semantics=sem, vmem_limit_bytes=vmem)


LOG2_E = math.log2(math.e)


def _silu(v):
    return v * (1.0 / (1.0 + jnp.exp2(v * -LOG2_E)))


def _for_row_tiles(n_rows, body, unroll):
    trips = n_rows // BF16_ROWS
    if trips == 1:
        body(0)
    else:
        def step(j, carry):
            body(pl.multiple_of(j * BF16_ROWS, BF16_ROWS))
            return carry
        lax.fori_loop(0, trips, step, 0, unroll=unroll)


def _inv_rms_rows(x_ref, r_ref, n_rows):
    lanes = r_ref.shape[1]
    width = x_ref.shape[1]

    def stats(r):
        x = x_ref[pl.ds(r, BF16_ROWS), :]
        sq = x * x
        part = sq[:, 0:lanes]
        for t in range(1, width // lanes):
            part = part + sq[:, t * lanes:(t + 1) * lanes]
        r_ref[pl.ds(r, BF16_ROWS), :] = part

    _for_row_tiles(n_rows, stats, 4)
    rows = slice(0, n_rows)
    ms = jnp.sum(r_ref[rows, :], axis=-1, keepdims=True) * (1.0 / width)
    r_ref[rows, :] = jnp.broadcast_to(lax.rsqrt(ms + EPS), (n_rows, lanes))


def _row_scale(r_ref, r, width):
    inv = r_ref[pl.ds(r, BF16_ROWS), :]
    return jnp.concatenate([inv] * (width // r_ref.shape[1]), axis=1)


def _norm_rows(x_ref, h_ref, r_ref, h_row0, n_rows, nw, sh, sc):
    gain = nw * (1.0 + sc)
    _inv_rms_rows(x_ref, r_ref, n_rows)

    def apply(r):
        x = x_ref[pl.ds(r, BF16_ROWS), :]
        dst = pl.ds(pl.multiple_of(h_row0 + r, BF16_ROWS), BF16_ROWS)
        h_ref[dst, :] = (x * _row_scale(r_ref, r, x.shape[1]) * gain + sh).astype(BF16)

    _for_row_tiles(n_rows, apply, 2)


def _scale_rows_by_inv_rms(o_ref, r_ref, n_rows, w):
    _inv_rms_rows(o_ref, r_ref, n_rows)

    def apply(r):
        rows = pl.ds(r, BF16_ROWS)
        x = o_ref[rows, :]
        o_ref[rows, :] = x * _row_scale(r_ref, r, x.shape[1]) * w

    _for_row_tiles(n_rows, apply, 2)


def _mod_row_fn(is_ctx, tm):
    if is_ctx:
        return lambda i: CTX_ROW
    return lambda i: i // (SEQ // tm)


def _rope_coeffs(cos, sin):
    even = lax.broadcasted_iota(jnp.int32, cos.shape, 1) % 2 == 0
    return cos, jnp.where(even, -sin, 0.0), jnp.where(even, 0.0, sin)


def _rope(seg, cos, sa, sb):
    hd = seg.shape[-1]
    return seg * cos + pltpu.roll(seg, hd - 1, 1) * sa + pltpu.roll(seg, 1, 1) * sb


MOD_TN = 1024


def _mod_kernel(c_ref, w_ref, b_ref, o_ref):
    a = _silu(c_ref[...]).astype(BF16)
    o_ref[...] = jnp.dot(a, w_ref[...].astype(BF16), preferred_element_type=F32) + b_ref[...]


def _modulation(cmat, ada_w, ada_b):
    depth = ada_w.shape[0]
    return pl.pallas_call(
        _mod_kernel,
        grid=(depth, 6 * D // MOD_TN),
        in_specs=[
            pl.BlockSpec((MOD_ROWS, D), lambda l, n: (0, 0)),
            pl.BlockSpec((None, D, MOD_TN), lambda l, n: (l, 0, n)),
            pl.BlockSpec((None, 1, MOD_TN), lambda l, n: (l, 0, n)),
        ],
        out_specs=pl.BlockSpec((None, MOD_ROWS, MOD_TN), lambda l, n: (l, 0, n)),
        out_shape=jax.ShapeDtypeStruct((depth, MOD_ROWS, 6 * D), F32),
        compiler_params=_params("arbitrary", "arbitrary"),
        name="adaln_mod",
    )(cmat, ada_w, ada_b.reshape(depth, 1, 6 * D))


IN_TM = 1024
IN_TN = 1024


IN_PIECES = 4


class _RetInEpilogue:
    @staticmethod
    def kinds(n, is_ctx):
        if is_ctx:
            return [("plain", None)]
        is_qk = n < 2 * D // IN_TN
        return [("rope", is_qk), ("plain", jnp.logical_not(is_qk))]

    @staticmethod
    def apply(kind, acc, rows, n, o_ref, cos_ref, sin_ref, extra):
        is_k = jnp.logical_and(n >= D // IN_TN, n < 2 * D // IN_TN)
        kscale = jnp.where(is_k, RET_DK ** -0.5, 1.0).astype(F32)
        if kind == "plain":
            o_ref[rows, :] = (acc * kscale).astype(BF16)
            return
        tabs = _rope_coeffs(cos_ref[rows, :], sin_ref[rows, :])
        for s in range(IN_TN // RET_DK):
            cols = slice(s * RET_DK, (s + 1) * RET_DK)
            o_ref[rows, cols] = _rope(acc[:, cols] * kscale, *tabs).astype(BF16)


ATT_QSCALE = ATT_HD ** -0.5 * LOG2_E
ATT_Q_TILES = D // IN_TN


class _AttnInEpilogue:
    @staticmethod
    def kinds(n, is_ctx):
        if is_ctx:
            return [("kv_ctx", None)]
        is_q = n < ATT_Q_TILES
        return [("q", is_q), ("kv", jnp.logical_not(is_q))]

    @staticmethod
    def apply(kind, acc, rows, n, o_ref, cos_ref, sin_ref, extra):
        qn_ref, kn_ref = extra
        heads = IN_TN // ATT_HD
        normed = heads if kind == "q" else ATT_KV
        hw = qn_ref[...] if kind == "q" else kn_ref[...]
        scale = ATT_QSCALE if kind == "q" else 1.0
        if kind != "kv_ctx":
            tabs = _rope_coeffs(cos_ref[rows, :], sin_ref[rows, :])
        for s in range(normed):
            cols = slice(s * ATT_HD, (s + 1) * ATT_HD)
            seg = acc[:, cols]
            ms = jnp.mean(seg * seg, axis=-1, keepdims=True)
            head = seg * (lax.rsqrt(ms + EPS) * scale) * hw
            if kind != "kv_ctx":
                head = _rope(head, *tabs)
            o_ref[rows, cols] = head.astype(BF16)
        if normed < heads:
            cols = slice(normed * ATT_HD, heads * ATT_HD)
            o_ref[rows, cols] = acc[:, cols].astype(BF16)


def _in_proj_kernel(x_ref, nw_ref, sh_ref, sc_ref, w_ref, cos_ref, sin_ref, *rest,
                    epilogue, is_ctx, n_off, n_col_tiles, n_steps):
    *extra, o_ref, h_ref, r_ref, acc_a, acc_b = rest
    s = pl.program_id(0)

    @pl.when(jnp.logical_and(s % n_col_tiles == 0, s < n_steps - 1))
    def _():
        _norm_rows(x_ref, h_ref, r_ref, 0, IN_TM, nw_ref[...], sh_ref[...], sc_ref[...])

    @pl.when(s == 0)
    def _():
        acc_b[...] = jnp.zeros_like(acc_b)

    n_prev = n_off + jnp.maximum(s - 1, 0) % n_col_tiles
    pr = IN_TM // IN_PIECES

    def run(acc_w, acc_r, kind):
        w = w_ref[...].astype(BF16)
        for p in range(IN_PIECES):
            rows = slice(p * pr, (p + 1) * pr)
            acc_w[rows, :] = jnp.dot(h_ref[rows, :], w, preferred_element_type=F32)
            epilogue.apply(kind, acc_r[rows, :], rows, n_prev, o_ref, cos_ref, sin_ref, extra)

    for parity, (acc_w, acc_r) in enumerate(((acc_a, acc_b), (acc_b, acc_a))):
        for kind, cond in epilogue.kinds(n_prev, is_ctx):
            pred = s % 2 == parity
            if cond is not None:
                pred = jnp.logical_and(pred, cond)
            pl.when(pred)(functools.partial(run, acc_w, acc_r, kind))


def _in_proj(epilogue, name, src, is_ctx, n_off, n_cols, hd, nw, mod, w, tabs, extra):
    tm, tn = IN_TM, IN_TN
    tps = SEQ // tm
    n_row_tiles = src.shape[0] // tm
    nct = n_cols // tn
    n_steps = n_row_tiles * nct + 1
    mrow = _mod_row_fn(is_ctx, tm)

    def row_tile(s):
        return jnp.minimum(s // nct, n_row_tiles - 1)

    def prev(s):
        t = jnp.maximum(s - 1, 0)
        return t // nct, t % nct

    tab_spec = pl.BlockSpec((tm, hd), lambda s: (prev(s)[0] % tps, 0))
    return pl.pallas_call(
        functools.partial(_in_proj_kernel, epilogue=epilogue, is_ctx=is_ctx, n_off=n_off,
                          n_col_tiles=nct, n_steps=n_steps),
        grid=(n_steps,),
        in_specs=[
            pl.BlockSpec((tm, D), lambda s: (row_tile(s), 0), pipeline_mode=pl.Buffered(1)),
            pl.BlockSpec((1, D), lambda s: (0, 0)),
            pl.BlockSpec((None, 1, D), lambda s: (mrow(row_tile(s)), 0, 0)),
            pl.BlockSpec((None, 1, D), lambda s: (mrow(row_tile(s)), 0, 1)),
            pl.BlockSpec((D, tn), lambda s: (0, n_off + s % nct)),
            tab_spec, tab_spec,
        ] + [pl.BlockSpec(e.shape, lambda s: (0, 0)) for e in extra],
        out_specs=pl.BlockSpec((tm, tn), lambda s: prev(s)),
        out_shape=jax.ShapeDtypeStruct((src.shape[0], n_cols), BF16),
        scratch_shapes=[pltpu.VMEM((tm, D), BF16), pltpu.VMEM((tm, LANES), F32),
                        pltpu.VMEM((tm, tn), F32), pltpu.VMEM((tm, tn), F32)],
        compiler_params=_params("arbitrary"),
        name=name,
    )(src, nw, mod, mod, w, *tabs, *extra)


def _ret_scan_kernel(ld_ref, qx_ref, kx_ref, vx_ref, gx_ref, qc_ref, kc_ref, vc_ref, gc_ref, gnw_ref,
                     yx_ref, yc_ref, o_ref, sf_ref, sb_ref):
    h = pl.program_id(1)
    c = CHUNK
    nn = lax.broadcasted_iota(jnp.int32, (c, c), 0).astype(F32)
    mm = lax.broadcasted_iota(jnp.int32, (c, c), 1).astype(F32)
    idx = lax.broadcasted_iota(jnp.int32, (c, 1), 0).astype(F32)

    def log_gamma(direction, shape):
        return -jnp.exp(jnp.full(shape, ld_ref[direction, h], F32))

    lgf, lgb = log_gamma(0, (c, c)), log_gamma(1, (c, c))
    lgf1, lgb1 = log_gamma(0, (c, 1)), log_gamma(1, (c, 1))
    lgf0, lgb0 = log_gamma(0, (1, 1)), log_gamma(1, (1, 1))
    fwd = (sf_ref,
           jnp.where(nn >= mm, jnp.exp(lgf * jnp.where(nn >= mm, nn - mm, 0.0)), 0.0),
           jnp.exp(lgf1 * (idx + 1.0)), jnp.exp(lgf1 * (c - 1.0 - idx)), jnp.exp(lgf0 * c))
    bwd = (sb_ref,
           jnp.where(mm >= nn, jnp.exp(lgb * jnp.where(mm >= nn, mm - nn, 0.0)), 0.0),
           jnp.exp(lgb1 * (c - idx)), jnp.exp(lgb1 * idx), jnp.exp(lgb0 * c))
    gnw = gnw_ref[...]

    def chunk_step(direction, q_ref, k_ref, v_ref, rows):
        st_ref, decay, qd, kd, cd = direction
        q, k, v = q_ref[rows, :], k_ref[rows, :], v_ref[rows, :]
        state = st_ref[...]
        scores = lax.dot_general(q, k, NT_DIMS, preferred_element_type=F32) * decay
        out = jnp.dot(scores.astype(BF16), v, preferred_element_type=F32)
        out = out + jnp.dot(q, state.astype(BF16), preferred_element_type=F32) * qd
        kdec = (k.astype(F32) * kd).astype(BF16)
        st_ref[...] = state * cd + lax.dot_general(kdec, v, TN_DIMS, preferred_element_type=F32)
        return out

    def finish(tot, g_ref, y_ref, rows):
        mu = jnp.mean(tot, axis=-1, keepdims=True)
        cen = tot - mu
        var = jnp.mean(cen * cen, axis=-1, keepdims=True)
        yn = cen * lax.rsqrt(var + EPS) * gnw
        y_ref[rows, :] = (_silu(g_ref[rows, :].astype(F32)) * yn).astype(BF16)

    def scan(q_ref, k_ref, v_ref, g_ref, y_ref, base, n_chunks):
        def rows_of(j):
            r = pl.multiple_of(j * c, c)
            return pl.ds(r, c), pl.ds(pl.multiple_of(base + r, c), c)

        def first_half(j, carry):
            for direction, jj in ((fwd, j), (bwd, n_chunks - 1 - j)):
                rows, orows = rows_of(jj)
                o_ref[orows, :] = chunk_step(direction, q_ref, k_ref, v_ref, rows)
            return carry

        def second_half(j, carry):
            for direction, jj in ((fwd, j), (bwd, n_chunks - 1 - j)):
                rows, orows = rows_of(jj)
                tot = chunk_step(direction, q_ref, k_ref, v_ref, rows) + o_ref[orows, :]
                finish(tot, g_ref, y_ref, rows)
            return carry

        if n_chunks == 1:
            rows = pl.ds(0, c)
            tot = chunk_step(fwd, q_ref, k_ref, v_ref, rows) + chunk_step(bwd, q_ref, k_ref, v_ref, rows)
            finish(tot, g_ref, y_ref, rows)
        else:
            lax.fori_loop(0, n_chunks // 2, first_half, 0)
            lax.fori_loop(n_chunks // 2, n_chunks, second_half, 0)

    sf_ref[...] = jnp.zeros_like(sf_ref)
    sb_ref[...] = jnp.zeros_like(sb_ref)
    scan(qc_ref, kc_ref, vc_ref, gc_ref, yc_ref, SEQ, CTX // c)
    scan(qx_ref, kx_ref, vx_ref, gx_ref, yx_ref, 0, SEQ // c)


def _ret_scan(qkvg_x, qkvg_c, log_decay, gn_w):
    hq = D // RET_DK
    hv = 2 * D // RET_DV

    def specs(rows):
        return [
            pl.BlockSpec((rows, RET_DK), lambda b, h: (b, h)),
            pl.BlockSpec((rows, RET_DK), lambda b, h: (b, hq + h)),
            pl.BlockSpec((rows, RET_DV), lambda b, h: (b, hv + h)),
            pl.BlockSpec((rows, RET_DV), lambda b, h: (b, 2 * hv + h)),
        ]

    state = pltpu.VMEM((RET_DK, RET_DV), F32)
    return pl.pallas_call(
        _ret_scan_kernel,
        grid=(B, RET_HEADS),
        in_specs=[pl.BlockSpec(memory_space=pltpu.SMEM)] + specs(SEQ) + specs(CTX)
                 + [pl.BlockSpec((1, RET_DV), lambda b, h: (0, h))],
        out_specs=[
            pl.BlockSpec((SEQ, RET_DV), lambda b, h: (b, h)),
            pl.BlockSpec((CTX, RET_DV), lambda b, h: (b, h)),
        ],
        out_shape=[jax.ShapeDtypeStruct((NX, 2 * D), BF16), jax.ShapeDtypeStruct((NC, 2 * D), BF16)],
        scratch_shapes=[pltpu.VMEM((SEQ + CTX, RET_DV), F32), state, state],
        compiler_params=_params("arbitrary", "arbitrary"),
        name="ret_scan",
    )(log_decay, qkvg_x, qkvg_x, qkvg_x, qkvg_x, qkvg_c, qkvg_c, qkvg_c, qkvg_c, gn_w)


PROJ_TM = 1024
PROJ_TN = 512


def _proj_res_kernel(y_ref, w_ref, x_ref, g_ref, o_ref):
    acc = jnp.dot(y_ref[...], w_ref[...].astype(BF16), preferred_element_type=F32)
    o_ref[...] = x_ref[...] + g_ref[...] * acc


def _proj_res(name, y, w, res, mod, is_ctx):
    tm, tn = PROJ_TM, PROJ_TN
    rows, ky = y.shape
    mrow = _mod_row_fn(is_ctx, tm)
    gate0 = 2 * D // tn
    return pl.pallas_call(
        _proj_res_kernel,
        grid=(rows // tm, D // tn),
        in_specs=[
            pl.BlockSpec((tm, ky), lambda i, n: (i, 0)),
            pl.BlockSpec((ky, tn), lambda i, n: (0, n)),
            pl.BlockSpec((tm, tn), lambda i, n: (i, n)),
            pl.BlockSpec((None, 1, tn), lambda i, n: (mrow(i), 0, gate0 + n)),
        ],
        out_specs=pl.BlockSpec((tm, tn), lambda i, n: (i, n)),
        out_shape=jax.ShapeDtypeStruct((rows, D), F32),
        compiler_params=_params("arbitrary", "arbitrary"),
        name=name,
    )(y, w, res, mod)


FFN_TM = 1024
FFN_TF = 512
FFN_SUB = 256
FFN_SLAB = 256
HALO = BF16_ROWS


def _ffn_kernel(xp_ref, x_ref, xn_ref, nw_ref, sh_ref, sc_ref, g_ref, wa_ref, wb_ref,
                cwa_ref, cwb_ref, cba_ref, cbb_ref, wd_ref, fnw_ref, o_ref,
                h_ref, r_ref, *slab_refs, is_ctx, final_norm):
    tm = FFN_TM
    i = pl.program_id(0)
    f = pl.program_id(1)

    @pl.when(f == 0)
    def _():
        nw, sh, sc = nw_ref[...], sh_ref[...], sc_ref[...]
        _norm_rows(xp_ref, h_ref, r_ref, 0, HALO, nw, sh, sc)
        _norm_rows(x_ref, h_ref, r_ref, HALO, tm, nw, sh, sc)
        _norm_rows(xn_ref, h_ref, r_ref, HALO + tm, HALO, nw, sh, sc)
        o_ref[...] = jnp.zeros_like(o_ref)

    sub, slab = FFN_SUB, FFN_SLAB
    n_slabs = FFN_TF // slab
    n_blocks = tm // sub
    *u_refs, act_ref = slab_refs
    bounds = [0] + [2 * HALO + sub * (q + 1) for q in range(n_blocks - 1)] + [tm + 2 * HALO]

    def up_proj(s, q=None):
        rows = slice(0, tm + 2 * HALO) if q is None else slice(bounds[q], bounds[q + 1])
        cols = slice(slab * s, slab * (s + 1))
        hh = h_ref[rows, :]
        u_refs[2 * s][rows, :] = jnp.dot(hh, wa_ref[:, cols], preferred_element_type=F32)
        u_refs[2 * s + 1][rows, :] = jnp.dot(hh, wb_ref[:, cols], preferred_element_type=F32)

    def clear_outside_rows(s):
        tiles_per_seq = SEQ // tm
        at_start = i % tiles_per_seq == 0
        at_end = i % tiles_per_seq == tiles_per_seq - 1
        before = slice(HALO - 8, HALO)
        after = slice(HALO + tm, HALO + tm + 8)
        for u_ref in u_refs[2 * s:2 * s + 2]:
            u_ref[before, :] = jnp.where(at_start, 0.0, u_ref[before, :])
            u_ref[after, :] = jnp.where(at_end, 0.0, u_ref[after, :])

    def conv_act(s, q):
        cols = slice(slab * s, slab * (s + 1))
        r0 = sub * q
        lo = HALO + r0
        if is_ctx:
            pos = (r0 + lax.broadcasted_iota(jnp.int32, (sub, 1), 0)) % CTX
            has_prev = pos > 0
            has_next = pos < CTX - 1

        def conv(u_ref, cw_ref, cb_ref):
            prev = u_ref[lo - 1:lo - 1 + sub, :]
            cur = u_ref[lo:lo + sub, :]
            nxt = u_ref[lo + 1:lo + 1 + sub, :]
            if is_ctx:
                prev = jnp.where(has_prev, prev, 0.0)
                nxt = jnp.where(has_next, nxt, 0.0)
            return (prev * cw_ref[0:1, cols] + cur * cw_ref[1:2, cols] + nxt * cw_ref[2:3, cols]
                    + cb_ref[:, cols])

        a = conv(u_refs[2 * s], cwa_ref, cba_ref)
        b = conv(u_refs[2 * s + 1], cwb_ref, cbb_ref)
        act_ref[r0:r0 + sub, cols] = (_silu(a) * b).astype(BF16)

    up_proj(0)
    if not is_ctx:
        clear_outside_rows(0)
    for s in range(1, n_slabs):
        for q in range(n_blocks):
            up_proj(s, q)
            conv_act(s - 1, q)
        if not is_ctx:
            clear_outside_rows(s)
    wd = wd_ref[...].astype(BF16)
    for q in range(n_blocks):
        conv_act(n_slabs - 1, q)
        rows = slice(sub * q, sub * (q + 1))
        o_ref[rows, :] += jnp.dot(act_ref[rows, :], wd, preferred_element_type=F32)


    @pl.when(f == pl.num_programs(1) - 1)
    def _():
        gate = g_ref[...]
        for r0 in range(0, tm, sub):
            rows = slice(r0, r0 + sub)
            o_ref[rows, :] = x_ref[rows, :] + gate * o_ref[rows, :]
        if final_norm:
            _scale_rows_by_inv_rms(o_ref, r_ref, tm, fnw_ref[...])


def _conv_ffn(name, xs, nw, mod, layer, w_up, conv_w, conv_b, w_down, fnw, is_ctx, final_norm):
    tm, tf = FFN_TM, FFN_TF
    n_rows = xs.shape[0]
    nf = FFN // tf
    hb = tm // HALO
    last_hb = n_rows // HALO - 1
    mrow = _mod_row_fn(is_ctx, tm)

    def mod_spec(chunk):
        return pl.BlockSpec((None, 1, D), lambda i, f: (mrow(i), 0, chunk))

    return pl.pallas_call(
        functools.partial(_ffn_kernel, is_ctx=is_ctx, final_norm=final_norm),
        grid=(n_rows // tm, nf),
        in_specs=[
            pl.BlockSpec((HALO, D), lambda i, f: (jnp.maximum(i * hb - 1, 0), 0)),
            pl.BlockSpec((tm, D), lambda i, f: (i, 0), pipeline_mode=pl.Buffered(1)),
            pl.BlockSpec((HALO, D), lambda i, f: (jnp.minimum((i + 1) * hb, last_hb), 0)),
            pl.BlockSpec((1, D), lambda i, f: (0, 0)),
            mod_spec(3), mod_spec(4), mod_spec(5),
            pl.BlockSpec((None, D, tf), lambda i, f: (layer, 0, f)),
            pl.BlockSpec((None, D, tf), lambda i, f: (layer, 0, nf + f)),
            pl.BlockSpec((None, 3, tf), lambda i, f: (layer, 0, f)),
            pl.BlockSpec((None, 3, tf), lambda i, f: (layer, 0, nf + f)),
            pl.BlockSpec((None, 1, tf), lambda i, f: (layer, 0, f)),
            pl.BlockSpec((None, 1, tf), lambda i, f: (layer, 0, nf + f)),
            pl.BlockSpec((None, tf, D), lambda i, f: (layer, f, 0)),
            pl.BlockSpec((1, D), lambda i, f: (0, 0)),
        ],
        out_specs=pl.BlockSpec((tm, D), lambda i, f: (i, 0)),
        out_shape=jax.ShapeDtypeStruct((n_rows, D), F32),
        scratch_shapes=[
            pltpu.VMEM((tm + 2 * HALO, D), BF16),
            pltpu.VMEM((tm, LANES), F32),
        ] + [
            pltpu.VMEM((tm + 2 * HALO, FFN_SLAB), F32),
            pltpu.VMEM((tm + 2 * HALO, FFN_SLAB), F32),
        ] * (tf // FFN_SLAB) + [
            pltpu.VMEM((tm, tf), BF16),
        ],
        compiler_params=_params("arbitrary", "arbitrary", vmem=VMEM_LIMIT_FFN),
        name=name,
    )(xs, xs, xs, nw, mod, mod, mod, w_up, w_up, conv_w, conv_w, conv_b, conv_b, w_down, fnw)


ATT_TQ = 1024
ATT_UNIT = 128


def _attn_kernel(q_ref, kx_ref, kc_ref, vx_ref, vc_ref, o_ref):
    kx, kc, vx, vc = kx_ref[...], kc_ref[...], vx_ref[...], vc_ref[...]
    for u in range(ATT_TQ // ATT_UNIT):
        r0 = u * ATT_UNIT
        q = q_ref[r0:r0 + ATT_UNIT, :]
        qs = jnp.concatenate([q[:, j * ATT_HD:(j + 1) * ATT_HD] for j in range(ATT_GROUP)], axis=0)
        sx = lax.dot_general(qs, kx, NT_DIMS, preferred_element_type=F32)
        sc = lax.dot_general(qs, kc, NT_DIMS, preferred_element_type=F32)
        m = jnp.maximum(jnp.max(sx, axis=-1, keepdims=True), jnp.max(sc, axis=-1, keepdims=True))
        px = jnp.exp2(sx - m)
        pc = jnp.exp2(sc - m)
        denom = jnp.sum(px, axis=-1, keepdims=True) + jnp.sum(pc, axis=-1, keepdims=True)
        out = jnp.dot(px.astype(BF16), vx, preferred_element_type=F32)
        out = out + jnp.dot(pc.astype(BF16), vc, preferred_element_type=F32)
        out = out * (1.0 / denom)
        for j in range(ATT_GROUP):
            o_ref[r0:r0 + ATT_UNIT, j * ATT_HD:(j + 1) * ATT_HD] = (
                out[j * ATT_UNIT:(j + 1) * ATT_UNIT, :].astype(BF16))


def _attention(qkv_x, kv_c):
    tq = ATT_TQ
    qb = SEQ // tq
    k0 = D // ATT_HD
    v0 = k0 + ATT_KV
    return pl.pallas_call(
        _attn_kernel,
        grid=(B, ATT_KV, qb),
        in_specs=[
            pl.BlockSpec((tq, ATT_GROUP * ATT_HD), lambda b, g, t: (b * qb + t, g)),
            pl.BlockSpec((SEQ, ATT_HD), lambda b, g, t: (b, k0 + g)),
            pl.BlockSpec((CTX, ATT_HD), lambda b, g, t: (b, g)),
            pl.BlockSpec((SEQ, ATT_HD), lambda b, g, t: (b, v0 + g)),
            pl.BlockSpec((CTX, ATT_HD), lambda b, g, t: (b, ATT_KV + g)),
        ],
        out_specs=pl.BlockSpec((tq, ATT_GROUP * ATT_HD), lambda b, g, t: (b * qb + t, g)),
        out_shape=jax.ShapeDtypeStruct((NX, D), BF16),
        compiler_params=_params("arbitrary", "arbitrary", "arbitrary"),
        name="gqa_attention",
    )(qkv_x, qkv_x, kv_c, qkv_x, kv_c)


def _rope_tables(head_dim):
    rows = SEQ // GRID_W
    row = jnp.repeat(jnp.arange(rows, dtype=F32), GRID_W)
    col = jnp.tile(jnp.arange(GRID_W, dtype=F32), rows)
    n_freq = head_dim // 4
    inv = ROPE_THETA ** (-jnp.arange(n_freq, dtype=F32) / n_freq)
    ang = jnp.concatenate([row[:, None] * inv, col[:, None] * inv], axis=-1)
    return jnp.repeat(jnp.cos(ang), 2, axis=-1), jnp.repeat(jnp.sin(ang), 2, axis=-1)


def kernel(x, c, ctx, c_ctx, ada_w, ada_b, norm_w, ret_w_in, ret_w_out, ret_log_decay, ret_gn_w,
           attn_w_in, attn_w_out, attn_q_norm, attn_k_norm, ffn_w_up, ffn_conv_w, ffn_conv_b,
           ffn_w_down, final_norm_w):
    xx = x.reshape(NX, D)
    xc = ctx.reshape(NC, D)
    cmat = jnp.concatenate([c, c_ctx[None, :], jnp.zeros((MOD_ROWS - B - 1, D), F32)], axis=0)
    mod = _modulation(cmat, ada_w, ada_b)
    mod0 = mod[0].reshape(MOD_ROWS, 1, 6 * D)
    mod1 = mod[1].reshape(MOD_ROWS, 1, 6 * D)
    fnw = final_norm_w.reshape(1, D)

    nw = norm_w[0, 0].reshape(1, D)
    tabs = _rope_tables(RET_DK)
    qkvg_x = _in_proj(_RetInEpilogue, "ret_in_x", xx, False, 0, 6 * D, RET_DK, nw, mod0, ret_w_in[0], tabs, [])
    qkvg_c = _in_proj(_RetInEpilogue, "ret_in_ctx", xc, True, 0, 6 * D, RET_DK, nw, mod0, ret_w_in[0], tabs, [])
    yx, yc = _ret_scan(qkvg_x, qkvg_c, ret_log_decay[0], ret_gn_w[0].reshape(1, 2 * D))
    xx = _proj_res("ret_out_x", yx, ret_w_out[0], xx, mod0, False)
    xc = _proj_res("ret_out_ctx", yc, ret_w_out[0], xc, mod0, True)
    nw = norm_w[0, 1].reshape(1, D)
    ffn = (ffn_w_up.astype(BF16), ffn_conv_w, ffn_conv_b.reshape(-1, 1, 2 * FFN), ffn_w_down)
    xx = _conv_ffn("conv_ffn_x", xx, nw, mod0, 0, *ffn, fnw, False, False)
    xc = _conv_ffn("conv_ffn_ctx", xc, nw, mod0, 0, *ffn, fnw, True, False)

    nw = norm_w[1, 0].reshape(1, D)
    tabs = _rope_tables(ATT_HD)
    heads = [attn_q_norm[0].reshape(1, ATT_HD), attn_k_norm[0].reshape(1, ATT_HD)]
    qkv_x = _in_proj(_AttnInEpilogue, "attn_in_x", xx, False, 0, ATT_IN, ATT_HD, nw, mod1, attn_w_in[0],
                     tabs, heads)
    kv_c = _in_proj(_AttnInEpilogue, "attn_in_ctx", xc, True, ATT_Q_TILES, ATT_IN - D, ATT_HD, nw, mod1,
                    attn_w_in[0], tabs, heads)
    ya = _attention(qkv_x, kv_c)
    xx = _proj_res("attn_out", ya, attn_w_out[0], xx, mod1, False)
    out = _conv_ffn("conv_ffn_out", xx, norm_w[1, 1].reshape(1, D), mod1, 1, *ffn, fnw, False, True)
    return out.reshape(B, SEQ, D)
```

```python
import functools
import math

import jax
import jax.numpy as jnp
from jax import lax
from jax.experimental import pallas as pl
from jax.experimental.pallas import tpu as pltpu

D = 2048
B = 4
SEQ = 2048
CTX = 256
GRID_W = 64
RET_HEADS = 8
RET_DK = D // RET_HEADS
RET_DV = 2 * D // RET_HEADS
CHUNK = 256
ATT_HEADS = 16
ATT_KV = 4
ATT_HD = D // ATT_HEADS
ATT_GROUP = ATT_HEADS // ATT_KV
ATT_IN = (ATT_HEADS + 2 * ATT_KV) * ATT_HD
FFN = 256 * ((8 * D // 3 + 255) // 256)
ROPE_THETA = 10000.0
EPS = 1e-6

NX = B * SEQ
NC = B * CTX
CTX_ROW = B
MOD_ROWS = 8

F32 = jnp.float32
BF16 = jnp.bfloat16
BF16_ROWS = 16
LANES = 128

V7X_VMEM_BYTES = 64 * 1024 * 1024
VMEM_LIMIT = V7X_VMEM_BYTES - 8 * 1024 * 1024
VMEM_LIMIT_FFN = V7X_VMEM_BYTES - 4 * 1024 * 1024

NT_DIMS = (((1,), (1,)), ((), ()))
TN_DIMS = (((0,), (0,)), ((), ()))


def _params(*sem, vmem=VMEM_LIMIT):
    return pltpu.CompilerParams(dimension_semantics=sem, vmem_limit_bytes=vmem)


LOG2_E = math.log2(math.e)


def _silu(v):
    return v * (1.0 / (1.0 + jnp.exp2(v * -LOG2_E)))


def _for_row_tiles(n_rows, body, unroll):
    trips = n_rows // BF16_ROWS
    if trips == 1:
        body(0)
    else:
        def step(j, carry):
            body(pl.multiple_of(j * BF16_ROWS, BF16_ROWS))
            return carry
        lax.fori_loop(0, trips, step, 0, unroll=unroll)


def _inv_rms_rows(x_ref, r_ref, n_rows):
    lanes = r_ref.shape[1]
    width = x_ref.shape[1]

    def stats(r):
        x = x_ref[pl.ds(r, BF16_ROWS), :]
        sq = x * x
        part = sq[:, 0:lanes]
        for t in range(1, width // lanes):
            part = part + sq[:, t * lanes:(t + 1) * lanes]
        r_ref[pl.ds(r, BF16_ROWS), :] = part

    _for_row_tiles(n_rows, stats, 4)
    rows = slice(0, n_rows)
    ms = jnp.sum(r_ref[rows, :], axis=-1, keepdims=True) * (1.0 / width)
    r_ref[rows, :] = jnp.broadcast_to(lax.rsqrt(ms + EPS), (n_rows, lanes))


def _row_scale(r_ref, r, width):
    inv = r_ref[pl.ds(r, BF16_ROWS), :]
    return jnp.concatenate([inv] * (width // r_ref.shape[1]), axis=1)


def _norm_rows(x_ref, h_ref, r_ref, h_row0, n_rows, nw, sh, sc):
    gain = nw * (1.0 + sc)
    _inv_rms_rows(x_ref, r_ref, n_rows)

    def apply(r):
        x = x_ref[pl.ds(r, BF16_ROWS), :]
        dst = pl.ds(pl.multiple_of(h_row0 + r, BF16_ROWS), BF16_ROWS)
        h_ref[dst, :] = (x * _row_scale(r_ref, r, x.shape[1]) * gain + sh).astype(BF16)

    _for_row_tiles(n_rows, apply, 2)


def _scale_rows_by_inv_rms(o_ref, r_ref, n_rows, w):
    _inv_rms_rows(o_ref, r_ref, n_rows)

    def apply(r):
        rows = pl.ds(r, BF16_ROWS)
        x = o_ref[rows, :]
        o_ref[rows, :] = x * _row_scale(r_ref, r, x.shape[1]) * w

    _for_row_tiles(n_rows, apply, 2)


def _mod_row_fn(is_ctx, tm):
    if is_ctx:
        return lambda i: CTX_ROW
    return lambda i: i // (SEQ // tm)


def _rope_coeffs(cos, sin):
    even = lax.broadcasted_iota(jnp.int32, cos.shape, 1) % 2 == 0
    return cos, jnp.where(even, -sin, 0.0), jnp.where(even, 0.0, sin)


def _rope(seg, cos, sa, sb):
    hd = seg.shape[-1]
    return seg * cos + pltpu.roll(seg, hd - 1, 1) * sa + pltpu.roll(seg, 1, 1) * sb


MOD_TN = 1024


def _mod_kernel(c_ref, w_ref, b_ref, o_ref):
    a = _silu(c_ref[...]).astype(BF16)
    o_ref[...] = jnp.dot(a, w_ref[...].astype(BF16), preferred_element_type=F32) + b_ref[...]


def _modulation(cmat, ada_w, ada_b):
    depth = ada_w.shape[0]
    return pl.pallas_call(
        _mod_kernel,
        grid=(depth, 6 * D // MOD_TN),
        in_specs=[
            pl.BlockSpec((MOD_ROWS, D), lambda l, n: (0, 0)),
            pl.BlockSpec((None, D, MOD_TN), lambda l, n: (l, 0, n)),
            pl.BlockSpec((None, 1, MOD_TN), lambda l, n: (l, 0, n)),
        ],
        out_specs=pl.BlockSpec((None, MOD_ROWS, MOD_TN), lambda l, n: (l, 0, n)),
        out_shape=jax.ShapeDtypeStruct((depth, MOD_ROWS, 6 * D), F32),
        compiler_params=_params("arbitrary", "arbitrary"),
        name="adaln_mod",
    )(cmat, ada_w, ada_b.reshape(depth, 1, 6 * D))


IN_TM = 1024
IN_TN = 1024


IN_PIECES = 4


class _RetInEpilogue:
    @staticmethod
    def kinds(n, is_ctx):
        if is_ctx:
            return [("plain", None)]
        is_qk = n < 2 * D // IN_TN
        return [("rope", is_qk), ("plain", jnp.logical_not(is_qk))]

    @staticmethod
    def apply(kind, acc, rows, n, o_ref, cos_ref, sin_ref, extra):
        is_k = jnp.logical_and(n >= D // IN_TN, n < 2 * D // IN_TN)
        kscale = jnp.where(is_k, RET_DK ** -0.5, 1.0).astype(F32)
        if kind == "plain":
            o_ref[rows, :] = (acc * kscale).astype(BF16)
            return
        tabs = _rope_coeffs(cos_ref[rows, :], sin_ref[rows, :])
        for s in range(IN_TN // RET_DK):
            cols = slice(s * RET_DK, (s + 1) * RET_DK)
            o_ref[rows, cols] = _rope(acc[:, cols] * kscale, *tabs).astype(BF16)


ATT_QSCALE = ATT_HD ** -0.5 * LOG2_E
ATT_Q_TILES = D // IN_TN


class _AttnInEpilogue:
    @staticmethod
    def kinds(n, is_ctx):
        if is_ctx:
            return [("kv_ctx", None)]
        is_q = n < ATT_Q_TILES
        return [("q", is_q), ("kv", jnp.logical_not(is_q))]

    @staticmethod
    def apply(kind, acc, rows, n, o_ref, cos_ref, sin_ref, extra):
        qn_ref, kn_ref = extra
        heads = IN_TN // ATT_HD
        normed = heads if kind == "q" else ATT_KV
        hw = qn_ref[...] if kind == "q" else kn_ref[...]
        scale = ATT_QSCALE if kind == "q" else 1.0
        if kind != "kv_ctx":
            tabs = _rope_coeffs(cos_ref[rows, :], sin_ref[rows, :])
        for s in range(normed):
            cols = slice(s * ATT_HD, (s + 1) * ATT_HD)
            seg = acc[:, cols]
            ms = jnp.mean(seg * seg, axis=-1, keepdims=True)
            head = seg * (lax.rsqrt(ms + EPS) * scale) * hw
            if kind != "kv_ctx":
                head = _rope(head, *tabs)
            o_ref[rows, cols] = head.astype(BF16)
        if normed < heads:
            cols = slice(normed * ATT_HD, heads * ATT_HD)
            o_ref[rows, cols] = acc[:, cols].astype(BF16)


def _in_proj_kernel(x_ref, nw_ref, sh_ref, sc_ref, w_ref, cos_ref, sin_ref, *rest,
                    epilogue, is_ctx, n_off, n_col_tiles, n_steps):
    *extra, o_ref, h_ref, r_ref, acc_a, acc_b = rest
    s = pl.program_id(0)

    @pl.when(jnp.logical_and(s % n_col_tiles == 0, s < n_steps - 1))
    def _():
        _norm_rows(x_ref, h_ref, r_ref, 0, IN_TM, nw_ref[...], sh_ref[...], sc_ref[...])

    @pl.when(s == 0)
    def _():
        acc_b[...] = jnp.zeros_like(acc_b)

    n_prev = n_off + jnp.maximum(s - 1, 0) % n_col_tiles
    pr = IN_TM // IN_PIECES

    def run(acc_w, acc_r, kind):
        w = w_ref[...].astype(BF16)
        for p in range(IN_PIECES):
            rows = slice(p * pr, (p + 1) * pr)
            acc_w[rows, :] = jnp.dot(h_ref[rows, :], w, preferred_element_type=F32)
            epilogue.apply(kind, acc_r[rows, :], rows, n_prev, o_ref, cos_ref, sin_ref, extra)

    for parity, (acc_w, acc_r) in enumerate(((acc_a, acc_b), (acc_b, acc_a))):
        for kind, cond in epilogue.kinds(n_prev, is_ctx):
            pred = s % 2 == parity
            if cond is not None:
                pred = jnp.logical_and(pred, cond)
            pl.when(pred)(functools.partial(run, acc_w, acc_r, kind))


def _in_proj(epilogue, name, src, is_ctx, n_off, n_cols, hd, nw, mod, w, tabs, extra):
    tm, tn = IN_TM, IN_TN
    tps = SEQ // tm
    n_row_tiles = src.shape[0] // tm
    nct = n_cols // tn
    n_steps = n_row_tiles * nct + 1
    mrow = _mod_row_fn(is_ctx, tm)

    def row_tile(s):
        return jnp.minimum(s // nct, n_row_tiles - 1)

    def prev(s):
        t = jnp.maximum(s - 1, 0)
        return t // nct, t % nct

    tab_spec = pl.BlockSpec((tm, hd), lambda s: (prev(s)[0] % tps, 0))
    return pl.pallas_call(
        functools.partial(_in_proj_kernel, epilogue=epilogue, is_ctx=is_ctx, n_off=n_off,
                          n_col_tiles=nct, n_steps=n_steps),
        grid=(n_steps,),
        in_specs=[
            pl.BlockSpec((tm, D), lambda s: (row_tile(s), 0), pipeline_mode=pl.Buffered(1)),
            pl.BlockSpec((1, D), lambda s: (0, 0)),
            pl.BlockSpec((None, 1, D), lambda s: (mrow(row_tile(s)), 0, 0)),
            pl.BlockSpec((None, 1, D), lambda s: (mrow(row_tile(s)), 0, 1)),
            pl.BlockSpec((D, tn), lambda s: (0, n_off + s % nct)),
            tab_spec, tab_spec,
        ] + [pl.BlockSpec(e.shape, lambda s: (0, 0)) for e in extra],
        out_specs=pl.BlockSpec((tm, tn), lambda s: prev(s)),
        out_shape=jax.ShapeDtypeStruct((src.shape[0], n_cols), BF16),
        scratch_shapes=[pltpu.VMEM((tm, D), BF16), pltpu.VMEM((tm, LANES), F32),
                        pltpu.VMEM((tm, tn), F32), pltpu.VMEM((tm, tn), F32)],
        compiler_params=_params("arbitrary"),
        name=name,
    )(src, nw, mod, mod, w, *tabs, *extra)


def _ret_scan_kernel(ld_ref, qx_ref, kx_ref, vx_ref, gx_ref, qc_ref, kc_ref, vc_ref, gc_ref, gnw_ref,
                     yx_ref, yc_ref, o_ref, sf_ref, sb_ref):
    h = pl.program_id(1)
    c = CHUNK
    nn = lax.broadcasted_iota(jnp.int32, (c, c), 0).astype(F32)
    mm = lax.broadcasted_iota(jnp.int32, (c, c), 1).astype(F32)
    idx = lax.broadcasted_iota(jnp.int32, (c, 1), 0).astype(F32)

    def log_gamma(direction, shape):
        return -jnp.exp(jnp.full(shape, ld_ref[direction, h], F32))

    lgf, lgb = log_gamma(0, (c, c)), log_gamma(1, (c, c))
    lgf1, lgb1 = log_gamma(0, (c, 1)), log_gamma(1, (c, 1))
    lgf0, lgb0 = log_gamma(0, (1, 1)), log_gamma(1, (1, 1))
    fwd = (sf_ref,
           jnp.where(nn >= mm, jnp.exp(lgf * jnp.where(nn >= mm, nn - mm, 0.0)), 0.0),
           jnp.exp(lgf1 * (idx + 1.0)), jnp.exp(lgf1 * (c - 1.0 - idx)), jnp.exp(lgf0 * c))
    bwd = (sb_ref,
           jnp.where(mm >= nn, jnp.exp(lgb * jnp.where(mm >= nn, mm - nn, 0.0)), 0.0),
           jnp.exp(lgb1 * (c - idx)), jnp.exp(lgb1 * idx), jnp.exp(lgb0 * c))
    gnw = gnw_ref[...]

    def chunk_step(direction, q_ref, k_ref, v_ref, rows):
        st_ref, decay, qd, kd, cd = direction
        q, k, v = q_ref[rows, :], k_ref[rows, :], v_ref[rows, :]
        state = st_ref[...]
        scores = lax.dot_general(q, k, NT_DIMS, preferred_element_type=F32) * decay
        out = jnp.dot(scores.astype(BF16), v, preferred_element_type=F32)
        out = out + jnp.dot(q, state.astype(BF16), preferred_element_type=F32) * qd
        kdec = k * kd.astype(BF16)
        st_ref[...] = state * cd + lax.dot_general(kdec, v, TN_DIMS, preferred_element_type=F32)
        return out

    def finish(tot, g_ref, y_ref, rows):
        mu = jnp.mean(tot, axis=-1, keepdims=True)
        cen = tot - mu
        var = jnp.mean(cen * cen, axis=-1, keepdims=True)
        yn = cen * lax.rsqrt(var + EPS) * gnw
        y_ref[rows, :] = (_silu(g_ref[rows, :].astype(F32)) * yn).astype(BF16)

    def scan(q_ref, k_ref, v_ref, g_ref, y_ref, base, n_chunks):
        def rows_of(j):
            r = pl.multiple_of(j * c, c)
            return pl.ds(r, c), pl.ds(pl.multiple_of(base + r, c), c)

        def first_half(j, carry):
            for direction, jj in ((fwd, j), (bwd, n_chunks - 1 - j)):
                rows, orows = rows_of(jj)
                o_ref[orows, :] = chunk_step(direction, q_ref, k_ref, v_ref, rows)
            return carry

        def second_half(j, carry):
            for direction, jj in ((fwd, j), (bwd, n_chunks - 1 - j)):
                rows, orows = rows_of(jj)
                tot = chunk_step(direction, q_ref, k_ref, v_ref, rows) + o_ref[orows, :]
                finish(tot, g_ref, y_ref, rows)
            return carry

        if n_chunks == 1:
            rows = pl.ds(0, c)
            tot = chunk_step(fwd, q_ref, k_ref, v_ref, rows) + chunk_step(bwd, q_ref, k_ref, v_ref, rows)
            finish(tot, g_ref, y_ref, rows)
        else:
            lax.fori_loop(0, n_chunks // 2, first_half, 0)
            lax.fori_loop(n_chunks // 2, n_chunks, second_half, 0)

    sf_ref[...] = jnp.zeros_like(sf_ref)
    sb_ref[...] = jnp.zeros_like(sb_ref)
    scan(qc_ref, kc_ref, vc_ref, gc_ref, yc_ref, SEQ, CTX // c)
    scan(qx_ref, kx_ref, vx_ref, gx_ref, yx_ref, 0, SEQ // c)


def _ret_scan(qkvg_x, qkvg_c, log_decay, gn_w):
    hq = D // RET_DK
    hv = 2 * D // RET_DV

    def specs(rows):
        return [
            pl.BlockSpec((rows, RET_DK), lambda b, h: (b, h)),
            pl.BlockSpec((rows, RET_DK), lambda b, h: (b, hq + h)),
            pl.BlockSpec((rows, RET_DV), lambda b, h: (b, hv + h)),
            pl.BlockSpec((rows, RET_DV), lambda b, h: (b, 2 * hv + h)),
        ]

    state = pltpu.VMEM((RET_DK, RET_DV), F32)
    return pl.pallas_call(
        _ret_scan_kernel,
        grid=(B, RET_HEADS),
        in_specs=[pl.BlockSpec(memory_space=pltpu.SMEM)] + specs(SEQ) + specs(CTX)
                 + [pl.BlockSpec((1, RET_DV), lambda b, h: (0, h))],
        out_specs=[
            pl.BlockSpec((SEQ, RET_DV), lambda b, h: (b, h)),
            pl.BlockSpec((CTX, RET_DV), lambda b, h: (b, h)),
        ],
        out_shape=[jax.ShapeDtypeStruct((NX, 2 * D), BF16), jax.ShapeDtypeStruct((NC, 2 * D), BF16)],
        scratch_shapes=[pltpu.VMEM((SEQ + CTX, RET_DV), F32), state, state],
        compiler_params=_params("arbitrary", "arbitrary"),
        name="ret_scan",
    )(log_decay, qkvg_x, qkvg_x, qkvg_x, qkvg_x, qkvg_c, qkvg_c, qkvg_c, qkvg_c, gn_w)


PROJ_TM = 1024
PROJ_TN = 512


def _proj_res_kernel(y_ref, w_ref, x_ref, g_ref, o_ref):
    acc = jnp.dot(y_ref[...], w_ref[...].astype(BF16), preferred_element_type=F32)
    o_ref[...] = x_ref[...] + g_ref[...] * acc


def _proj_res(name, y, w, res, mod, is_ctx):
    tm, tn = PROJ_TM, PROJ_TN
    rows, ky = y.shape
    mrow = _mod_row_fn(is_ctx, tm)
    gate0 = 2 * D // tn
    return pl.pallas_call(
        _proj_res_kernel,
        grid=(rows // tm, D // tn),
        in_specs=[
            pl.BlockSpec((tm, ky), lambda i, n: (i, 0)),
            pl.BlockSpec((ky, tn), lambda i, n: (0, n)),
            pl.BlockSpec((tm, tn), lambda i, n: (i, n)),
            pl.BlockSpec((None, 1, tn), lambda i, n: (mrow(i), 0, gate0 + n)),
        ],
        out_specs=pl.BlockSpec((tm, tn), lambda i, n: (i, n)),
        out_shape=jax.ShapeDtypeStruct((rows, D), F32),
        compiler_params=_params("arbitrary", "arbitrary"),
        name=name,
    )(y, w, res, mod)


FFN_TM = 1024
FFN_TF = 512
FFN_SUB = 256
FFN_SLAB = 256
HALO = BF16_ROWS


def _ffn_kernel(xp_ref, x_ref, xn_ref, nw_ref, sh_ref, sc_ref, g_ref, wa_ref, wb_ref,
                cwa_ref, cwb_ref, cba_ref, cbb_ref, wd_ref, fnw_ref, o_ref,
                h_ref, r_ref, *slab_refs, is_ctx, final_norm):
    tm = FFN_TM
    i = pl.program_id(0)
    f = pl.program_id(1)

    @pl.when(f == 0)
    def _():
        nw, sh, sc = nw_ref[...], sh_ref[...], sc_ref[...]
        _norm_rows(xp_ref, h_ref, r_ref, 0, HALO, nw, sh, sc)
        _norm_rows(x_ref, h_ref, r_ref, HALO, tm, nw, sh, sc)
        _norm_rows(xn_ref, h_ref, r_ref, HALO + tm, HALO, nw, sh, sc)
        o_ref[...] = jnp.zeros_like(o_ref)

    sub, slab = FFN_SUB, FFN_SLAB
    n_slabs = FFN_TF // slab
    n_blocks = tm // sub
    *u_refs, act_ref = slab_refs
    bounds = [0] + [2 * HALO + sub * (q + 1) for q in range(n_blocks - 1)] + [tm + 2 * HALO]

    def up_proj(s, q=None):
        rows = slice(0, tm + 2 * HALO) if q is None else slice(bounds[q], bounds[q + 1])
        cols = slice(slab * s, slab * (s + 1))
        hh = h_ref[rows, :]
        u_refs[2 * s][rows, :] = jnp.dot(hh, wa_ref[:, cols].astype(BF16), preferred_element_type=F32)
        u_refs[2 * s + 1][rows, :] = jnp.dot(hh, wb_ref[:, cols].astype(BF16), preferred_element_type=F32)

    def clear_outside_rows(s):
        tiles_per_seq = SEQ // tm
        at_start = i % tiles_per_seq == 0
        at_end = i % tiles_per_seq == tiles_per_seq - 1
        before = slice(HALO - 8, HALO)
        after = slice(HALO + tm, HALO + tm + 8)
        for u_ref in u_refs[2 * s:2 * s + 2]:
            u_ref[before, :] = jnp.where(at_start, 0.0, u_ref[before, :])
            u_ref[after, :] = jnp.where(at_end, 0.0, u_ref[after, :])

    def conv_act(s, q):
        cols = slice(slab * s, slab * (s + 1))
        r0 = sub * q
        lo = HALO + r0
        if is_ctx:
            pos = (r0 + lax.broadcasted_iota(jnp.int32, (sub, 1), 0)) % CTX
            has_prev = pos > 0
            has_next = pos < CTX - 1

        def conv(u_ref, cw_ref, cb_ref):
            prev = u_ref[lo - 1:lo - 1 + sub, :]
            cur = u_ref[lo:lo + sub, :]
            nxt = u_ref[lo + 1:lo + 1 + sub, :]
            if is_ctx:
                prev = jnp.where(has_prev, prev, 0.0)
                nxt = jnp.where(has_next, nxt, 0.0)
            return (prev * cw_ref[0:1, cols] + cur * cw_ref[1:2, cols] + nxt * cw_ref[2:3, cols]
                    + cb_ref[:, cols])

        a = conv(u_refs[2 * s], cwa_ref, cba_ref)
        b = conv(u_refs[2 * s + 1], cwb_ref, cbb_ref)
        act_ref[r0:r0 + sub, cols] = (_silu(a) * b).astype(BF16)

    up_proj(0)
    if not is_ctx:
        clear_outside_rows(0)
    for s in range(1, n_slabs):
        for q in range(n_blocks):
            up_proj(s, q)
            conv_act(s - 1, q)
        if not is_ctx:
            clear_outside_rows(s)
    wd = wd_ref[...].astype(BF16)
    for q in range(n_blocks):
        conv_act(n_slabs - 1, q)
        rows = slice(sub * q, sub * (q + 1))
        o_ref[rows, :] += jnp.dot(act_ref[rows, :], wd, preferred_element_type=F32)


    @pl.when(f == pl.num_programs(1) - 1)
    def _():
        gate = g_ref[...]
        for r0 in range(0, tm, sub):
            rows = slice(r0, r0 + sub)
            o_ref[rows, :] = x_ref[rows, :] + gate * o_ref[rows, :]
        if final_norm:
            _scale_rows_by_inv_rms(o_ref, r_ref, tm, fnw_ref[...])


def _conv_ffn(name, xs, nw, mod, layer, w_up, conv_w, conv_b, w_down, fnw, is_ctx, final_norm):
    tm, tf = FFN_TM, FFN_TF
    n_rows = xs.shape[0]
    nf = FFN // tf
    hb = tm // HALO
    last_hb = n_rows // HALO - 1
    mrow = _mod_row_fn(is_ctx, tm)

    def mod_spec(chunk):
        return pl.BlockSpec((None, 1, D), lambda i, f: (mrow(i), 0, chunk))

    return pl.pallas_call(
        functools.partial(_ffn_kernel, is_ctx=is_ctx, final_norm=final_norm),
        grid=(n_rows // tm, nf),
        in_specs=[
            pl.BlockSpec((HALO, D), lambda i, f: (jnp.maximum(i * hb - 1, 0), 0)),
            pl.BlockSpec((tm, D), lambda i, f: (i, 0), pipeline_mode=pl.Buffered(1)),
            pl.BlockSpec((HALO, D), lambda i, f: (jnp.minimum((i + 1) * hb, last_hb), 0)),
            pl.BlockSpec((1, D), lambda i, f: (0, 0)),
            mod_spec(3), mod_spec(4), mod_spec(5),
            pl.BlockSpec((None, D, tf), lambda i, f: (layer, 0, f)),
            pl.BlockSpec((None, D, tf), lambda i, f: (layer, 0, nf + f)),
            pl.BlockSpec((None, 3, tf), lambda i, f: (layer, 0, f)),
            pl.BlockSpec((None, 3, tf), lambda i, f: (layer, 0, nf + f)),
            pl.BlockSpec((None, 1, tf), lambda i, f: (layer, 0, f)),
            pl.BlockSpec((None, 1, tf), lambda i, f: (layer, 0, nf + f)),
            pl.BlockSpec((None, tf, D), lambda i, f: (layer, f, 0)),
            pl.BlockSpec((1, D), lambda i, f: (0, 0)),
        ],
        out_specs=pl.BlockSpec((tm, D), lambda i, f: (i, 0), pipeline_mode=pl.Buffered(1)),
        out_shape=jax.ShapeDtypeStruct((n_rows, D), F32),
        scratch_shapes=[
            pltpu.VMEM((tm + 2 * HALO, D), BF16),
            pltpu.VMEM((tm, LANES), F32),
        ] + [
            pltpu.VMEM((tm + 2 * HALO, FFN_SLAB), F32),
            pltpu.VMEM((tm + 2 * HALO, FFN_SLAB), F32),
        ] * (tf // FFN_SLAB) + [
            pltpu.VMEM((tm, tf), BF16),
        ],
        compiler_params=_params("arbitrary", "arbitrary", vmem=VMEM_LIMIT_FFN),
        name=name,
    )(xs, xs, xs, nw, mod, mod, mod, w_up, w_up, conv_w, conv_w, conv_b, conv_b, w_down, fnw)


ATT_TQ = 1024
ATT_UNIT = 128


def _attn_kernel(q_ref, kx_ref, kc_ref, vx_ref, vc_ref, o_ref):
    kx, kc, vx, vc = kx_ref[...], kc_ref[...], vx_ref[...], vc_ref[...]
    for u in range(ATT_TQ // ATT_UNIT):
        r0 = u * ATT_UNIT
        q = q_ref[r0:r0 + ATT_UNIT, :]
        qs = jnp.concatenate([q[:, j * ATT_HD:(j + 1) * ATT_HD] for j in range(ATT_GROUP)], axis=0)
        sx = lax.dot_general(qs, kx, NT_DIMS, preferred_element_type=F32)
        sc = lax.dot_general(qs, kc, NT_DIMS, preferred_element_type=F32)
        m = jnp.maximum(jnp.max(sx, axis=-1, keepdims=True), jnp.max(sc, axis=-1, keepdims=True))
        px = jnp.exp2(sx - m)
        pc = jnp.exp2(sc - m)
        denom = jnp.sum(px, axis=-1, keepdims=True) + jnp.sum(pc, axis=-1, keepdims=True)
        out = jnp.dot(px.astype(BF16), vx, preferred_element_type=F32)
        out = out + jnp.dot(pc.astype(BF16), vc, preferred_element_type=F32)
        out = out * (1.0 / denom)
        for j in range(ATT_GROUP):
            o_ref[r0:r0 + ATT_UNIT, j * ATT_HD:(j + 1) * ATT_HD] = (
                out[j * ATT_UNIT:(j + 1) * ATT_UNIT, :].astype(BF16))


def _attention(qkv_x, kv_c):
    tq = ATT_TQ
    qb = SEQ // tq
    k0 = D // ATT_HD
    v0 = k0 + ATT_KV
    return pl.pallas_call(
        _attn_kernel,
        grid=(B, ATT_KV, qb),
        in_specs=[
            pl.BlockSpec((tq, ATT_GROUP * ATT_HD), lambda b, g, t: (b * qb + t, g)),
            pl.BlockSpec((SEQ, ATT_HD), lambda b, g, t: (b, k0 + g)),
            pl.BlockSpec((CTX, ATT_HD), lambda b, g, t: (b, g)),
            pl.BlockSpec((SEQ, ATT_HD), lambda b, g, t: (b, v0 + g)),
            pl.BlockSpec((CTX, ATT_HD), lambda b, g, t: (b, ATT_KV + g)),
        ],
        out_specs=pl.BlockSpec((tq, ATT_GROUP * ATT_HD), lambda b, g, t: (b * qb + t, g)),
        out_shape=jax.ShapeDtypeStruct((NX, D), BF16),
        compiler_params=_params("arbitrary", "arbitrary", "arbitrary"),
        name="gqa_attention",
    )(qkv_x, qkv_x, kv_c, qkv_x, kv_c)


def _rope_tables(head_dim):
    rows = SEQ // GRID_W
    row = jnp.repeat(jnp.arange(rows, dtype=F32), GRID_W)
    col = jnp.tile(jnp.arange(GRID_W, dtype=F32), rows)
    n_freq = head_dim // 4
    inv = ROPE_THETA ** (-jnp.arange(n_freq, dtype=F32) / n_freq)
    ang = jnp.concatenate([row[:, None] * inv, col[:, None] * inv], axis=-1)
    return jnp.repeat(jnp.cos(ang), 2, axis=-1), jnp.repeat(jnp.sin(ang), 2, axis=-1)


def kernel(x, c, ctx, c_ctx, ada_w, ada_b, norm_w, ret_w_in, ret_w_out, ret_log_decay, ret_gn_w,
           attn_w_in, attn_w_out, attn_q_norm, attn_k_norm, ffn_w_up, ffn_conv_w, ffn_conv_b,
           ffn_w_down, final_norm_w):
    xx = x.reshape(NX, D)
    xc = ctx.reshape(NC, D)
    cmat = jnp.concatenate([c, c_ctx[None, :], jnp.zeros((MOD_ROWS - B - 1, D), F32)], axis=0)
    mod = _modulation(cmat, ada_w, ada_b)
    mod0 = mod[0].reshape(MOD_ROWS, 1, 6 * D)
    mod1 = mod[1].reshape(MOD_ROWS, 1, 6 * D)
    fnw = final_norm_w.reshape(1, D)

    nw = norm_w[0, 0].reshape(1, D)
    tabs = _rope_tables(RET_DK)
    qkvg_x = _in_proj(_RetInEpilogue, "ret_in_x", xx, False, 0, 6 * D, RET_DK, nw, mod0, ret_w_in[0], tabs, [])
    qkvg_c = _in_proj(_RetInEpilogue, "ret_in_ctx", xc, True, 0, 6 * D, RET_DK, nw, mod0, ret_w_in[0], tabs, [])
    yx, yc = _ret_scan(qkvg_x, qkvg_c, ret_log_decay[0], ret_gn_w[0].reshape(1, 2 * D))
    xx = _proj_res("ret_out_x", yx, ret_w_out[0], xx, mod0, False)
    xc = _proj_res("ret_out_ctx", yc, ret_w_out[0], xc, mod0, True)
    nw = norm_w[0, 1].reshape(1, D)
    ffn = (ffn_w_up, ffn_conv_w, ffn_conv_b.reshape(-1, 1, 2 * FFN), ffn_w_down)
    xx = _conv_ffn("conv_ffn_x", xx, nw, mod0, 0, *ffn, fnw, False, False)
    xc = _conv_ffn("conv_ffn_ctx", xc, nw, mod0, 0, *ffn, fnw, True, False)

    nw = norm_w[1, 0].reshape(1, D)
    tabs = _rope_tables(ATT_HD)
    heads = [attn_q_norm[0].reshape(1, ATT_HD), attn_k_norm[0].reshape(1, ATT_HD)]
    qkv_x = _in_proj(_AttnInEpilogue, "attn_in_x", xx, False, 0, ATT_IN, ATT_HD, nw, mod1, attn_w_in[0],
                     tabs, heads)
    kv_c = _in_proj(_AttnInEpilogue, "attn_in_ctx", xc, True, ATT_Q_TILES, ATT_IN - D, ATT_HD, nw, mod1,
                    attn_w_in[0], tabs, heads)
    ya = _attention(qkv_x, kv_c)
    xx = _proj_res("attn_out", ya, attn_w_out[0], xx, mod1, False)
    out = _conv_ffn("conv_ffn_out", xx, norm_w[1, 1].reshape(1, D), mod1, 1, *ffn, fnw, False, True)
    return out.reshape(B, SEQ, D)
```

```python
import functools
import math

import jax
import jax.numpy as jnp
from jax import lax
from jax.experimental import pallas as pl
from jax.experimental.pallas import tpu as pltpu

D = 2048
B = 4
SEQ = 2048
CTX = 256
GRID_W = 64
RET_HEADS = 8
RET_DK = D // RET_HEADS
RET_DV = 2 * D // RET_HEADS
CHUNK = 256
ATT_HEADS = 16
ATT_KV = 4
ATT_HD = D // ATT_HEADS
ATT_GROUP = ATT_HEADS // ATT_KV
ATT_IN = (ATT_HEADS + 2 * ATT_KV) * ATT_HD
FFN = 256 * ((8 * D // 3 + 255) // 256)
ROPE_THETA = 10000.0
EPS = 1e-6

NX = B * SEQ
NC = B * CTX
CTX_ROW = B
MOD_ROWS = 8

F32 = jnp.float32
BF16 = jnp.bfloat16
BF16_ROWS = 16
LANES = 128

V7X_VMEM_BYTES = 64 * 1024 * 1024
VMEM_LIMIT = V7X_VMEM_BYTES - 8 * 1024 * 1024
VMEM_LIMIT_FFN = V7X_VMEM_BYTES - 4 * 1024 * 1024

NT_DIMS = (((1,), (1,)), ((), ()))
TN_DIMS = (((0,), (0,)), ((), ()))


def _params(*sem, vmem=VMEM_LIMIT):
    return pltpu.CompilerParams(dimension_semantics=sem, vmem_limit_bytes=vmem)


LOG2_E = math.log2(math.e)


def _silu(v):
    return v * (1.0 / (1.0 + jnp.exp2(v * -LOG2_E)))


def _for_row_tiles(n_rows, body, unroll):
    trips = n_rows // BF16_ROWS
    if trips == 1:
        body(0)
    else:
        def step(j, carry):
            body(pl.multiple_of(j * BF16_ROWS, BF16_ROWS))
            return carry
        lax.fori_loop(0, trips, step, 0, unroll=unroll)


def _inv_rms_rows(x_ref, r_ref, n_rows):
    lanes = r_ref.shape[1]
    width = x_ref.shape[1]

    def stats(r):
        x = x_ref[pl.ds(r, BF16_ROWS), :]
        sq = x * x
        part = sq[:, 0:lanes]
        for t in range(1, width // lanes):
            part = part + sq[:, t * lanes:(t + 1) * lanes]
        r_ref[pl.ds(r, BF16_ROWS), :] = part

    _for_row_tiles(n_rows, stats, 4)
    rows = slice(0, n_rows)
    ms = jnp.sum(r_ref[rows, :], axis=-1, keepdims=True) * (1.0 / width)
    r_ref[rows, :] = jnp.broadcast_to(lax.rsqrt(ms + EPS), (n_rows, lanes))


def _row_scale(r_ref, r, width):
    inv = r_ref[pl.ds(r, BF16_ROWS), :]
    return jnp.concatenate([inv] * (width // r_ref.shape[1]), axis=1)


def _norm_rows(x_ref, h_ref, r_ref, h_row0, n_rows, nw, sh, sc):
    gain = nw * (1.0 + sc)
    _inv_rms_rows(x_ref, r_ref, n_rows)

    def apply(r):
        x = x_ref[pl.ds(r, BF16_ROWS), :]
        dst = pl.ds(pl.multiple_of(h_row0 + r, BF16_ROWS), BF16_ROWS)
        h_ref[dst, :] = (x * _row_scale(r_ref, r, x.shape[1]) * gain + sh).astype(BF16)

    _for_row_tiles(n_rows, apply, 2)


def _scale_rows_by_inv_rms(o_ref, r_ref, n_rows, w):
    _inv_rms_rows(o_ref, r_ref, n_rows)

    def apply(r):
        rows = pl.ds(r, BF16_ROWS)
        x = o_ref[rows, :]
        o_ref[rows, :] = x * _row_scale(r_ref, r, x.shape[1]) * w

    _for_row_tiles(n_rows, apply, 2)


def _mod_row_fn(is_ctx, tm):
    if is_ctx:
        return lambda i: CTX_ROW
    return lambda i: i // (SEQ // tm)


def _rope_coeffs(cos, sin):
    even = lax.broadcasted_iota(jnp.int32, cos.shape, 1) % 2 == 0
    return cos, jnp.where(even, -sin, 0.0), jnp.where(even, 0.0, sin)


def _rope(seg, cos, sa, sb):
    hd = seg.shape[-1]
    return seg * cos + pltpu.roll(seg, hd - 1, 1) * sa + pltpu.roll(seg, 1, 1) * sb


MOD_TN = 1024


def _mod_kernel(c_ref, w_ref, b_ref, o_ref):
    a = _silu(c_ref[...]).astype(BF16)
    o_ref[...] = jnp.dot(a, w_ref[...].astype(BF16), preferred_element_type=F32) + b_ref[...]


def _modulation(cmat, ada_w, ada_b):
    depth = ada_w.shape[0]
    return pl.pallas_call(
        _mod_kernel,
        grid=(depth, 6 * D // MOD_TN),
        in_specs=[
            pl.BlockSpec((MOD_ROWS, D), lambda l, n: (0, 0)),
            pl.BlockSpec((None, D, MOD_TN), lambda l, n: (l, 0, n)),
            pl.BlockSpec((None, 1, MOD_TN), lambda l, n: (l, 0, n)),
        ],
        out_specs=pl.BlockSpec((None, MOD_ROWS, MOD_TN), lambda l, n: (l, 0, n)),
        out_shape=jax.ShapeDtypeStruct((depth, MOD_ROWS, 6 * D), F32),
        compiler_params=_params("arbitrary", "arbitrary"),
        name="adaln_mod",
    )(cmat, ada_w, ada_b.reshape(depth, 1, 6 * D))


IN_TM = 1024
IN_TN = 1024


IN_PIECES = 4


class _RetInEpilogue:
    @staticmethod
    def kinds(n, is_ctx):
        if is_ctx:
            return [("plain", None)]
        is_qk = n < 2 * D // IN_TN
        return [("rope", is_qk), ("plain", jnp.logical_not(is_qk))]

    @staticmethod
    def apply(kind, acc, rows, n, o_ref, cos_ref, sin_ref, extra):
        is_k = jnp.logical_and(n >= D // IN_TN, n < 2 * D // IN_TN)
        kscale = jnp.where(is_k, RET_DK ** -0.5, 1.0).astype(F32)
        if kind == "plain":
            o_ref[rows, :] = (acc * kscale).astype(BF16)
            return
        tabs = _rope_coeffs(cos_ref[rows, :], sin_ref[rows, :])
        for s in range(IN_TN // RET_DK):
            cols = slice(s * RET_DK, (s + 1) * RET_DK)
            o_ref[rows, cols] = _rope(acc[:, cols] * kscale, *tabs).astype(BF16)


ATT_QSCALE = ATT_HD ** -0.5 * LOG2_E
ATT_Q_TILES = D // IN_TN


class _AttnInEpilogue:
    @staticmethod
    def kinds(n, is_ctx):
        if is_ctx:
            return [("kv_ctx", None)]
        is_q = n < ATT_Q_TILES
        return [("q", is_q), ("kv", jnp.logical_not(is_q))]

    @staticmethod
    def apply(kind, acc, rows, n, o_ref, cos_ref, sin_ref, extra):
        qn_ref, kn_ref = extra
        heads = IN_TN // ATT_HD
        normed = heads if kind == "q" else ATT_KV
        hw = qn_ref[...] if kind == "q" else kn_ref[...]
        scale = ATT_QSCALE if kind == "q" else 1.0
        if kind != "kv_ctx":
            tabs = _rope_coeffs(cos_ref[rows, :], sin_ref[rows, :])
        for s in range(normed):
            cols = slice(s * ATT_HD, (s + 1) * ATT_HD)
            seg = acc[:, cols]
            ms = jnp.mean(seg * seg, axis=-1, keepdims=True)
            head = seg * (lax.rsqrt(ms + EPS) * scale) * hw
            if kind != "kv_ctx":
                head = _rope(head, *tabs)
            o_ref[rows, cols] = head.astype(BF16)
        if normed < heads:
            cols = slice(normed * ATT_HD, heads * ATT_HD)
            o_ref[rows, cols] = acc[:, cols].astype(BF16)


def _in_proj_kernel(x_ref, nw_ref, sh_ref, sc_ref, w_ref, cos_ref, sin_ref, *rest,
                    epilogue, is_ctx, n_off, n_col_tiles, n_steps):
    *extra, o_ref, h_ref, r_ref, acc_a, acc_b = rest
    s = pl.program_id(0)

    @pl.when(jnp.logical_and(s % n_col_tiles == 0, s < n_steps - 1))
    def _():
        _norm_rows(x_ref, h_ref, r_ref, 0, IN_TM, nw_ref[...], sh_ref[...], sc_ref[...])

    @pl.when(s == 0)
    def _():
        acc_b[...] = jnp.zeros_like(acc_b)

    n_prev = n_off + jnp.maximum(s - 1, 0) % n_col_tiles
    pr = IN_TM // IN_PIECES

    def run(acc_w, acc_r, kind):
        w = w_ref[...].astype(BF16)
        for p in range(IN_PIECES):
            rows = slice(p * pr, (p + 1) * pr)
            acc_w[rows, :] = jnp.dot(h_ref[rows, :], w, preferred_element_type=F32)
            epilogue.apply(kind, acc_r[rows, :], rows, n_prev, o_ref, cos_ref, sin_ref, extra)

    for parity, (acc_w, acc_r) in enumerate(((acc_a, acc_b), (acc_b, acc_a))):
        for kind, cond in epilogue.kinds(n_prev, is_ctx):
            pred = s % 2 == parity
            if cond is not None:
                pred = jnp.logical_and(pred, cond)
            pl.when(pred)(functools.partial(run, acc_w, acc_r, kind))


def _in_proj(epilogue, name, src, is_ctx, n_off, n_cols, hd, nw, mod, w, tabs, extra):
    tm, tn = IN_TM, IN_TN
    tps = SEQ // tm
    n_row_tiles = src.shape[0] // tm
    nct = n_cols // tn
    n_steps = n_row_tiles * nct + 1
    mrow = _mod_row_fn(is_ctx, tm)

    def row_tile(s):
        return jnp.minimum(s // nct, n_row_tiles - 1)

    def prev(s):
        t = jnp.maximum(s - 1, 0)
        return t // nct, t % nct

    tab_spec = pl.BlockSpec((tm, hd), lambda s: (prev(s)[0] % tps, 0))
    return pl.pallas_call(
        functools.partial(_in_proj_kernel, epilogue=epilogue, is_ctx=is_ctx, n_off=n_off,
                          n_col_tiles=nct, n_steps=n_steps),
        grid=(n_steps,),
        in_specs=[
            pl.BlockSpec((tm, D), lambda s: (row_tile(s), 0), pipeline_mode=pl.Buffered(1)),
            pl.BlockSpec((1, D), lambda s: (0, 0)),
            pl.BlockSpec((None, 1, D), lambda s: (mrow(row_tile(s)), 0, 0)),
            pl.BlockSpec((None, 1, D), lambda s: (mrow(row_tile(s)), 0, 1)),
            pl.BlockSpec((D, tn), lambda s: (0, n_off + s % nct)),
            tab_spec, tab_spec,
        ] + [pl.BlockSpec(e.shape, lambda s: (0, 0)) for e in extra],
        out_specs=pl.BlockSpec((tm, tn), lambda s: prev(s)),
        out_shape=jax.ShapeDtypeStruct((src.shape[0], n_cols), BF16),
        scratch_shapes=[pltpu.VMEM((tm, D), BF16), pltpu.VMEM((tm, LANES), F32),
                        pltpu.VMEM((tm, tn), F32), pltpu.VMEM((tm, tn), F32)],
        compiler_params=_params("arbitrary"),
        name=name,
    )(src, nw, mod, mod, w, *tabs, *extra)


def _ret_scan_kernel(ld_ref, qx_ref, kx_ref, vx_ref, gx_ref, qc_ref, kc_ref, vc_ref, gc_ref, gnw_ref,
                     yx_ref, yc_ref, o_ref, gate_ref, sf_ref, sb_ref):
    h = pl.program_id(1)
    c = CHUNK
    nn = lax.broadcasted_iota(jnp.int32, (c, c), 0).astype(F32)
    mm = lax.broadcasted_iota(jnp.int32, (c, c), 1).astype(F32)
    idx = lax.broadcasted_iota(jnp.int32, (c, 1), 0).astype(F32)

    def log_gamma(direction, shape):
        return -jnp.exp(jnp.full(shape, ld_ref[direction, h], F32))

    lgf, lgb = log_gamma(0, (c, c)), log_gamma(1, (c, c))
    lgf1, lgb1 = log_gamma(0, (c, 1)), log_gamma(1, (c, 1))
    lgf0, lgb0 = log_gamma(0, (1, 1)), log_gamma(1, (1, 1))
    fwd = (sf_ref,
           jnp.where(nn >= mm, jnp.exp(lgf * jnp.where(nn >= mm, nn - mm, 0.0)), 0.0),
           jnp.exp(lgf1 * (idx + 1.0)), jnp.exp(lgf1 * (c - 1.0 - idx)), jnp.exp(lgf0 * c))
    bwd = (sb_ref,
           jnp.where(mm >= nn, jnp.exp(lgb * jnp.where(mm >= nn, mm - nn, 0.0)), 0.0),
           jnp.exp(lgb1 * (c - idx)), jnp.exp(lgb1 * idx), jnp.exp(lgb0 * c))
    gnw = gnw_ref[...]

    def chunk_step(direction, q_ref, k_ref, v_ref, rows):
        st_ref, decay, qd, kd, cd = direction
        q, k, v = q_ref[rows, :], k_ref[rows, :], v_ref[rows, :]
        state = st_ref[...]
        scores = lax.dot_general(q, k, NT_DIMS, preferred_element_type=F32) * decay
        out = jnp.dot(scores.astype(BF16), v, preferred_element_type=F32)
        out = out + jnp.dot(q, state.astype(BF16), preferred_element_type=F32) * qd
        kdec = k * kd.astype(BF16)
        st_ref[...] = state * cd + lax.dot_general(kdec, v, TN_DIMS, preferred_element_type=F32)
        return out

    def gate_of(g_ref, rows):
        return _silu(g_ref[rows, :].astype(F32)) * gnw

    def finish(tot, gate, y_ref, rows):
        mu = jnp.mean(tot, axis=-1, keepdims=True)
        cen = tot - mu
        var = jnp.mean(cen * cen, axis=-1, keepdims=True)
        y_ref[rows, :] = (gate * (cen * lax.rsqrt(var + EPS))).astype(BF16)

    def scan(q_ref, k_ref, v_ref, g_ref, y_ref, base, n_chunks):
        def rows_of(j):
            r = pl.multiple_of(j * c, c)
            return pl.ds(r, c), pl.ds(pl.multiple_of(base + r, c), c)

        def first_half(j, carry):
            for direction, jj in ((fwd, j), (bwd, n_chunks - 1 - j)):
                rows, orows = rows_of(jj)
                o_ref[orows, :] = chunk_step(direction, q_ref, k_ref, v_ref, rows)
                gate_ref[orows, :] = gate_of(g_ref, rows)
            return carry

        def second_half(j, carry):
            for direction, jj in ((fwd, j), (bwd, n_chunks - 1 - j)):
                rows, orows = rows_of(jj)
                tot = chunk_step(direction, q_ref, k_ref, v_ref, rows) + o_ref[orows, :]
                finish(tot, gate_ref[orows, :], y_ref, rows)
            return carry

        if n_chunks == 1:
            rows = pl.ds(0, c)
            tot = chunk_step(fwd, q_ref, k_ref, v_ref, rows) + chunk_step(bwd, q_ref, k_ref, v_ref, rows)
            finish(tot, gate_of(g_ref, rows), y_ref, rows)
        else:
            lax.fori_loop(0, n_chunks // 2, first_half, 0)
            lax.fori_loop(n_chunks // 2, n_chunks, second_half, 0)

    sf_ref[...] = jnp.zeros_like(sf_ref)
    sb_ref[...] = jnp.zeros_like(sb_ref)
    scan(qc_ref, kc_ref, vc_ref, gc_ref, yc_ref, SEQ, CTX // c)
    scan(qx_ref, kx_ref, vx_ref, gx_ref, yx_ref, 0, SEQ // c)


def _ret_scan(qkvg_x, qkvg_c, log_decay, gn_w):
    hq = D // RET_DK
    hv = 2 * D // RET_DV

    def specs(rows):
        return [
            pl.BlockSpec((rows, RET_DK), lambda b, h: (b, h)),
            pl.BlockSpec((rows, RET_DK), lambda b, h: (b, hq + h)),
            pl.BlockSpec((rows, RET_DV), lambda b, h: (b, hv + h)),
            pl.BlockSpec((rows, RET_DV), lambda b, h: (b, 2 * hv + h)),
        ]

    state = pltpu.VMEM((RET_DK, RET_DV), F32)
    return pl.pallas_call(
        _ret_scan_kernel,
        grid=(B, RET_HEADS),
        in_specs=[pl.BlockSpec(memory_space=pltpu.SMEM)] + specs(SEQ) + specs(CTX)
                 + [pl.BlockSpec((1, RET_DV), lambda b, h: (0, h))],
        out_specs=[
            pl.BlockSpec((SEQ, RET_DV), lambda b, h: (b, h)),
            pl.BlockSpec((CTX, RET_DV), lambda b, h: (b, h)),
        ],
        out_shape=[jax.ShapeDtypeStruct((NX, 2 * D), BF16), jax.ShapeDtypeStruct((NC, 2 * D), BF16)],
        scratch_shapes=[pltpu.VMEM((SEQ + CTX, RET_DV), F32), pltpu.VMEM((SEQ + CTX, RET_DV), F32), state, state],
        compiler_params=_params("arbitrary", "arbitrary"),
        name="ret_scan",
    )(log_decay, qkvg_x, qkvg_x, qkvg_x, qkvg_x, qkvg_c, qkvg_c, qkvg_c, qkvg_c, gn_w)


PROJ_TM = 1024
PROJ_TN = 512


def _proj_res_kernel(y_ref, w_ref, x_ref, g_ref, o_ref):
    acc = jnp.dot(y_ref[...], w_ref[...].astype(BF16), preferred_element_type=F32)
    o_ref[...] = x_ref[...] + g_ref[...] * acc


def _proj_res(name, y, w, res, mod, is_ctx):
    tm, tn = PROJ_TM, PROJ_TN
    rows, ky = y.shape
    mrow = _mod_row_fn(is_ctx, tm)
    gate0 = 2 * D // tn
    return pl.pallas_call(
        _proj_res_kernel,
        grid=(rows // tm, D // tn),
        in_specs=[
            pl.BlockSpec((tm, ky), lambda i, n: (i, 0)),
            pl.BlockSpec((ky, tn), lambda i, n: (0, n)),
            pl.BlockSpec((tm, tn), lambda i, n: (i, n)),
            pl.BlockSpec((None, 1, tn), lambda i, n: (mrow(i), 0, gate0 + n)),
        ],
        out_specs=pl.BlockSpec((tm, tn), lambda i, n: (i, n)),
        out_shape=jax.ShapeDtypeStruct((rows, D), F32),
        compiler_params=_params("arbitrary", "arbitrary"),
        name=name,
    )(y, w, res, mod)


FFN_TM = 1024
FFN_TF = 512
FFN_SUB = 256
FFN_SLAB = 256
HALO = BF16_ROWS


def _ffn_kernel(xp_ref, x_ref, xn_ref, nw_ref, sh_ref, sc_ref, g_ref, wa_ref, wb_ref,
                cwa_ref, cwb_ref, cba_ref, cbb_ref, wd_ref, fnw_ref, o_ref,
                h_ref, r_ref, *slab_refs, is_ctx, final_norm):
    tm = FFN_TM
    i = pl.program_id(0)
    f = pl.program_id(1)

    @pl.when(f == 0)
    def _():
        nw, sh, sc = nw_ref[...], sh_ref[...], sc_ref[...]
        _norm_rows(xp_ref, h_ref, r_ref, 0, HALO, nw, sh, sc)
        _norm_rows(x_ref, h_ref, r_ref, HALO, tm, nw, sh, sc)
        _norm_rows(xn_ref, h_ref, r_ref, HALO + tm, HALO, nw, sh, sc)
        o_ref[...] = jnp.zeros_like(o_ref)

    sub, slab = FFN_SUB, FFN_SLAB
    n_slabs = FFN_TF // slab
    n_blocks = tm // sub
    *u_refs, act_ref = slab_refs
    bounds = [0] + [2 * HALO + sub * (q + 1) for q in range(n_blocks - 1)] + [tm + 2 * HALO]

    def up_proj(s, q=None):
        rows = slice(0, tm + 2 * HALO) if q is None else slice(bounds[q], bounds[q + 1])
        cols = slice(slab * s, slab * (s + 1))
        hh = h_ref[rows, :]
        u_refs[2 * s][rows, :] = jnp.dot(hh, wa_ref[:, cols].astype(BF16), preferred_element_type=F32)
        u_refs[2 * s + 1][rows, :] = jnp.dot(hh, wb_ref[:, cols].astype(BF16), preferred_element_type=F32)

    def clear_outside_rows(s):
        tiles_per_seq = SEQ // tm
        at_start = i % tiles_per_seq == 0
        at_end = i % tiles_per_seq == tiles_per_seq - 1
        before = slice(HALO - 8, HALO)
        after = slice(HALO + tm, HALO + tm + 8)
        for u_ref in u_refs[2 * s:2 * s + 2]:
            u_ref[before, :] = jnp.where(at_start, 0.0, u_ref[before, :])
            u_ref[after, :] = jnp.where(at_end, 0.0, u_ref[after, :])

    def conv_act(s, q):
        cols = slice(slab * s, slab * (s + 1))
        r0 = sub * q
        lo = HALO + r0
        if is_ctx:
            pos = (r0 + lax.broadcasted_iota(jnp.int32, (sub, 1), 0)) % CTX
            has_prev = pos > 0
            has_next = pos < CTX - 1

        def conv(u_ref, cw_ref, cb_ref):
            prev = u_ref[lo - 1:lo - 1 + sub, :]
            cur = u_ref[lo:lo + sub, :]
            nxt = u_ref[lo + 1:lo + 1 + sub, :]
            if is_ctx:
                prev = jnp.where(has_prev, prev, 0.0)
                nxt = jnp.where(has_next, nxt, 0.0)
            return (prev * cw_ref[0:1, cols] + cur * cw_ref[1:2, cols] + nxt * cw_ref[2:3, cols]
                    + cb_ref[:, cols])

        a = conv(u_refs[2 * s], cwa_ref, cba_ref)
        b = conv(u_refs[2 * s + 1], cwb_ref, cbb_ref)
        act_ref[r0:r0 + sub, cols] = (_silu(a) * b).astype(BF16)

    up_proj(0)
    if not is_ctx:
        clear_outside_rows(0)
    for s in range(1, n_slabs):
        for q in range(n_blocks):
            up_proj(s, q)
            conv_act(s - 1, q)
        if not is_ctx:
            clear_outside_rows(s)
    wd = wd_ref[...].astype(BF16)
    for q in range(n_blocks):
        conv_act(n_slabs - 1, q)
        rows = slice(sub * q, sub * (q + 1))
        o_ref[rows, :] += jnp.dot(act_ref[rows, :], wd, preferred_element_type=F32)


    @pl.when(f == pl.num_programs(1) - 1)
    def _():
        gate = g_ref[...]
        for r0 in range(0, tm, sub):
            rows = slice(r0, r0 + sub)
            o_ref[rows, :] = x_ref[rows, :] + gate * o_ref[rows, :]
        if final_norm:
            _scale_rows_by_inv_rms(o_ref, r_ref, tm, fnw_ref[...])


def _conv_ffn(name, xs, nw, mod, layer, w_up, conv_w, conv_b, w_down, fnw, is_ctx, final_norm):
    tm, tf = FFN_TM, FFN_TF
    n_rows = xs.shape[0]
    nf = FFN // tf
    hb = tm // HALO
    last_hb = n_rows // HALO - 1
    mrow = _mod_row_fn(is_ctx, tm)

    def mod_spec(chunk):
        return pl.BlockSpec((None, 1, D), lambda i, f: (mrow(i), 0, chunk))

    return pl.pallas_call(
        functools.partial(_ffn_kernel, is_ctx=is_ctx, final_norm=final_norm),
        grid=(n_rows // tm, nf),
        in_specs=[
            pl.BlockSpec((HALO, D), lambda i, f: (jnp.maximum(i * hb - 1, 0), 0)),
            pl.BlockSpec((tm, D), lambda i, f: (i, 0), pipeline_mode=pl.Buffered(1)),
            pl.BlockSpec((HALO, D), lambda i, f: (jnp.minimum((i + 1) * hb, last_hb), 0)),
            pl.BlockSpec((1, D), lambda i, f: (0, 0)),
            mod_spec(3), mod_spec(4), mod_spec(5),
            pl.BlockSpec((None, D, tf), lambda i, f: (layer, 0, f)),
            pl.BlockSpec((None, D, tf), lambda i, f: (layer, 0, nf + f)),
            pl.BlockSpec((None, 3, tf), lambda i, f: (layer, 0, f)),
            pl.BlockSpec((None, 3, tf), lambda i, f: (layer, 0, nf + f)),
            pl.BlockSpec((None, 1, tf), lambda i, f: (layer, 0, f)),
            pl.BlockSpec((None, 1, tf), lambda i, f: (layer, 0, nf + f)),
            pl.BlockSpec((None, tf, D), lambda i, f: (layer, f, 0)),
            pl.BlockSpec((1, D), lambda i, f: (0, 0)),
        ],
        out_specs=pl.BlockSpec((tm, D), lambda i, f: (i, 0), pipeline_mode=pl.Buffered(1)),
        out_shape=jax.ShapeDtypeStruct((n_rows, D), F32),
        scratch_shapes=[
            pltpu.VMEM((tm + 2 * HALO, D), BF16),
            pltpu.VMEM((tm, LANES), F32),
        ] + [
            pltpu.VMEM((tm + 2 * HALO, FFN_SLAB), F32),
            pltpu.VMEM((tm + 2 * HALO, FFN_SLAB), F32),
        ] * (tf // FFN_SLAB) + [
            pltpu.VMEM((tm, tf), BF16),
        ],
        compiler_params=_params("arbitrary", "arbitrary", vmem=VMEM_LIMIT_FFN),
        name=name,
    )(xs, xs, xs, nw, mod, mod, mod, w_up, w_up, conv_w, conv_w, conv_b, conv_b, w_down, fnw)


ATT_TQ = 1024
ATT_UNIT = 128


def _attn_kernel(q_ref, kx_ref, kc_ref, vx_ref, vc_ref, o_ref):
    kx, kc, vx, vc = kx_ref[...], kc_ref[...], vx_ref[...], vc_ref[...]
    for u in range(ATT_TQ // ATT_UNIT):
        r0 = u * ATT_UNIT
        q = q_ref[r0:r0 + ATT_UNIT, :]
        qs = jnp.concatenate([q[:, j * ATT_HD:(j + 1) * ATT_HD] for j in range(ATT_GROUP)], axis=0)
        sx = lax.dot_general(qs, kx, NT_DIMS, preferred_element_type=F32)
        sc = lax.dot_general(qs, kc, NT_DIMS, preferred_element_type=F32)
        m = jnp.maximum(jnp.max(sx, axis=-1, keepdims=True), jnp.max(sc, axis=-1, keepdims=True))
        px = jnp.exp2(sx - m)
        pc = jnp.exp2(sc - m)
        denom = jnp.sum(px, axis=-1, keepdims=True) + jnp.sum(pc, axis=-1, keepdims=True)
        out = jnp.dot(px.astype(BF16), vx, preferred_element_type=F32)
        out = out + jnp.dot(pc.astype(BF16), vc, preferred_element_type=F32)
        out = out * (1.0 / denom)
        for j in range(ATT_GROUP):
            o_ref[r0:r0 + ATT_UNIT, j * ATT_HD:(j + 1) * ATT_HD] = (
                out[j * ATT_UNIT:(j + 1) * ATT_UNIT, :].astype(BF16))


def _attention(qkv_x, kv_c):
    tq = ATT_TQ
    qb = SEQ // tq
    k0 = D // ATT_HD
    v0 = k0 + ATT_KV
    return pl.pallas_call(
        _attn_kernel,
        grid=(B, ATT_KV, qb),
        in_specs=[
            pl.BlockSpec((tq, ATT_GROUP * ATT_HD), lambda b, g, t: (b * qb + t, g)),
            pl.BlockSpec((SEQ, ATT_HD), lambda b, g, t: (b, k0 + g)),
            pl.BlockSpec((CTX, ATT_HD), lambda b, g, t: (b, g)),
            pl.BlockSpec((SEQ, ATT_HD), lambda b, g, t: (b, v0 + g)),
            pl.BlockSpec((CTX, ATT_HD), lambda b, g, t: (b, ATT_KV + g)),
        ],
        out_specs=pl.BlockSpec((tq, ATT_GROUP * ATT_HD), lambda b, g, t: (b * qb + t, g)),
        out_shape=jax.ShapeDtypeStruct((NX, D), BF16),
        compiler_params=_params("arbitrary", "arbitrary", "arbitrary"),
        name="gqa_attention",
    )(qkv_x, qkv_x, kv_c, qkv_x, kv_c)


def _rope_tables(head_dim):
    rows = SEQ // GRID_W
    row = jnp.repeat(jnp.arange(rows, dtype=F32), GRID_W)
    col = jnp.tile(jnp.arange(GRID_W, dtype=F32), rows)
    n_freq = head_dim // 4
    inv = ROPE_THETA ** (-jnp.arange(n_freq, dtype=F32) / n_freq)
    ang = jnp.concatenate([row[:, None] * inv, col[:, None] * inv], axis=-1)
    return jnp.repeat(jnp.cos(ang), 2, axis=-1), jnp.repeat(jnp.sin(ang), 2, axis=-1)


def kernel(x, c, ctx, c_ctx, ada_w, ada_b, norm_w, ret_w_in, ret_w_out, ret_log_decay, ret_gn_w,
           attn_w_in, attn_w_out, attn_q_norm, attn_k_norm, ffn_w_up, ffn_conv_w, ffn_conv_b,
           ffn_w_down, final_norm_w):
    xx = x.reshape(NX, D)
    xc = ctx.reshape(NC, D)
    cmat = jnp.concatenate([c, c_ctx[None, :], jnp.zeros((MOD_ROWS - B - 1, D), F32)], axis=0)
    mod = _modulation(cmat, ada_w, ada_b)
    mod0 = mod[0].reshape(MOD_ROWS, 1, 6 * D)
    mod1 = mod[1].reshape(MOD_ROWS, 1, 6 * D)
    fnw = final_norm_w.reshape(1, D)

    nw = norm_w[0, 0].reshape(1, D)
    tabs = _rope_tables(RET_DK)
    qkvg_x = _in_proj(_RetInEpilogue, "ret_in_x", xx, False, 0, 6 * D, RET_DK, nw, mod0, ret_w_in[0], tabs, [])
    qkvg_c = _in_proj(_RetInEpilogue, "ret_in_ctx", xc, True, 0, 6 * D, RET_DK, nw, mod0, ret_w_in[0], tabs, [])
    yx, yc = _ret_scan(qkvg_x, qkvg_c, ret_log_decay[0], ret_gn_w[0].reshape(1, 2 * D))
    xx = _proj_res("ret_out_x", yx, ret_w_out[0], xx, mod0, False)
    xc = _proj_res("ret_out_ctx", yc, ret_w_out[0], xc, mod0, True)
    nw = norm_w[0, 1].reshape(1, D)
    ffn = (ffn_w_up, ffn_conv_w, ffn_conv_b.reshape(-1, 1, 2 * FFN), ffn_w_down)
    xx = _conv_ffn("conv_ffn_x", xx, nw, mod0, 0, *ffn, fnw, False, False)
    xc = _conv_ffn("conv_ffn_ctx", xc, nw, mod0, 0, *ffn, fnw, True, False)

    nw = norm_w[1, 0].reshape(1, D)
    tabs = _rope_tables(ATT_HD)
    heads = [attn_q_norm[0].reshape(1, ATT_HD), attn_k_norm[0].reshape(1, ATT_HD)]
    qkv_x = _in_proj(_AttnInEpilogue, "attn_in_x", xx, False, 0, ATT_IN, ATT_HD, nw, mod1, attn_w_in[0],
                     tabs, heads)
    kv_c = _in_proj(_AttnInEpilogue, "attn_in_ctx", xc, True, ATT_Q_TILES, ATT_IN - D, ATT_HD, nw, mod1,
                    attn_w_in[0], tabs, heads)
    ya = _attention(qkv_x, kv_c)
    xx = _proj_res("attn_out", ya, attn_w_out[0], xx, mod1, False)
    out = _conv_ffn("conv_ffn_out", xx, norm_w[1, 1].reshape(1, D), mod1, 1, *ffn, fnw, False, True)
    return out.reshape(B, SEQ, D)
```

```python
import functools
import math

import jax
import jax.numpy as jnp
from jax import lax
from jax.experimental import pallas as pl
from jax.experimental.pallas import tpu as pltpu

D = 2048
B = 4
SEQ = 2048
CTX = 256
GRID_W = 64
RET_HEADS = 8
RET_DK = D // RET_HEADS
RET_DV = 2 * D // RET_HEADS
CHUNK = 256
ATT_HEADS = 16
ATT_KV = 4
ATT_HD = D // ATT_HEADS
ATT_GROUP = ATT_HEADS // ATT_KV
ATT_IN = (ATT_HEADS + 2 * ATT_KV) * ATT_HD
FFN = 256 * ((8 * D // 3 + 255) // 256)
ROPE_THETA = 10000.0
EPS = 1e-6

NX = B * SEQ
NC = B * CTX
CTX_ROW = B
MOD_ROWS = 8

F32 = jnp.float32
BF16 = jnp.bfloat16
BF16_ROWS = 16
LANES = 128

V7X_VMEM_BYTES = 64 * 1024 * 1024
VMEM_LIMIT = V7X_VMEM_BYTES - 8 * 1024 * 1024
VMEM_LIMIT_FFN = V7X_VMEM_BYTES - 4 * 1024 * 1024

NT_DIMS = (((1,), (1,)), ((), ()))
TN_DIMS = (((0,), (0,)), ((), ()))


def _params(*sem, vmem=VMEM_LIMIT):
    return pltpu.CompilerParams(dimension_semantics=sem, vmem_limit_bytes=vmem)


LOG2_E = math.log2(math.e)


def _silu(v):
    return v * (1.0 / (1.0 + jnp.exp2(v * -LOG2_E)))


def _for_row_tiles(n_rows, body, unroll):
    trips = n_rows // BF16_ROWS
    if trips == 1:
        body(0)
    else:
        def step(j, carry):
            body(pl.multiple_of(j * BF16_ROWS, BF16_ROWS))
            return carry
        lax.fori_loop(0, trips, step, 0, unroll=unroll)


def _inv_rms_rows(x_ref, r_ref, n_rows):
    lanes = r_ref.shape[1]
    width = x_ref.shape[1]

    def stats(r):
        x = x_ref[pl.ds(r, BF16_ROWS), :]
        sq = x * x
        part = sq[:, 0:lanes]
        for t in range(1, width // lanes):
            part = part + sq[:, t * lanes:(t + 1) * lanes]
        r_ref[pl.ds(r, BF16_ROWS), :] = part

    _for_row_tiles(n_rows, stats, 4)
    rows = slice(0, n_rows)
    ms = jnp.sum(r_ref[rows, :], axis=-1, keepdims=True) * (1.0 / width)
    r_ref[rows, :] = jnp.broadcast_to(lax.rsqrt(ms + EPS), (n_rows, lanes))


def _row_scale(r_ref, r, width):
    inv = r_ref[pl.ds(r, BF16_ROWS), :]
    return jnp.concatenate([inv] * (width // r_ref.shape[1]), axis=1)


def _norm_rows(x_ref, h_ref, r_ref, h_row0, n_rows, nw, sh, sc):
    gain = nw * (1.0 + sc)
    _inv_rms_rows(x_ref, r_ref, n_rows)

    def apply(r):
        x = x_ref[pl.ds(r, BF16_ROWS), :]
        dst = pl.ds(pl.multiple_of(h_row0 + r, BF16_ROWS), BF16_ROWS)
        h_ref[dst, :] = (x * _row_scale(r_ref, r, x.shape[1]) * gain + sh).astype(BF16)

    _for_row_tiles(n_rows, apply, 2)


def _scale_rows_by_inv_rms(o_ref, r_ref, n_rows, w):
    _inv_rms_rows(o_ref, r_ref, n_rows)

    def apply(r):
        rows = pl.ds(r, BF16_ROWS)
        x = o_ref[rows, :]
        o_ref[rows, :] = x * _row_scale(r_ref, r, x.shape[1]) * w

    _for_row_tiles(n_rows, apply, 2)


def _mod_row_fn(is_ctx, tm):
    if is_ctx:
        return lambda i: CTX_ROW
    return lambda i: i // (SEQ // tm)


def _rope_coeffs(cos, sin):
    even = lax.broadcasted_iota(jnp.int32, cos.shape, 1) % 2 == 0
    return cos, jnp.where(even, -sin, 0.0), jnp.where(even, 0.0, sin)


def _rope(seg, cos, sa, sb):
    hd = seg.shape[-1]
    return seg * cos + pltpu.roll(seg, hd - 1, 1) * sa + pltpu.roll(seg, 1, 1) * sb


MOD_TN = 1024


def _mod_kernel(c_ref, w_ref, b_ref, o_ref):
    a = _silu(c_ref[...]).astype(BF16)
    o_ref[...] = jnp.dot(a, w_ref[...].astype(BF16), preferred_element_type=F32) + b_ref[...]


def _modulation(cmat, ada_w, ada_b):
    depth = ada_w.shape[0]
    return pl.pallas_call(
        _mod_kernel,
        grid=(depth, 6 * D // MOD_TN),
        in_specs=[
            pl.BlockSpec((MOD_ROWS, D), lambda l, n: (0, 0)),
            pl.BlockSpec((None, D, MOD_TN), lambda l, n: (l, 0, n)),
            pl.BlockSpec((None, 1, MOD_TN), lambda l, n: (l, 0, n)),
        ],
        out_specs=pl.BlockSpec((None, MOD_ROWS, MOD_TN), lambda l, n: (l, 0, n)),
        out_shape=jax.ShapeDtypeStruct((depth, MOD_ROWS, 6 * D), F32),
        compiler_params=_params("arbitrary", "arbitrary"),
        name="adaln_mod",
    )(cmat, ada_w, ada_b.reshape(depth, 1, 6 * D))


IN_TM = 1024
IN_TN = 1024


IN_PIECES = 4


class _RetInEpilogue:
    @staticmethod
    def kinds(n, is_ctx):
        if is_ctx:
            return [("plain", None)]
        is_qk = n < 2 * D // IN_TN
        return [("rope", is_qk), ("plain", jnp.logical_not(is_qk))]

    @staticmethod
    def apply(kind, acc, rows, n, o_ref, cos_ref, sin_ref, extra):
        is_k = jnp.logical_and(n >= D // IN_TN, n < 2 * D // IN_TN)
        kscale = jnp.where(is_k, RET_DK ** -0.5, 1.0).astype(F32)
        if kind == "plain":
            o_ref[rows, :] = (acc * kscale).astype(BF16)
            return
        tabs = _rope_coeffs(cos_ref[rows, :], sin_ref[rows, :])
        for s in range(IN_TN // RET_DK):
            cols = slice(s * RET_DK, (s + 1) * RET_DK)
            o_ref[rows, cols] = _rope(acc[:, cols] * kscale, *tabs).astype(BF16)


ATT_QSCALE = ATT_HD ** -0.5 * LOG2_E
ATT_Q_TILES = D // IN_TN


class _AttnInEpilogue:
    @staticmethod
    def kinds(n, is_ctx):
        if is_ctx:
            return [("kv_ctx", None)]
        is_q = n < ATT_Q_TILES
        return [("q", is_q), ("kv", jnp.logical_not(is_q))]

    @staticmethod
    def apply(kind, acc, rows, n, o_ref, cos_ref, sin_ref, extra):
        qn_ref, kn_ref = extra
        heads = IN_TN // ATT_HD
        normed = heads if kind == "q" else ATT_KV
        hw = qn_ref[...] if kind == "q" else kn_ref[...]
        scale = ATT_QSCALE if kind == "q" else 1.0
        if kind != "kv_ctx":
            tabs = _rope_coeffs(cos_ref[rows, :], sin_ref[rows, :])
        for s in range(normed):
            cols = slice(s * ATT_HD, (s + 1) * ATT_HD)
            seg = acc[:, cols]
            ms = jnp.mean(seg * seg, axis=-1, keepdims=True)
            head = seg * (lax.rsqrt(ms + EPS) * scale) * hw
            if kind != "kv_ctx":
                head = _rope(head, *tabs)
            o_ref[rows, cols] = head.astype(BF16)
        if normed < heads:
            cols = slice(normed * ATT_HD, heads * ATT_HD)
            o_ref[rows, cols] = acc[:, cols].astype(BF16)


def _in_proj_kernel(x_ref, nw_ref, sh_ref, sc_ref, w_ref, cos_ref, sin_ref, *rest,
                    epilogue, is_ctx, n_off, n_col_tiles, n_steps):
    *extra, o_ref, h_ref, r_ref, acc_a, acc_b = rest
    s = pl.program_id(0)

    @pl.when(jnp.logical_and(s % n_col_tiles == 0, s < n_steps - 1))
    def _():
        _norm_rows(x_ref, h_ref, r_ref, 0, IN_TM, nw_ref[...], sh_ref[...], sc_ref[...])

    @pl.when(s == 0)
    def _():
        acc_b[...] = jnp.zeros_like(acc_b)

    n_prev = n_off + jnp.maximum(s - 1, 0) % n_col_tiles
    pr = IN_TM // IN_PIECES

    def run(acc_w, acc_r, kind):
        w = w_ref[...].astype(BF16)
        for p in range(IN_PIECES):
            rows = slice(p * pr, (p + 1) * pr)
            acc_w[rows, :] = jnp.dot(h_ref[rows, :], w, preferred_element_type=F32)
            epilogue.apply(kind, acc_r[rows, :], rows, n_prev, o_ref, cos_ref, sin_ref, extra)

    for parity, (acc_w, acc_r) in enumerate(((acc_a, acc_b), (acc_b, acc_a))):
        for kind, cond in epilogue.kinds(n_prev, is_ctx):
            pred = s % 2 == parity
            if cond is not None:
                pred = jnp.logical_and(pred, cond)
            pl.when(pred)(functools.partial(run, acc_w, acc_r, kind))


def _in_proj(epilogue, name, src, is_ctx, n_off, n_cols, hd, nw, mod, w, tabs, extra):
    tm, tn = IN_TM, IN_TN
    tps = SEQ // tm
    n_row_tiles = src.shape[0] // tm
    nct = n_cols // tn
    n_steps = n_row_tiles * nct + 1
    mrow = _mod_row_fn(is_ctx, tm)

    def row_tile(s):
        return jnp.minimum(s // nct, n_row_tiles - 1)

    def prev(s):
        t = jnp.maximum(s - 1, 0)
        return t // nct, t % nct

    tab_spec = pl.BlockSpec((tm, hd), lambda s: (prev(s)[0] % tps, 0))
    return pl.pallas_call(
        functools.partial(_in_proj_kernel, epilogue=epilogue, is_ctx=is_ctx, n_off=n_off,
                          n_col_tiles=nct, n_steps=n_steps),
        grid=(n_steps,),
        in_specs=[
            pl.BlockSpec((tm, D), lambda s: (row_tile(s), 0), pipeline_mode=pl.Buffered(1)),
            pl.BlockSpec((1, D), lambda s: (0, 0)),
            pl.BlockSpec((None, 1, D), lambda s: (mrow(row_tile(s)), 0, 0)),
            pl.BlockSpec((None, 1, D), lambda s: (mrow(row_tile(s)), 0, 1)),
            pl.BlockSpec((D, tn), lambda s: (0, n_off + s % nct)),
            tab_spec, tab_spec,
        ] + [pl.BlockSpec(e.shape, lambda s: (0, 0)) for e in extra],
        out_specs=pl.BlockSpec((tm, tn), lambda s: prev(s)),
        out_shape=jax.ShapeDtypeStruct((src.shape[0], n_cols), BF16),
        scratch_shapes=[pltpu.VMEM((tm, D), BF16), pltpu.VMEM((tm, LANES), F32),
                        pltpu.VMEM((tm, tn), F32), pltpu.VMEM((tm, tn), F32)],
        compiler_params=_params("arbitrary"),
        name=name,
    )(src, nw, mod, mod, w, *tabs, *extra)


def _ret_scan_kernel(ld_ref, qx_ref, kx_ref, vx_ref, gx_ref, qc_ref, kc_ref, vc_ref, gc_ref, gnw_ref,
                     yx_ref, yc_ref, o_ref, gate_ref, sf_ref, sb_ref):
    h = pl.program_id(1)
    c = CHUNK
    nn = lax.broadcasted_iota(jnp.int32, (c, c), 0).astype(F32)
    mm = lax.broadcasted_iota(jnp.int32, (c, c), 1).astype(F32)
    idx = lax.broadcasted_iota(jnp.int32, (c, 1), 0).astype(F32)

    def log_gamma(direction, shape):
        return -jnp.exp(jnp.full(shape, ld_ref[direction, h], F32))

    lgf, lgb = log_gamma(0, (c, c)), log_gamma(1, (c, c))
    lgf1, lgb1 = log_gamma(0, (c, 1)), log_gamma(1, (c, 1))
    lgf0, lgb0 = log_gamma(0, (1, 1)), log_gamma(1, (1, 1))
    fwd = (sf_ref,
           jnp.where(nn >= mm, jnp.exp(lgf * jnp.where(nn >= mm, nn - mm, 0.0)), 0.0),
           jnp.exp(lgf1 * (idx + 1.0)), jnp.exp(lgf1 * (c - 1.0 - idx)), jnp.exp(lgf0 * c))
    bwd = (sb_ref,
           jnp.where(mm >= nn, jnp.exp(lgb * jnp.where(mm >= nn, mm - nn, 0.0)), 0.0),
           jnp.exp(lgb1 * (c - idx)), jnp.exp(lgb1 * idx), jnp.exp(lgb0 * c))
    gnw = gnw_ref[...]

    def chunk_step(direction, q_ref, k_ref, v_ref, rows):
        st_ref, decay, qd, kd, cd = direction
        q, k, v = q_ref[rows, :], k_ref[rows, :], v_ref[rows, :]
        state = st_ref[...]
        scores = lax.dot_general(q, k, NT_DIMS, preferred_element_type=F32) * decay
        out = jnp.dot(scores.astype(BF16), v, preferred_element_type=F32)
        out = out + jnp.dot(q, state.astype(BF16), preferred_element_type=F32) * qd
        kdec = k * kd.astype(BF16)
        st_ref[...] = state * cd + lax.dot_general(kdec, v, TN_DIMS, preferred_element_type=F32)
        return out

    def gate_of(g_ref, rows):
        return _silu(g_ref[rows, :].astype(F32)) * gnw

    def finish(tot, gate, y_ref, rows):
        mu = jnp.mean(tot, axis=-1, keepdims=True)
        cen = tot - mu
        var = jnp.mean(cen * cen, axis=-1, keepdims=True)
        y_ref[rows, :] = (gate * (cen * lax.rsqrt(var + EPS))).astype(BF16)

    def scan(q_ref, k_ref, v_ref, g_ref, y_ref, base, n_chunks):
        def rows_of(j):
            r = pl.multiple_of(j * c, c)
            return pl.ds(r, c), pl.ds(pl.multiple_of(base + r, c), c)

        def first_half(j, carry):
            for direction, jj in ((fwd, j), (bwd, n_chunks - 1 - j)):
                rows, orows = rows_of(jj)
                o_ref[orows, :] = chunk_step(direction, q_ref, k_ref, v_ref, rows)
                gate_ref[orows, :] = gate_of(g_ref, rows)
            return carry

        def second_half(j, carry):
            for direction, jj in ((fwd, j), (bwd, n_chunks - 1 - j)):
                rows, orows = rows_of(jj)
                tot = chunk_step(direction, q_ref, k_ref, v_ref, rows) + o_ref[orows, :]
                finish(tot, gate_ref[orows, :], y_ref, rows)
            return carry

        if n_chunks == 1:
            rows = pl.ds(0, c)
            tot = chunk_step(fwd, q_ref, k_ref, v_ref, rows) + chunk_step(bwd, q_ref, k_ref, v_ref, rows)
            finish(tot, gate_of(g_ref, rows), y_ref, rows)
        else:
            lax.fori_loop(0, n_chunks // 2, first_half, 0, unroll=2)
            lax.fori_loop(n_chunks // 2, n_chunks, second_half, 0, unroll=2)

    sf_ref[...] = jnp.zeros_like(sf_ref)
    sb_ref[...] = jnp.zeros_like(sb_ref)
    scan(qc_ref, kc_ref, vc_ref, gc_ref, yc_ref, SEQ, CTX // c)
    scan(qx_ref, kx_ref, vx_ref, gx_ref, yx_ref, 0, SEQ // c)


def _ret_scan(qkvg_x, qkvg_c, log_decay, gn_w):
    hq = D // RET_DK
    hv = 2 * D // RET_DV

    def specs(rows):
        return [
            pl.BlockSpec((rows, RET_DK), lambda b, h: (b, h)),
            pl.BlockSpec((rows, RET_DK), lambda b, h: (b, hq + h)),
            pl.BlockSpec((rows, RET_DV), lambda b, h: (b, hv + h)),
            pl.BlockSpec((rows, RET_DV), lambda b, h: (b, 2 * hv + h)),
        ]

    state = pltpu.VMEM((RET_DK, RET_DV), F32)
    return pl.pallas_call(
        _ret_scan_kernel,
        grid=(B, RET_HEADS),
        in_specs=[pl.BlockSpec(memory_space=pltpu.SMEM)] + specs(SEQ) + specs(CTX)
                 + [pl.BlockSpec((1, RET_DV), lambda b, h: (0, h))],
        out_specs=[
            pl.BlockSpec((SEQ, RET_DV), lambda b, h: (b, h)),
            pl.BlockSpec((CTX, RET_DV), lambda b, h: (b, h)),
        ],
        out_shape=[jax.ShapeDtypeStruct((NX, 2 * D), BF16), jax.ShapeDtypeStruct((NC, 2 * D), BF16)],
        scratch_shapes=[pltpu.VMEM((SEQ + CTX, RET_DV), F32), pltpu.VMEM((SEQ + CTX, RET_DV), F32), state, state],
        compiler_params=_params("arbitrary", "arbitrary"),
        name="ret_scan",
    )(log_decay, qkvg_x, qkvg_x, qkvg_x, qkvg_x, qkvg_c, qkvg_c, qkvg_c, qkvg_c, gn_w)


PROJ_TM = 1024
PROJ_TN = 512


def _proj_res_kernel(y_ref, w_ref, x_ref, g_ref, o_ref):
    acc = jnp.dot(y_ref[...], w_ref[...].astype(BF16), preferred_element_type=F32)
    o_ref[...] = x_ref[...] + g_ref[...] * acc


def _proj_res(name, y, w, res, mod, is_ctx):
    tm, tn = PROJ_TM, PROJ_TN
    rows, ky = y.shape
    mrow = _mod_row_fn(is_ctx, tm)
    gate0 = 2 * D // tn
    return pl.pallas_call(
        _proj_res_kernel,
        grid=(rows // tm, D // tn),
        in_specs=[
            pl.BlockSpec((tm, ky), lambda i, n: (i, 0)),
            pl.BlockSpec((ky, tn), lambda i, n: (0, n)),
            pl.BlockSpec((tm, tn), lambda i, n: (i, n)),
            pl.BlockSpec((None, 1, tn), lambda i, n: (mrow(i), 0, gate0 + n)),
        ],
        out_specs=pl.BlockSpec((tm, tn), lambda i, n: (i, n)),
        out_shape=jax.ShapeDtypeStruct((rows, D), F32),
        compiler_params=_params("arbitrary", "arbitrary"),
        name=name,
    )(y, w, res, mod)


FFN_TM = 1024
FFN_TF = 512
FFN_SUB = 256
FFN_SLAB = 256
HALO = BF16_ROWS


def _ffn_kernel(xp_ref, x_ref, xn_ref, nw_ref, sh_ref, sc_ref, g_ref, wa_ref, wb_ref,
                cwa_ref, cwb_ref, cba_ref, cbb_ref, wd_ref, fnw_ref, o_ref,
                h_ref, r_ref, *slab_refs, is_ctx, final_norm):
    tm = FFN_TM
    i = pl.program_id(0)
    f = pl.program_id(1)

    @pl.when(f == 0)
    def _():
        nw, sh, sc = nw_ref[...], sh_ref[...], sc_ref[...]
        _norm_rows(xp_ref, h_ref, r_ref, 0, HALO, nw, sh, sc)
        _norm_rows(x_ref, h_ref, r_ref, HALO, tm, nw, sh, sc)
        _norm_rows(xn_ref, h_ref, r_ref, HALO + tm, HALO, nw, sh, sc)
        o_ref[...] = jnp.zeros_like(o_ref)

    sub, slab = FFN_SUB, FFN_SLAB
    n_slabs = FFN_TF // slab
    n_blocks = tm // sub
    *u_refs, act_ref = slab_refs
    bounds = [0] + [2 * HALO + sub * (q + 1) for q in range(n_blocks - 1)] + [tm + 2 * HALO]

    def up_proj(s, q=None):
        rows = slice(0, tm + 2 * HALO) if q is None else slice(bounds[q], bounds[q + 1])
        cols = slice(slab * s, slab * (s + 1))
        hh = h_ref[rows, :]
        u_refs[2 * s][rows, :] = jnp.dot(hh, wa_ref[:, cols].astype(BF16), preferred_element_type=F32)
        u_refs[2 * s + 1][rows, :] = jnp.dot(hh, wb_ref[:, cols].astype(BF16), preferred_element_type=F32)

    def clear_outside_rows(s):
        tiles_per_seq = SEQ // tm
        at_start = i % tiles_per_seq == 0
        at_end = i % tiles_per_seq == tiles_per_seq - 1
        before = slice(HALO - 8, HALO)
        after = slice(HALO + tm, HALO + tm + 8)
        for u_ref in u_refs[2 * s:2 * s + 2]:
            u_ref[before, :] = jnp.where(at_start, 0.0, u_ref[before, :])
            u_ref[after, :] = jnp.where(at_end, 0.0, u_ref[after, :])

    def conv_act(s, q):
        cols = slice(slab * s, slab * (s + 1))
        r0 = sub * q
        lo = HALO + r0
        if is_ctx:
            pos = (r0 + lax.broadcasted_iota(jnp.int32, (sub, 1), 0)) % CTX
            has_prev = pos > 0
            has_next = pos < CTX - 1

        def conv(u_ref, cw_ref, cb_ref):
            prev = u_ref[lo - 1:lo - 1 + sub, :]
            cur = u_ref[lo:lo + sub, :]
            nxt = u_ref[lo + 1:lo + 1 + sub, :]
            if is_ctx:
                prev = jnp.where(has_prev, prev, 0.0)
                nxt = jnp.where(has_next, nxt, 0.0)
            return (prev * cw_ref[0:1, cols] + cur * cw_ref[1:2, cols] + nxt * cw_ref[2:3, cols]
                    + cb_ref[:, cols])

        a = conv(u_refs[2 * s], cwa_ref, cba_ref)
        b = conv(u_refs[2 * s + 1], cwb_ref, cbb_ref)
        act_ref[r0:r0 + sub, cols] = (_silu(a) * b).astype(BF16)

    up_proj(0)
    if not is_ctx:
        clear_outside_rows(0)
    for s in range(1, n_slabs):
        for q in range(n_blocks):
            up_proj(s, q)
            conv_act(s - 1, q)
        if not is_ctx:
            clear_outside_rows(s)
    wd = wd_ref[...].astype(BF16)
    for q in range(n_blocks):
        conv_act(n_slabs - 1, q)
        rows = slice(sub * q, sub * (q + 1))
        o_ref[rows, :] += jnp.dot(act_ref[rows, :], wd, preferred_element_type=F32)


    @pl.when(f == pl.num_programs(1) - 1)
    def _():
        gate = g_ref[...]
        for r0 in range(0, tm, sub):
            rows = slice(r0, r0 + sub)
            o_ref[rows, :] = x_ref[rows, :] + gate * o_ref[rows, :]
        if final_norm:
            _scale_rows_by_inv_rms(o_ref, r_ref, tm, fnw_ref[...])


def _conv_ffn(name, xs, nw, mod, layer, w_up, conv_w, conv_b, w_down, fnw, is_ctx, final_norm):
    tm, tf = FFN_TM, FFN_TF
    n_rows = xs.shape[0]
    nf = FFN // tf
    hb = tm // HALO
    last_hb = n_rows // HALO - 1
    mrow = _mod_row_fn(is_ctx, tm)

    def mod_spec(chunk):
        return pl.BlockSpec((None, 1, D), lambda i, f: (mrow(i), 0, chunk))

    return pl.pallas_call(
        functools.partial(_ffn_kernel, is_ctx=is_ctx, final_norm=final_norm),
        grid=(n_rows // tm, nf),
        in_specs=[
            pl.BlockSpec((HALO, D), lambda i, f: (jnp.maximum(i * hb - 1, 0), 0)),
            pl.BlockSpec((tm, D), lambda i, f: (i, 0), pipeline_mode=pl.Buffered(1)),
            pl.BlockSpec((HALO, D), lambda i, f: (jnp.minimum((i + 1) * hb, last_hb), 0)),
            pl.BlockSpec((1, D), lambda i, f: (0, 0)),
            mod_spec(3), mod_spec(4), mod_spec(5),
            pl.BlockSpec((None, D, tf), lambda i, f: (layer, 0, f)),
            pl.BlockSpec((None, D, tf), lambda i, f: (layer, 0, nf + f)),
            pl.BlockSpec((None, 3, tf), lambda i, f: (layer, 0, f)),
            pl.BlockSpec((None, 3, tf), lambda i, f: (layer, 0, nf + f)),
            pl.BlockSpec((None, 1, tf), lambda i, f: (layer, 0, f)),
            pl.BlockSpec((None, 1, tf), lambda i, f: (layer, 0, nf + f)),
            pl.BlockSpec((None, tf, D), lambda i, f: (layer, f, 0)),
            pl.BlockSpec((1, D), lambda i, f: (0, 0)),
        ],
        out_specs=pl.BlockSpec((tm, D), lambda i, f: (i, 0), pipeline_mode=pl.Buffered(1)),
        out_shape=jax.ShapeDtypeStruct((n_rows, D), F32),
        scratch_shapes=[
            pltpu.VMEM((tm + 2 * HALO, D), BF16),
            pltpu.VMEM((tm, LANES), F32),
        ] + [
            pltpu.VMEM((tm + 2 * HALO, FFN_SLAB), F32),
            pltpu.VMEM((tm + 2 * HALO, FFN_SLAB), F32),
        ] * (tf // FFN_SLAB) + [
            pltpu.VMEM((tm, tf), BF16),
        ],
        compiler_params=_params("arbitrary", "arbitrary", vmem=VMEM_LIMIT_FFN),
        name=name,
    )(xs, xs, xs, nw, mod, mod, mod, w_up, w_up, conv_w, conv_w, conv_b, conv_b, w_down, fnw)


ATT_TQ = 1024
ATT_UNIT = 128


def _attn_kernel(q_ref, kx_ref, kc_ref, vx_ref, vc_ref, o_ref):
    kx, kc, vx, vc = kx_ref[...], kc_ref[...], vx_ref[...], vc_ref[...]
    for u in range(ATT_TQ // ATT_UNIT):
        r0 = u * ATT_UNIT
        q = q_ref[r0:r0 + ATT_UNIT, :]
        qs = jnp.concatenate([q[:, j * ATT_HD:(j + 1) * ATT_HD] for j in range(ATT_GROUP)], axis=0)
        sx = lax.dot_general(qs, kx, NT_DIMS, preferred_element_type=F32)
        sc = lax.dot_general(qs, kc, NT_DIMS, preferred_element_type=F32)
        m = jnp.maximum(jnp.max(sx, axis=-1, keepdims=True), jnp.max(sc, axis=-1, keepdims=True))
        px = jnp.exp2(sx - m)
        pc = jnp.exp2(sc - m)
        denom = jnp.sum(px, axis=-1, keepdims=True) + jnp.sum(pc, axis=-1, keepdims=True)
        out = jnp.dot(px.astype(BF16), vx, preferred_element_type=F32)
        out = out + jnp.dot(pc.astype(BF16), vc, preferred_element_type=F32)
        out = out * (1.0 / denom)
        for j in range(ATT_GROUP):
            o_ref[r0:r0 + ATT_UNIT, j * ATT_HD:(j + 1) * ATT_HD] = (
                out[j * ATT_UNIT:(j + 1) * ATT_UNIT, :].astype(BF16))


def _attention(qkv_x, kv_c):
    tq = ATT_TQ
    qb = SEQ // tq
    k0 = D // ATT_HD
    v0 = k0 + ATT_KV
    return pl.pallas_call(
        _attn_kernel,
        grid=(B, ATT_KV, qb),
        in_specs=[
            pl.BlockSpec((tq, ATT_GROUP * ATT_HD), lambda b, g, t: (b * qb + t, g)),
            pl.BlockSpec((SEQ, ATT_HD), lambda b, g, t: (b, k0 + g)),
            pl.BlockSpec((CTX, ATT_HD), lambda b, g, t: (b, g)),
            pl.BlockSpec((SEQ, ATT_HD), lambda b, g, t: (b, v0 + g)),
            pl.BlockSpec((CTX, ATT_HD), lambda b, g, t: (b, ATT_KV + g)),
        ],
        out_specs=pl.BlockSpec((tq, ATT_GROUP * ATT_HD), lambda b, g, t: (b * qb + t, g)),
        out_shape=jax.ShapeDtypeStruct((NX, D), BF16),
        compiler_params=_params("arbitrary", "arbitrary", "arbitrary"),
        name="gqa_attention",
    )(qkv_x, qkv_x, kv_c, qkv_x, kv_c)


def _rope_tables(head_dim):
    rows = SEQ // GRID_W
    row = jnp.repeat(jnp.arange(rows, dtype=F32), GRID_W)
    col = jnp.tile(jnp.arange(GRID_W, dtype=F32), rows)
    n_freq = head_dim // 4
    inv = ROPE_THETA ** (-jnp.arange(n_freq, dtype=F32) / n_freq)
    ang = jnp.concatenate([row[:, None] * inv, col[:, None] * inv], axis=-1)
    return jnp.repeat(jnp.cos(ang), 2, axis=-1), jnp.repeat(jnp.sin(ang), 2, axis=-1)


def kernel(x, c, ctx, c_ctx, ada_w, ada_b, norm_w, ret_w_in, ret_w_out, ret_log_decay, ret_gn_w,
           attn_w_in, attn_w_out, attn_q_norm, attn_k_norm, ffn_w_up, ffn_conv_w, ffn_conv_b,
           ffn_w_down, final_norm_w):
    xx = x.reshape(NX, D)
    xc = ctx.reshape(NC, D)
    cmat = jnp.concatenate([c, c_ctx[None, :], jnp.zeros((MOD_ROWS - B - 1, D), F32)], axis=0)
    mod = _modulation(cmat, ada_w, ada_b)
    mod0 = mod[0].reshape(MOD_ROWS, 1, 6 * D)
    mod1 = mod[1].reshape(MOD_ROWS, 1, 6 * D)
    fnw = final_norm_w.reshape(1, D)

    nw = norm_w[0, 0].reshape(1, D)
    tabs = _rope_tables(RET_DK)
    qkvg_x = _in_proj(_RetInEpilogue, "ret_in_x", xx, False, 0, 6 * D, RET_DK, nw, mod0, ret_w_in[0], tabs, [])
    qkvg_c = _in_proj(_RetInEpilogue, "ret_in_ctx", xc, True, 0, 6 * D, RET_DK, nw, mod0, ret_w_in[0], tabs, [])
    yx, yc = _ret_scan(qkvg_x, qkvg_c, ret_log_decay[0], ret_gn_w[0].reshape(1, 2 * D))
    xx = _proj_res("ret_out_x", yx, ret_w_out[0], xx, mod0, False)
    xc = _proj_res("ret_out_ctx", yc, ret_w_out[0], xc, mod0, True)
    nw = norm_w[0, 1].reshape(1, D)
    ffn = (ffn_w_up, ffn_conv_w, ffn_conv_b.reshape(-1, 1, 2 * FFN), ffn_w_down)
    xx = _conv_ffn("conv_ffn_x", xx, nw, mod0, 0, *ffn, fnw, False, False)
    xc = _conv_ffn("conv_ffn_ctx", xc, nw, mod0, 0, *ffn, fnw, True, False)

    nw = norm_w[1, 0].reshape(1, D)
    tabs = _rope_tables(ATT_HD)
    heads = [attn_q_norm[0].reshape(1, ATT_HD), attn_k_norm[0].reshape(1, ATT_HD)]
    qkv_x = _in_proj(_AttnInEpilogue, "attn_in_x", xx, False, 0, ATT_IN, ATT_HD, nw, mod1, attn_w_in[0],
                     tabs, heads)
    kv_c = _in_proj(_AttnInEpilogue, "attn_in_ctx", xc, True, ATT_Q_TILES, ATT_IN - D, ATT_HD, nw, mod1,
                    attn_w_in[0], tabs, heads)
    ya = _attention(qkv_x, kv_c)
    xx = _proj_res("attn_out", ya, attn_w_out[0], xx, mod1, False)
    out = _conv_ffn("conv_ffn_out", xx, norm_w[1, 1].reshape(1, D), mod1, 1, *ffn, fnw, False, True)
    return out.reshape(B, SEQ, D)
```

```python
import functools
import math

import jax
import jax.numpy as jnp
from jax import lax
from jax.experimental import pallas as pl
from jax.experimental.pallas import tpu as pltpu

D = 2048
B = 4
SEQ = 2048
CTX = 256
GRID_W = 64
RET_HEADS = 8
RET_DK = D // RET_HEADS
RET_DV = 2 * D // RET_HEADS
CHUNK = 256
ATT_HEADS = 16
ATT_KV = 4
ATT_HD = D // ATT_HEADS
ATT_GROUP = ATT_HEADS // ATT_KV
ATT_IN = (ATT_HEADS + 2 * ATT_KV) * ATT_HD
FFN = 256 * ((8 * D // 3 + 255) // 256)
ROPE_THETA = 10000.0
EPS = 1e-6

NX = B * SEQ
NC = B * CTX
CTX_ROW = B
MOD_ROWS = 8

F32 = jnp.float32
BF16 = jnp.bfloat16
BF16_ROWS = 16
LANES = 128

V7X_VMEM_BYTES = 64 * 1024 * 1024
VMEM_LIMIT = V7X_VMEM_BYTES - 8 * 1024 * 1024
VMEM_LIMIT_FFN = V7X_VMEM_BYTES - 4 * 1024 * 1024

NT_DIMS = (((1,), (1,)), ((), ()))
TN_DIMS = (((0,), (0,)), ((), ()))


def _params(*sem, vmem=VMEM_LIMIT):
    return pltpu.CompilerParams(dimension_semantics=sem, vmem_limit_bytes=vmem)


LOG2_E = math.log2(math.e)


def _silu(v):
    return v * (1.0 / (1.0 + jnp.exp2(v * -LOG2_E)))


def _for_row_tiles(n_rows, body, unroll):
    trips = n_rows // BF16_ROWS
    if trips == 1:
        body(0)
    else:
        def step(j, carry):
            body(pl.multiple_of(j * BF16_ROWS, BF16_ROWS))
            return carry
        lax.fori_loop(0, trips, step, 0, unroll=unroll)


def _inv_rms_rows(x_ref, r_ref, n_rows):
    lanes = r_ref.shape[1]
    width = x_ref.shape[1]

    def stats(r):
        x = x_ref[pl.ds(r, BF16_ROWS), :]
        sq = x * x
        part = sq[:, 0:lanes]
        for t in range(1, width // lanes):
            part = part + sq[:, t * lanes:(t + 1) * lanes]
        r_ref[pl.ds(r, BF16_ROWS), :] = part

    _for_row_tiles(n_rows, stats, 4)
    rows = slice(0, n_rows)
    ms = jnp.sum(r_ref[rows, :], axis=-1, keepdims=True) * (1.0 / width)
    r_ref[rows, :] = jnp.broadcast_to(lax.rsqrt(ms + EPS), (n_rows, lanes))


def _row_scale(r_ref, r, width):
    inv = r_ref[pl.ds(r, BF16_ROWS), :]
    return jnp.concatenate([inv] * (width // r_ref.shape[1]), axis=1)


def _norm_rows(x_ref, h_ref, r_ref, h_row0, n_rows, nw, sh, sc):
    gain = nw * (1.0 + sc)
    _inv_rms_rows(x_ref, r_ref, n_rows)

    def apply(r):
        x = x_ref[pl.ds(r, BF16_ROWS), :]
        dst = pl.ds(pl.multiple_of(h_row0 + r, BF16_ROWS), BF16_ROWS)
        h_ref[dst, :] = (x * _row_scale(r_ref, r, x.shape[1]) * gain + sh).astype(BF16)

    _for_row_tiles(n_rows, apply, 2)


def _scale_rows_by_inv_rms(o_ref, r_ref, n_rows, w):
    _inv_rms_rows(o_ref, r_ref, n_rows)

    def apply(r):
        rows = pl.ds(r, BF16_ROWS)
        x = o_ref[rows, :]
        o_ref[rows, :] = x * _row_scale(r_ref, r, x.shape[1]) * w

    _for_row_tiles(n_rows, apply, 2)


def _mod_row_fn(is_ctx, tm):
    if is_ctx:
        return lambda i: CTX_ROW
    return lambda i: i // (SEQ // tm)


def _rope_coeffs(cos, sin):
    even = lax.broadcasted_iota(jnp.int32, cos.shape, 1) % 2 == 0
    return cos, jnp.where(even, -sin, sin)


def _rope(seg, cos, sin_signed):
    lane = lax.broadcasted_iota(jnp.int32, (seg.shape[0], LANES), 1)
    partner = jnp.bitwise_xor(lane, 1)
    swapped = jnp.concatenate(
        [jnp.take_along_axis(seg[:, t * LANES:(t + 1) * LANES], partner, axis=1)
         for t in range(seg.shape[1] // LANES)], axis=1)
    return seg * cos + swapped * sin_signed


MOD_TN = 1024


def _mod_kernel(c_ref, w_ref, b_ref, o_ref):
    a = _silu(c_ref[...]).astype(BF16)
    o_ref[...] = jnp.dot(a, w_ref[...].astype(BF16), preferred_element_type=F32) + b_ref[...]


def _modulation(cmat, ada_w, ada_b):
    depth = ada_w.shape[0]
    return pl.pallas_call(
        _mod_kernel,
        grid=(depth, 6 * D // MOD_TN),
        in_specs=[
            pl.BlockSpec((MOD_ROWS, D), lambda l, n: (0, 0)),
            pl.BlockSpec((None, D, MOD_TN), lambda l, n: (l, 0, n)),
            pl.BlockSpec((None, 1, MOD_TN), lambda l, n: (l, 0, n)),
        ],
        out_specs=pl.BlockSpec((None, MOD_ROWS, MOD_TN), lambda l, n: (l, 0, n)),
        out_shape=jax.ShapeDtypeStruct((depth, MOD_ROWS, 6 * D), F32),
        compiler_params=_params("arbitrary", "arbitrary"),
        name="adaln_mod",
    )(cmat, ada_w, ada_b.reshape(depth, 1, 6 * D))


IN_TM = 1024
IN_TN = 1024


IN_PIECES = 4


class _RetInEpilogue:
    @staticmethod
    def kinds(n, is_ctx):
        if is_ctx:
            return [("plain", None)]
        is_qk = n < 2 * D // IN_TN
        return [("rope", is_qk), ("plain", jnp.logical_not(is_qk))]

    @staticmethod
    def apply(kind, acc, rows, n, o_ref, cos_ref, sin_ref, extra):
        is_k = jnp.logical_and(n >= D // IN_TN, n < 2 * D // IN_TN)
        kscale = jnp.where(is_k, RET_DK ** -0.5, 1.0).astype(F32)
        if kind == "plain":
            o_ref[rows, :] = (acc * kscale).astype(BF16)
            return
        tabs = _rope_coeffs(cos_ref[rows, :], sin_ref[rows, :])
        for s in range(IN_TN // RET_DK):
            cols = slice(s * RET_DK, (s + 1) * RET_DK)
            o_ref[rows, cols] = _rope(acc[:, cols] * kscale, *tabs).astype(BF16)


ATT_QSCALE = ATT_HD ** -0.5 * LOG2_E
ATT_Q_TILES = D // IN_TN


class _AttnInEpilogue:
    @staticmethod
    def kinds(n, is_ctx):
        if is_ctx:
            return [("kv_ctx", None)]
        is_q = n < ATT_Q_TILES
        return [("q", is_q), ("kv", jnp.logical_not(is_q))]

    @staticmethod
    def apply(kind, acc, rows, n, o_ref, cos_ref, sin_ref, extra):
        qn_ref, kn_ref = extra
        heads = IN_TN // ATT_HD
        normed = heads if kind == "q" else ATT_KV
        hw = qn_ref[...] if kind == "q" else kn_ref[...]
        scale = ATT_QSCALE if kind == "q" else 1.0
        if kind != "kv_ctx":
            tabs = _rope_coeffs(cos_ref[rows, :], sin_ref[rows, :])
        for s in range(normed):
            cols = slice(s * ATT_HD, (s + 1) * ATT_HD)
            seg = acc[:, cols]
            ms = jnp.mean(seg * seg, axis=-1, keepdims=True)
            head = seg * (lax.rsqrt(ms + EPS) * scale) * hw
            if kind != "kv_ctx":
                head = _rope(head, *tabs)
            o_ref[rows, cols] = head.astype(BF16)
        if normed < heads:
            cols = slice(normed * ATT_HD, heads * ATT_HD)
            o_ref[rows, cols] = acc[:, cols].astype(BF16)


def _in_proj_kernel(x_ref, nw_ref, sh_ref, sc_ref, w_ref, cos_ref, sin_ref, *rest,
                    epilogue, is_ctx, n_off, n_col_tiles, n_steps):
    *extra, o_ref, h_ref, r_ref, acc_a, acc_b = rest
    s = pl.program_id(0)

    @pl.when(jnp.logical_and(s % n_col_tiles == 0, s < n_steps - 1))
    def _():
        _norm_rows(x_ref, h_ref, r_ref, 0, IN_TM, nw_ref[...], sh_ref[...], sc_ref[...])

    @pl.when(s == 0)
    def _():
        acc_b[...] = jnp.zeros_like(acc_b)

    n_prev = n_off + jnp.maximum(s - 1, 0) % n_col_tiles
    pr = IN_TM // IN_PIECES

    def run(acc_w, acc_r, kind):
        w = w_ref[...].astype(BF16)
        for p in range(IN_PIECES):
            rows = slice(p * pr, (p + 1) * pr)
            acc_w[rows, :] = jnp.dot(h_ref[rows, :], w, preferred_element_type=F32)
            epilogue.apply(kind, acc_r[rows, :], rows, n_prev, o_ref, cos_ref, sin_ref, extra)

    for parity, (acc_w, acc_r) in enumerate(((acc_a, acc_b), (acc_b, acc_a))):
        for kind, cond in epilogue.kinds(n_prev, is_ctx):
            pred = s % 2 == parity
            if cond is not None:
                pred = jnp.logical_and(pred, cond)
            pl.when(pred)(functools.partial(run, acc_w, acc_r, kind))


def _in_proj(epilogue, name, src, is_ctx, n_off, n_cols, hd, nw, mod, w, tabs, extra):
    tm, tn = IN_TM, IN_TN
    tps = SEQ // tm
    n_row_tiles = src.shape[0] // tm
    nct = n_cols // tn
    n_steps = n_row_tiles * nct + 1
    mrow = _mod_row_fn(is_ctx, tm)

    def row_tile(s):
        return jnp.minimum(s // nct, n_row_tiles - 1)

    def prev(s):
        t = jnp.maximum(s - 1, 0)
        return t // nct, t % nct

    tab_spec = pl.BlockSpec((tm, hd), lambda s: (prev(s)[0] % tps, 0))
    return pl.pallas_call(
        functools.partial(_in_proj_kernel, epilogue=epilogue, is_ctx=is_ctx, n_off=n_off,
                          n_col_tiles=nct, n_steps=n_steps),
        grid=(n_steps,),
        in_specs=[
            pl.BlockSpec((tm, D), lambda s: (row_tile(s), 0), pipeline_mode=pl.Buffered(1)),
            pl.BlockSpec((1, D), lambda s: (0, 0)),
            pl.BlockSpec((None, 1, D), lambda s: (mrow(row_tile(s)), 0, 0)),
            pl.BlockSpec((None, 1, D), lambda s: (mrow(row_tile(s)), 0, 1)),
            pl.BlockSpec((D, tn), lambda s: (0, n_off + s % nct)),
            tab_spec, tab_spec,
        ] + [pl.BlockSpec(e.shape, lambda s: (0, 0)) for e in extra],
        out_specs=pl.BlockSpec((tm, tn), lambda s: prev(s)),
        out_shape=jax.ShapeDtypeStruct((src.shape[0], n_cols), BF16),
        scratch_shapes=[pltpu.VMEM((tm, D), BF16), pltpu.VMEM((tm, LANES), F32),
                        pltpu.VMEM((tm, tn), F32), pltpu.VMEM((tm, tn), F32)],
        compiler_params=_params("arbitrary"),
        name=name,
    )(src, nw, mod, mod, w, *tabs, *extra)


def _ret_scan_kernel(ld_ref, qx_ref, kx_ref, vx_ref, gx_ref, qc_ref, kc_ref, vc_ref, gc_ref, gnw_ref,
                     yx_ref, yc_ref, o_ref, gate_ref, sf_ref, sb_ref):
    h = pl.program_id(1)
    c = CHUNK
    nn = lax.broadcasted_iota(jnp.int32, (c, c), 0).astype(F32)
    mm = lax.broadcasted_iota(jnp.int32, (c, c), 1).astype(F32)
    idx = lax.broadcasted_iota(jnp.int32, (c, 1), 0).astype(F32)

    def log_gamma(direction, shape):
        return -jnp.exp(jnp.full(shape, ld_ref[direction, h], F32))

    lgf, lgb = log_gamma(0, (c, c)), log_gamma(1, (c, c))
    lgf1, lgb1 = log_gamma(0, (c, 1)), log_gamma(1, (c, 1))
    lgf0, lgb0 = log_gamma(0, (1, 1)), log_gamma(1, (1, 1))
    fwd = (sf_ref,
           jnp.where(nn >= mm, jnp.exp(lgf * jnp.where(nn >= mm, nn - mm, 0.0)), 0.0),
           jnp.exp(lgf1 * (idx + 1.0)), jnp.exp(lgf1 * (c - 1.0 - idx)), jnp.exp(lgf0 * c))
    bwd = (sb_ref,
           jnp.where(mm >= nn, jnp.exp(lgb * jnp.where(mm >= nn, mm - nn, 0.0)), 0.0),
           jnp.exp(lgb1 * (c - idx)), jnp.exp(lgb1 * idx), jnp.exp(lgb0 * c))
    gnw = gnw_ref[...]

    def chunk_step(direction, q_ref, k_ref, v_ref, rows):
        st_ref, decay, qd, kd, cd = direction
        q, k, v = q_ref[rows, :], k_ref[rows, :], v_ref[rows, :]
        state = st_ref[...]
        scores = lax.dot_general(q, k, NT_DIMS, preferred_element_type=F32) * decay
        out = jnp.dot(scores.astype(BF16), v, preferred_element_type=F32)
        out = out + jnp.dot(q, state.astype(BF16), preferred_element_type=F32) * qd
        kdec = k * kd.astype(BF16)
        st_ref[...] = state * cd + lax.dot_general(kdec, v, TN_DIMS, preferred_element_type=F32)
        return out

    def gate_of(g_ref, rows):
        return _silu(g_ref[rows, :].astype(F32)) * gnw

    def finish(tot, gate, y_ref, rows):
        mu = jnp.mean(tot, axis=-1, keepdims=True)
        cen = tot - mu
        var = jnp.mean(cen * cen, axis=-1, keepdims=True)
        y_ref[rows, :] = (gate * (cen * lax.rsqrt(var + EPS))).astype(BF16)

    def scan(q_ref, k_ref, v_ref, g_ref, y_ref, base, n_chunks):
        def rows_of(j):
            r = pl.multiple_of(j * c, c)
            return pl.ds(r, c), pl.ds(pl.multiple_of(base + r, c), c)

        def first_half(j, carry):
            for direction, jj in ((fwd, j), (bwd, n_chunks - 1 - j)):
                rows, orows = rows_of(jj)
                o_ref[orows, :] = chunk_step(direction, q_ref, k_ref, v_ref, rows)
                gate_ref[orows, :] = gate_of(g_ref, rows)
            return carry

        def second_half(j, carry):
            for direction, jj in ((fwd, j), (bwd, n_chunks - 1 - j)):
                rows, orows = rows_of(jj)
                tot = chunk_step(direction, q_ref, k_ref, v_ref, rows) + o_ref[orows, :]
                finish(tot, gate_ref[orows, :], y_ref, rows)
            return carry

        if n_chunks == 1:
            rows = pl.ds(0, c)
            tot = chunk_step(fwd, q_ref, k_ref, v_ref, rows) + chunk_step(bwd, q_ref, k_ref, v_ref, rows)
            finish(tot, gate_of(g_ref, rows), y_ref, rows)
        else:
            lax.fori_loop(0, n_chunks // 2, first_half, 0, unroll=2)
            lax.fori_loop(n_chunks // 2, n_chunks, second_half, 0, unroll=2)

    sf_ref[...] = jnp.zeros_like(sf_ref)
    sb_ref[...] = jnp.zeros_like(sb_ref)
    scan(qc_ref, kc_ref, vc_ref, gc_ref, yc_ref, SEQ, CTX // c)
    scan(qx_ref, kx_ref, vx_ref, gx_ref, yx_ref, 0, SEQ // c)


def _ret_scan(qkvg_x, qkvg_c, log_decay, gn_w):
    hq = D // RET_DK
    hv = 2 * D // RET_DV

    def specs(rows):
        return [
            pl.BlockSpec((rows, RET_DK), lambda b, h: (b, h)),
            pl.BlockSpec((rows, RET_DK), lambda b, h: (b, hq + h)),
            pl.BlockSpec((rows, RET_DV), lambda b, h: (b, hv + h)),
            pl.BlockSpec((rows, RET_DV), lambda b, h: (b, 2 * hv + h)),
        ]

    state = pltpu.VMEM((RET_DK, RET_DV), F32)
    return pl.pallas_call(
        _ret_scan_kernel,
        grid=(B, RET_HEADS),
        in_specs=[pl.BlockSpec(memory_space=pltpu.SMEM)] + specs(SEQ) + specs(CTX)
                 + [pl.BlockSpec((1, RET_DV), lambda b, h: (0, h))],
        out_specs=[
            pl.BlockSpec((SEQ, RET_DV), lambda b, h: (b, h)),
            pl.BlockSpec((CTX, RET_DV), lambda b, h: (b, h)),
        ],
        out_shape=[jax.ShapeDtypeStruct((NX, 2 * D), BF16), jax.ShapeDtypeStruct((NC, 2 * D), BF16)],
        scratch_shapes=[pltpu.VMEM((SEQ + CTX, RET_DV), F32), pltpu.VMEM((SEQ + CTX, RET_DV), F32), state, state],
        compiler_params=_params("arbitrary", "arbitrary"),
        name="ret_scan",
    )(log_decay, qkvg_x, qkvg_x, qkvg_x, qkvg_x, qkvg_c, qkvg_c, qkvg_c, qkvg_c, gn_w)


PROJ_TM = 1024
PROJ_TN = 512


def _proj_res_kernel(y_ref, w_ref, x_ref, g_ref, o_ref):
    acc = jnp.dot(y_ref[...], w_ref[...].astype(BF16), preferred_element_type=F32)
    o_ref[...] = x_ref[...] + g_ref[...] * acc


def _proj_res(name, y, w, res, mod, is_ctx):
    tm, tn = PROJ_TM, PROJ_TN
    rows, ky = y.shape
    mrow = _mod_row_fn(is_ctx, tm)
    gate0 = 2 * D // tn
    return pl.pallas_call(
        _proj_res_kernel,
        grid=(rows // tm, D // tn),
        in_specs=[
            pl.BlockSpec((tm, ky), lambda i, n: (i, 0)),
            pl.BlockSpec((ky, tn), lambda i, n: (0, n)),
            pl.BlockSpec((tm, tn), lambda i, n: (i, n)),
            pl.BlockSpec((None, 1, tn), lambda i, n: (mrow(i), 0, gate0 + n)),
        ],
        out_specs=pl.BlockSpec((tm, tn), lambda i, n: (i, n)),
        out_shape=jax.ShapeDtypeStruct((rows, D), F32),
        compiler_params=_params("arbitrary", "arbitrary"),
        name=name,
    )(y, w, res, mod)


FFN_TM = 1024
FFN_TF = 512
FFN_SUB = 256
FFN_SLAB = 256
HALO = BF16_ROWS


def _ffn_kernel(xp_ref, x_ref, xn_ref, nw_ref, sh_ref, sc_ref, g_ref, wa_ref, wb_ref,
                cwa_ref, cwb_ref, cba_ref, cbb_ref, wd_ref, fnw_ref, o_ref,
                h_ref, r_ref, *slab_refs, is_ctx, final_norm):
    tm = FFN_TM
    i = pl.program_id(0)
    f = pl.program_id(1)

    @pl.when(f == 0)
    def _():
        nw, sh, sc = nw_ref[...], sh_ref[...], sc_ref[...]
        _norm_rows(xp_ref, h_ref, r_ref, 0, HALO, nw, sh, sc)
        _norm_rows(x_ref, h_ref, r_ref, HALO, tm, nw, sh, sc)
        _norm_rows(xn_ref, h_ref, r_ref, HALO + tm, HALO, nw, sh, sc)
        o_ref[...] = jnp.zeros_like(o_ref)

    sub, slab = FFN_SUB, FFN_SLAB
    n_slabs = FFN_TF // slab
    n_blocks = tm // sub
    *u_refs, act_ref = slab_refs
    bounds = [0] + [2 * HALO + sub * (q + 1) for q in range(n_blocks - 1)] + [tm + 2 * HALO]

    def up_proj(s, q=None):
        rows = slice(0, tm + 2 * HALO) if q is None else slice(bounds[q], bounds[q + 1])
        cols = slice(slab * s, slab * (s + 1))
        hh = h_ref[rows, :]
        u_refs[2 * s][rows, :] = jnp.dot(hh, wa_ref[:, cols].astype(BF16), preferred_element_type=F32)
        u_refs[2 * s + 1][rows, :] = jnp.dot(hh, wb_ref[:, cols].astype(BF16), preferred_element_type=F32)

    def clear_outside_rows(s):
        tiles_per_seq = SEQ // tm
        at_start = i % tiles_per_seq == 0
        at_end = i % tiles_per_seq == tiles_per_seq - 1
        before = slice(HALO - 8, HALO)
        after = slice(HALO + tm, HALO + tm + 8)
        for u_ref in u_refs[2 * s:2 * s + 2]:
            u_ref[before, :] = jnp.where(at_start, 0.0, u_ref[before, :])
            u_ref[after, :] = jnp.where(at_end, 0.0, u_ref[after, :])

    def conv_act(s, q):
        cols = slice(slab * s, slab * (s + 1))
        r0 = sub * q
        lo = HALO + r0
        if is_ctx:
            pos = (r0 + lax.broadcasted_iota(jnp.int32, (sub, 1), 0)) % CTX
            has_prev = pos > 0
            has_next = pos < CTX - 1

        def conv(u_ref, cw_ref, cb_ref):
            prev = u_ref[lo - 1:lo - 1 + sub, :]
            cur = u_ref[lo:lo + sub, :]
            nxt = u_ref[lo + 1:lo + 1 + sub, :]
            if is_ctx:
                prev = jnp.where(has_prev, prev, 0.0)
                nxt = jnp.where(has_next, nxt, 0.0)
            return (prev * cw_ref[0:1, cols] + cur * cw_ref[1:2, cols] + nxt * cw_ref[2:3, cols]
                    + cb_ref[:, cols])

        a = conv(u_refs[2 * s], cwa_ref, cba_ref)
        b = conv(u_refs[2 * s + 1], cwb_ref, cbb_ref)
        act_ref[r0:r0 + sub, cols] = (_silu(a) * b).astype(BF16)

    up_proj(0)
    if not is_ctx:
        clear_outside_rows(0)
    for s in range(1, n_slabs):
        for q in range(n_blocks):
            up_proj(s, q)
            conv_act(s - 1, q)
        if not is_ctx:
            clear_outside_rows(s)
    wd = wd_ref[...].astype(BF16)
    for q in range(n_blocks):
        conv_act(n_slabs - 1, q)
        rows = slice(sub * q, sub * (q + 1))
        o_ref[rows, :] += jnp.dot(act_ref[rows, :], wd, preferred_element_type=F32)


    @pl.when(f == pl.num_programs(1) - 1)
    def _():
        gate = g_ref[...]
        for r0 in range(0, tm, sub):
            rows = slice(r0, r0 + sub)
            o_ref[rows, :] = x_ref[rows, :] + gate * o_ref[rows, :]
        if final_norm:
            _scale_rows_by_inv_rms(o_ref, r_ref, tm, fnw_ref[...])


def _conv_ffn(name, xs, nw, mod, layer, w_up, conv_w, conv_b, w_down, fnw, is_ctx, final_norm):
    tm, tf = FFN_TM, FFN_TF
    n_rows = xs.shape[0]
    nf = FFN // tf
    hb = tm // HALO
    last_hb = n_rows // HALO - 1
    mrow = _mod_row_fn(is_ctx, tm)

    def mod_spec(chunk):
        return pl.BlockSpec((None, 1, D), lambda i, f: (mrow(i), 0, chunk))

    return pl.pallas_call(
        functools.partial(_ffn_kernel, is_ctx=is_ctx, final_norm=final_norm),
        grid=(n_rows // tm, nf),
        in_specs=[
            pl.BlockSpec((HALO, D), lambda i, f: (jnp.maximum(i * hb - 1, 0), 0)),
            pl.BlockSpec((tm, D), lambda i, f: (i, 0), pipeline_mode=pl.Buffered(1)),
            pl.BlockSpec((HALO, D), lambda i, f: (jnp.minimum((i + 1) * hb, last_hb), 0)),
            pl.BlockSpec((1, D), lambda i, f: (0, 0)),
            mod_spec(3), mod_spec(4), mod_spec(5),
            pl.BlockSpec((None, D, tf), lambda i, f: (layer, 0, f)),
            pl.BlockSpec((None, D, tf), lambda i, f: (layer, 0, nf + f)),
            pl.BlockSpec((None, 3, tf), lambda i, f: (layer, 0, f)),
            pl.BlockSpec((None, 3, tf), lambda i, f: (layer, 0, nf + f)),
            pl.BlockSpec((None, 1, tf), lambda i, f: (layer, 0, f)),
            pl.BlockSpec((None, 1, tf), lambda i, f: (layer, 0, nf + f)),
            pl.BlockSpec((None, tf, D), lambda i, f: (layer, f, 0)),
            pl.BlockSpec((1, D), lambda i, f: (0, 0)),
        ],
        out_specs=pl.BlockSpec((tm, D), lambda i, f: (i, 0), pipeline_mode=pl.Buffered(1)),
        out_shape=jax.ShapeDtypeStruct((n_rows, D), F32),
        scratch_shapes=[
            pltpu.VMEM((tm + 2 * HALO, D), BF16),
            pltpu.VMEM((tm, LANES), F32),
        ] + [
            pltpu.VMEM((tm + 2 * HALO, FFN_SLAB), F32),
            pltpu.VMEM((tm + 2 * HALO, FFN_SLAB), F32),
        ] * (tf // FFN_SLAB) + [
            pltpu.VMEM((tm, tf), BF16),
        ],
        compiler_params=_params("arbitrary", "arbitrary", vmem=VMEM_LIMIT_FFN),
        name=name,
    )(xs, xs, xs, nw, mod, mod, mod, w_up, w_up, conv_w, conv_w, conv_b, conv_b, w_down, fnw)


ATT_TQ = 1024
ATT_UNIT = 128


def _attn_kernel(q_ref, kx_ref, kc_ref, vx_ref, vc_ref, o_ref):
    kx, kc, vx, vc = kx_ref[...], kc_ref[...], vx_ref[...], vc_ref[...]
    for u in range(ATT_TQ // ATT_UNIT):
        r0 = u * ATT_UNIT
        q = q_ref[r0:r0 + ATT_UNIT, :]
        qs = jnp.concatenate([q[:, j * ATT_HD:(j + 1) * ATT_HD] for j in range(ATT_GROUP)], axis=0)
        sx = lax.dot_general(qs, kx, NT_DIMS, preferred_element_type=F32)
        sc = lax.dot_general(qs, kc, NT_DIMS, preferred_element_type=F32)
        m = jnp.maximum(jnp.max(sx, axis=-1, keepdims=True), jnp.max(sc, axis=-1, keepdims=True))
        px = jnp.exp2(sx - m)
        pc = jnp.exp2(sc - m)
        denom = jnp.sum(px, axis=-1, keepdims=True) + jnp.sum(pc, axis=-1, keepdims=True)
        out = jnp.dot(px.astype(BF16), vx, preferred_element_type=F32)
        out = out + jnp.dot(pc.astype(BF16), vc, preferred_element_type=F32)
        out = out * (1.0 / denom)
        for j in range(ATT_GROUP):
            o_ref[r0:r0 + ATT_UNIT, j * ATT_HD:(j + 1) * ATT_HD] = (
                out[j * ATT_UNIT:(j + 1) * ATT_UNIT, :].astype(BF16))


def _attention(qkv_x, kv_c):
    tq = ATT_TQ
    qb = SEQ // tq
    k0 = D // ATT_HD
    v0 = k0 + ATT_KV
    return pl.pallas_call(
        _attn_kernel,
        grid=(B, ATT_KV, qb),
        in_specs=[
            pl.BlockSpec((tq, ATT_GROUP * ATT_HD), lambda b, g, t: (b * qb + t, g)),
            pl.BlockSpec((SEQ, ATT_HD), lambda b, g, t: (b, k0 + g)),
            pl.BlockSpec((CTX, ATT_HD), lambda b, g, t: (b, g)),
            pl.BlockSpec((SEQ, ATT_HD), lambda b, g, t: (b, v0 + g)),
            pl.BlockSpec((CTX, ATT_HD), lambda b, g, t: (b, ATT_KV + g)),
        ],
        out_specs=pl.BlockSpec((tq, ATT_GROUP * ATT_HD), lambda b, g, t: (b * qb + t, g)),
        out_shape=jax.ShapeDtypeStruct((NX, D), BF16),
        compiler_params=_params("arbitrary", "arbitrary", "arbitrary"),
        name="gqa_attention",
    )(qkv_x, qkv_x, kv_c, qkv_x, kv_c)


def _rope_tables(head_dim):
    rows = SEQ // GRID_W
    row = jnp.repeat(jnp.arange(rows, dtype=F32), GRID_W)
    col = jnp.tile(jnp.arange(GRID_W, dtype=F32), rows)
    n_freq = head_dim // 4
    inv = ROPE_THETA ** (-jnp.arange(n_freq, dtype=F32) / n_freq)
    ang = jnp.concatenate([row[:, None] * inv, col[:, None] * inv], axis=-1)
    return jnp.repeat(jnp.cos(ang), 2, axis=-1), jnp.repeat(jnp.sin(ang), 2, axis=-1)


def kernel(x, c, ctx, c_ctx, ada_w, ada_b, norm_w, ret_w_in, ret_w_out, ret_log_decay, ret_gn_w,
           attn_w_in, attn_w_out, attn_q_norm, attn_k_norm, ffn_w_up, ffn_conv_w, ffn_conv_b,
           ffn_w_down, final_norm_w):
    xx = x.reshape(NX, D)
    xc = ctx.reshape(NC, D)
    cmat = jnp.concatenate([c, c_ctx[None, :], jnp.zeros((MOD_ROWS - B - 1, D), F32)], axis=0)
    mod = _modulation(cmat, ada_w, ada_b)
    mod0 = mod[0].reshape(MOD_ROWS, 1, 6 * D)
    mod1 = mod[1].reshape(MOD_ROWS, 1, 6 * D)
    fnw = final_norm_w.reshape(1, D)

    nw = norm_w[0, 0].reshape(1, D)
    tabs = _rope_tables(RET_DK)
    qkvg_x = _in_proj(_RetInEpilogue, "ret_in_x", xx, False, 0, 6 * D, RET_DK, nw, mod0, ret_w_in[0], tabs, [])
    qkvg_c = _in_proj(_RetInEpilogue, "ret_in_ctx", xc, True, 0, 6 * D, RET_DK, nw, mod0, ret_w_in[0], tabs, [])
    yx, yc = _ret_scan(qkvg_x, qkvg_c, ret_log_decay[0], ret_gn_w[0].reshape(1, 2 * D))
    xx = _proj_res("ret_out_x", yx, ret_w_out[0], xx, mod0, False)
    xc = _proj_res("ret_out_ctx", yc, ret_w_out[0], xc, mod0, True)
    nw = norm_w[0, 1].reshape(1, D)
    ffn = (ffn_w_up, ffn_conv_w, ffn_conv_b.reshape(-1, 1, 2 * FFN), ffn_w_down)
    xx = _conv_ffn("conv_ffn_x", xx, nw, mod0, 0, *ffn, fnw, False, False)
    xc = _conv_ffn("conv_ffn_ctx", xc, nw, mod0, 0, *ffn, fnw, True, False)

    nw = norm_w[1, 0].reshape(1, D)
    tabs = _rope_tables(ATT_HD)
    heads = [attn_q_norm[0].reshape(1, ATT_HD), attn_k_norm[0].reshape(1, ATT_HD)]
    qkv_x = _in_proj(_AttnInEpilogue, "attn_in_x", xx, False, 0, ATT_IN, ATT_HD, nw, mod1, attn_w_in[0],
                     tabs, heads)
    kv_c = _in_proj(_AttnInEpilogue, "attn_in_ctx", xc, True, ATT_Q_TILES, ATT_IN - D, ATT_HD, nw, mod1,
                    attn_w_in[0], tabs, heads)
    ya = _attention(qkv_x, kv_c)
    xx = _proj_res("attn_out", ya, attn_w_out[0], xx, mod1, False)
    out = _conv_ffn("conv_ffn_out", xx, norm_w[1, 1].reshape(1, D), mod1, 1, *ffn, fnw, False, True)
    return out.reshape(B, SEQ, D)
```

```python
import functools
import math

import jax
import jax.numpy as jnp
from jax import lax
from jax.experimental import pallas as pl
from jax.experimental.pallas import tpu as pltpu

D = 2048
B = 4
SEQ = 2048
CTX = 256
GRID_W = 64
RET_HEADS = 8
RET_DK = D // RET_HEADS
RET_DV = 2 * D // RET_HEADS
CHUNK = 256
ATT_HEADS = 16
ATT_KV = 4
ATT_HD = D // ATT_HEADS
ATT_GROUP = ATT_HEADS // ATT_KV
ATT_IN = (ATT_HEADS + 2 * ATT_KV) * ATT_HD
FFN = 256 * ((8 * D // 3 + 255) // 256)
ROPE_THETA = 10000.0
EPS = 1e-6

NX = B * SEQ
NC = B * CTX
CTX_ROW = B
MOD_ROWS = 8

F32 = jnp.float32
BF16 = jnp.bfloat16
BF16_ROWS = 16
LANES = 128

V7X_VMEM_BYTES = 64 * 1024 * 1024
VMEM_LIMIT = V7X_VMEM_BYTES - 8 * 1024 * 1024
VMEM_LIMIT_FFN = V7X_VMEM_BYTES - 4 * 1024 * 1024

NT_DIMS = (((1,), (1,)), ((), ()))
TN_DIMS = (((0,), (0,)), ((), ()))


def _params(*sem, vmem=VMEM_LIMIT):
    return pltpu.CompilerParams(dimension_semantics=sem, vmem_limit_bytes=vmem)


LOG2_E = math.log2(math.e)


def _silu(v):
    return v * (1.0 / (1.0 + jnp.exp2(v * -LOG2_E)))


def _for_row_tiles(n_rows, body, unroll):
    trips = n_rows // BF16_ROWS
    if trips == 1:
        body(0)
    else:
        def step(j, carry):
            body(pl.multiple_of(j * BF16_ROWS, BF16_ROWS))
            return carry
        lax.fori_loop(0, trips, step, 0, unroll=unroll)


def _inv_rms_rows(x_ref, r_ref, n_rows):
    lanes = r_ref.shape[1]
    width = x_ref.shape[1]

    def stats(r):
        x = x_ref[pl.ds(r, BF16_ROWS), :]
        sq = x * x
        part = sq[:, 0:lanes]
        for t in range(1, width // lanes):
            part = part + sq[:, t * lanes:(t + 1) * lanes]
        r_ref[pl.ds(r, BF16_ROWS), :] = part

    _for_row_tiles(n_rows, stats, 4)
    rows = slice(0, n_rows)
    ms = jnp.sum(r_ref[rows, :], axis=-1, keepdims=True) * (1.0 / width)
    r_ref[rows, :] = jnp.broadcast_to(lax.rsqrt(ms + EPS), (n_rows, lanes))


def _row_scale(r_ref, r, width):
    inv = r_ref[pl.ds(r, BF16_ROWS), :]
    return jnp.concatenate([inv] * (width // r_ref.shape[1]), axis=1)


def _norm_rows(x_ref, h_ref, r_ref, h_row0, n_rows, nw, sh, sc):
    gain = nw * (1.0 + sc)
    _inv_rms_rows(x_ref, r_ref, n_rows)

    def apply(r):
        x = x_ref[pl.ds(r, BF16_ROWS), :]
        dst = pl.ds(pl.multiple_of(h_row0 + r, BF16_ROWS), BF16_ROWS)
        h_ref[dst, :] = (x * _row_scale(r_ref, r, x.shape[1]) * gain + sh).astype(BF16)

    _for_row_tiles(n_rows, apply, 2)


def _scale_rows_by_inv_rms(o_ref, r_ref, n_rows, w):
    _inv_rms_rows(o_ref, r_ref, n_rows)

    def apply(r):
        rows = pl.ds(r, BF16_ROWS)
        x = o_ref[rows, :]
        o_ref[rows, :] = x * _row_scale(r_ref, r, x.shape[1]) * w

    _for_row_tiles(n_rows, apply, 2)


def _mod_row_fn(is_ctx, tm):
    if is_ctx:
        return lambda i: CTX_ROW
    return lambda i: i // (SEQ // tm)


def _rope_coeffs(cos, sin):
    even = lax.broadcasted_iota(jnp.int32, cos.shape, 1) % 2 == 0
    return cos, jnp.where(even, -sin, sin)


def _rope(seg, cos, sin_signed):
    lane = lax.broadcasted_iota(jnp.int32, (seg.shape[0], LANES), 1)
    partner = jnp.bitwise_xor(lane, 1)
    swapped = jnp.concatenate(
        [jnp.take_along_axis(seg[:, t * LANES:(t + 1) * LANES], partner, axis=1)
         for t in range(seg.shape[1] // LANES)], axis=1)
    return seg * cos + swapped * sin_signed


MOD_TN = 1024


def _mod_kernel(c_ref, w_ref, b_ref, o_ref):
    a = _silu(c_ref[...]).astype(BF16)
    o_ref[...] = jnp.dot(a, w_ref[...].astype(BF16), preferred_element_type=F32) + b_ref[...]


def _modulation(cmat, ada_w, ada_b):
    depth = ada_w.shape[0]
    return pl.pallas_call(
        _mod_kernel,
        grid=(depth, 6 * D // MOD_TN),
        in_specs=[
            pl.BlockSpec((MOD_ROWS, D), lambda l, n: (0, 0)),
            pl.BlockSpec((None, D, MOD_TN), lambda l, n: (l, 0, n)),
            pl.BlockSpec((None, 1, MOD_TN), lambda l, n: (l, 0, n)),
        ],
        out_specs=pl.BlockSpec((None, MOD_ROWS, MOD_TN), lambda l, n: (l, 0, n)),
        out_shape=jax.ShapeDtypeStruct((depth, MOD_ROWS, 6 * D), F32),
        compiler_params=_params("arbitrary", "arbitrary"),
        name="adaln_mod",
    )(cmat, ada_w, ada_b.reshape(depth, 1, 6 * D))


IN_TM = 1024
IN_TN = 1024


IN_PIECES = 4


class _RetInEpilogue:
    @staticmethod
    def kinds(n, is_ctx):
        if is_ctx:
            return [("plain", None)]
        is_qk = n < 2 * D // IN_TN
        return [("rope", is_qk), ("plain", jnp.logical_not(is_qk))]

    @staticmethod
    def apply(kind, acc, rows, n, o_ref, cos_ref, sin_ref, extra):
        is_k = jnp.logical_and(n >= D // IN_TN, n < 2 * D // IN_TN)
        kscale = jnp.where(is_k, RET_DK ** -0.5, 1.0).astype(F32)
        if kind == "plain":
            o_ref[rows, :] = (acc * kscale).astype(BF16)
            return
        tabs = _rope_coeffs(cos_ref[rows, :], sin_ref[rows, :])
        for s in range(IN_TN // RET_DK):
            cols = slice(s * RET_DK, (s + 1) * RET_DK)
            o_ref[rows, cols] = _rope(acc[:, cols] * kscale, *tabs).astype(BF16)


ATT_QSCALE = ATT_HD ** -0.5 * LOG2_E
ATT_Q_TILES = D // IN_TN


class _AttnInEpilogue:
    @staticmethod
    def kinds(n, is_ctx):
        if is_ctx:
            return [("kv_ctx", None)]
        is_q = n < ATT_Q_TILES
        return [("q", is_q), ("kv", jnp.logical_not(is_q))]

    @staticmethod
    def apply(kind, acc, rows, n, o_ref, cos_ref, sin_ref, extra):
        qn_ref, kn_ref = extra
        heads = IN_TN // ATT_HD
        normed = heads if kind == "q" else ATT_KV
        hw = qn_ref[...] if kind == "q" else kn_ref[...]
        scale = ATT_QSCALE if kind == "q" else 1.0
        if kind != "kv_ctx":
            tabs = _rope_coeffs(cos_ref[rows, :], sin_ref[rows, :])
        ones = jnp.ones((ATT_HD, ATT_HD), BF16)
        for s in range(normed):
            cols = slice(s * ATT_HD, (s + 1) * ATT_HD)
            seg = acc[:, cols]
            ssum = jnp.dot((seg * seg).astype(BF16), ones, preferred_element_type=F32)
            head = seg * (lax.rsqrt(ssum * (1.0 / ATT_HD) + EPS) * scale) * hw
            if kind != "kv_ctx":
                head = _rope(head, *tabs)
            o_ref[rows, cols] = head.astype(BF16)
        if normed < heads:
            cols = slice(normed * ATT_HD, heads * ATT_HD)
            o_ref[rows, cols] = acc[:, cols].astype(BF16)


def _in_proj_kernel(x_ref, nw_ref, sh_ref, sc_ref, w_ref, cos_ref, sin_ref, *rest,
                    epilogue, is_ctx, n_off, n_col_tiles, n_steps):
    *extra, o_ref, h_ref, r_ref, acc_a, acc_b = rest
    s = pl.program_id(0)

    @pl.when(jnp.logical_and(s % n_col_tiles == 0, s < n_steps - 1))
    def _():
        _norm_rows(x_ref, h_ref, r_ref, 0, IN_TM, nw_ref[...], sh_ref[...], sc_ref[...])

    @pl.when(s == 0)
    def _():
        acc_b[...] = jnp.zeros_like(acc_b)

    n_prev = n_off + jnp.maximum(s - 1, 0) % n_col_tiles
    pr = IN_TM // IN_PIECES

    def run(acc_w, acc_r, kind):
        w = w_ref[...].astype(BF16)
        for p in range(IN_PIECES):
            rows = slice(p * pr, (p + 1) * pr)
            acc_w[rows, :] = jnp.dot(h_ref[rows, :], w, preferred_element_type=F32)
            epilogue.apply(kind, acc_r[rows, :], rows, n_prev, o_ref, cos_ref, sin_ref, extra)

    for parity, (acc_w, acc_r) in enumerate(((acc_a, acc_b), (acc_b, acc_a))):
        for kind, cond in epilogue.kinds(n_prev, is_ctx):
            pred = s % 2 == parity
            if cond is not None:
                pred = jnp.logical_and(pred, cond)
            pl.when(pred)(functools.partial(run, acc_w, acc_r, kind))


def _in_proj(epilogue, name, src, is_ctx, n_off, n_cols, hd, nw, mod, w, tabs, extra):
    tm, tn = IN_TM, IN_TN
    tps = SEQ // tm
    n_row_tiles = src.shape[0] // tm
    nct = n_cols // tn
    n_steps = n_row_tiles * nct + 1
    mrow = _mod_row_fn(is_ctx, tm)

    def row_tile(s):
        return jnp.minimum(s // nct, n_row_tiles - 1)

    def prev(s):
        t = jnp.maximum(s - 1, 0)
        return t // nct, t % nct

    tab_spec = pl.BlockSpec((tm, hd), lambda s: (prev(s)[0] % tps, 0))
    return pl.pallas_call(
        functools.partial(_in_proj_kernel, epilogue=epilogue, is_ctx=is_ctx, n_off=n_off,
                          n_col_tiles=nct, n_steps=n_steps),
        grid=(n_steps,),
        in_specs=[
            pl.BlockSpec((tm, D), lambda s: (row_tile(s), 0), pipeline_mode=pl.Buffered(1)),
            pl.BlockSpec((1, D), lambda s: (0, 0)),
            pl.BlockSpec((None, 1, D), lambda s: (mrow(row_tile(s)), 0, 0)),
            pl.BlockSpec((None, 1, D), lambda s: (mrow(row_tile(s)), 0, 1)),
            pl.BlockSpec((D, tn), lambda s: (0, n_off + s % nct)),
            tab_spec, tab_spec,
        ] + [pl.BlockSpec(e.shape, lambda s: (0, 0)) for e in extra],
        out_specs=pl.BlockSpec((tm, tn), lambda s: prev(s)),
        out_shape=jax.ShapeDtypeStruct((src.shape[0], n_cols), BF16),
        scratch_shapes=[pltpu.VMEM((tm, D), BF16), pltpu.VMEM((tm, LANES), F32),
                        pltpu.VMEM((tm, tn), F32), pltpu.VMEM((tm, tn), F32)],
        compiler_params=_params("arbitrary"),
        name=name,
    )(src, nw, mod, mod, w, *tabs, *extra)


def _ret_scan_kernel(ld_ref, qx_ref, kx_ref, vx_ref, gx_ref, qc_ref, kc_ref, vc_ref, gc_ref, gnw_ref,
                     yx_ref, yc_ref, o_ref, gate_ref, sf_ref, sb_ref):
    h = pl.program_id(1)
    c = CHUNK
    nn = lax.broadcasted_iota(jnp.int32, (c, c), 0).astype(F32)
    mm = lax.broadcasted_iota(jnp.int32, (c, c), 1).astype(F32)
    idx = lax.broadcasted_iota(jnp.int32, (c, 1), 0).astype(F32)

    def log_gamma(direction, shape):
        return -jnp.exp(jnp.full(shape, ld_ref[direction, h], F32))

    lgf, lgb = log_gamma(0, (c, c)), log_gamma(1, (c, c))
    lgf1, lgb1 = log_gamma(0, (c, 1)), log_gamma(1, (c, 1))
    lgf0, lgb0 = log_gamma(0, (1, 1)), log_gamma(1, (1, 1))
    fwd = (sf_ref,
           jnp.where(nn >= mm, jnp.exp(lgf * jnp.where(nn >= mm, nn - mm, 0.0)), 0.0),
           jnp.exp(lgf1 * (idx + 1.0)), jnp.exp(lgf1 * (c - 1.0 - idx)), jnp.exp(lgf0 * c))
    bwd = (sb_ref,
           jnp.where(mm >= nn, jnp.exp(lgb * jnp.where(mm >= nn, mm - nn, 0.0)), 0.0),
           jnp.exp(lgb1 * (c - idx)), jnp.exp(lgb1 * idx), jnp.exp(lgb0 * c))
    gnw = gnw_ref[...]

    def chunk_step(direction, q_ref, k_ref, v_ref, rows):
        st_ref, decay, qd, kd, cd = direction
        q, k, v = q_ref[rows, :], k_ref[rows, :], v_ref[rows, :]
        state = st_ref[...]
        scores = lax.dot_general(q, k, NT_DIMS, preferred_element_type=F32) * decay
        out = jnp.dot(scores.astype(BF16), v, preferred_element_type=F32)
        out = out + jnp.dot(q, state.astype(BF16), preferred_element_type=F32) * qd
        kdec = k * kd.astype(BF16)
        st_ref[...] = state * cd + lax.dot_general(kdec, v, TN_DIMS, preferred_element_type=F32)
        return out

    def gate_of(g_ref, rows):
        return _silu(g_ref[rows, :].astype(F32)) * gnw

    def finish(tot, gate, y_ref, rows):
        mu = jnp.mean(tot, axis=-1, keepdims=True)
        cen = tot - mu
        var = jnp.mean(cen * cen, axis=-1, keepdims=True)
        y_ref[rows, :] = (gate * (cen * lax.rsqrt(var + EPS))).astype(BF16)

    def scan(q_ref, k_ref, v_ref, g_ref, y_ref, base, n_chunks):
        def rows_of(j):
            r = pl.multiple_of(j * c, c)
            return pl.ds(r, c), pl.ds(pl.multiple_of(base + r, c), c)

        def first_half(j, carry):
            for direction, jj in ((fwd, j), (bwd, n_chunks - 1 - j)):
                rows, orows = rows_of(jj)
                o_ref[orows, :] = chunk_step(direction, q_ref, k_ref, v_ref, rows)
                gate_ref[orows, :] = gate_of(g_ref, rows)
            return carry

        def second_half(j, carry):
            for direction, jj in ((fwd, j), (bwd, n_chunks - 1 - j)):
                rows, orows = rows_of(jj)
                tot = chunk_step(direction, q_ref, k_ref, v_ref, rows) + o_ref[orows, :]
                finish(tot, gate_ref[orows, :], y_ref, rows)
            return carry

        if n_chunks == 1:
            rows = pl.ds(0, c)
            tot = chunk_step(fwd, q_ref, k_ref, v_ref, rows) + chunk_step(bwd, q_ref, k_ref, v_ref, rows)
            finish(tot, gate_of(g_ref, rows), y_ref, rows)
        else:
            lax.fori_loop(0, n_chunks // 2, first_half, 0, unroll=2)
            lax.fori_loop(n_chunks // 2, n_chunks, second_half, 0, unroll=2)

    sf_ref[...] = jnp.zeros_like(sf_ref)
    sb_ref[...] = jnp.zeros_like(sb_ref)
    scan(qc_ref, kc_ref, vc_ref, gc_ref, yc_ref, SEQ, CTX // c)
    scan(qx_ref, kx_ref, vx_ref, gx_ref, yx_ref, 0, SEQ // c)


def _ret_scan(qkvg_x, qkvg_c, log_decay, gn_w):
    hq = D // RET_DK
    hv = 2 * D // RET_DV

    def specs(rows):
        return [
            pl.BlockSpec((rows, RET_DK), lambda b, h: (b, h)),
            pl.BlockSpec((rows, RET_DK), lambda b, h: (b, hq + h)),
            pl.BlockSpec((rows, RET_DV), lambda b, h: (b, hv + h)),
            pl.BlockSpec((rows, RET_DV), lambda b, h: (b, 2 * hv + h)),
        ]

    state = pltpu.VMEM((RET_DK, RET_DV), F32)
    return pl.pallas_call(
        _ret_scan_kernel,
        grid=(B, RET_HEADS),
        in_specs=[pl.BlockSpec(memory_space=pltpu.SMEM)] + specs(SEQ) + specs(CTX)
                 + [pl.BlockSpec((1, RET_DV), lambda b, h: (0, h))],
        out_specs=[
            pl.BlockSpec((SEQ, RET_DV), lambda b, h: (b, h)),
            pl.BlockSpec((CTX, RET_DV), lambda b, h: (b, h)),
        ],
        out_shape=[jax.ShapeDtypeStruct((NX, 2 * D), BF16), jax.ShapeDtypeStruct((NC, 2 * D), BF16)],
        scratch_shapes=[pltpu.VMEM((SEQ + CTX, RET_DV), F32), pltpu.VMEM((SEQ + CTX, RET_DV), F32), state, state],
        compiler_params=_params("arbitrary", "arbitrary"),
        name="ret_scan",
    )(log_decay, qkvg_x, qkvg_x, qkvg_x, qkvg_x, qkvg_c, qkvg_c, qkvg_c, qkvg_c, gn_w)


PROJ_TM = 1024
PROJ_TN = 512


def _proj_res_kernel(y_ref, w_ref, x_ref, g_ref, o_ref):
    acc = jnp.dot(y_ref[...], w_ref[...].astype(BF16), preferred_element_type=F32)
    o_ref[...] = x_ref[...] + g_ref[...] * acc


def _proj_res(name, y, w, res, mod, is_ctx):
    tm, tn = PROJ_TM, PROJ_TN
    rows, ky = y.shape
    mrow = _mod_row_fn(is_ctx, tm)
    gate0 = 2 * D // tn
    return pl.pallas_call(
        _proj_res_kernel,
        grid=(rows // tm, D // tn),
        in_specs=[
            pl.BlockSpec((tm, ky), lambda i, n: (i, 0)),
            pl.BlockSpec((ky, tn), lambda i, n: (0, n)),
            pl.BlockSpec((tm, tn), lambda i, n: (i, n)),
            pl.BlockSpec((None, 1, tn), lambda i, n: (mrow(i), 0, gate0 + n)),
        ],
        out_specs=pl.BlockSpec((tm, tn), lambda i, n: (i, n)),
        out_shape=jax.ShapeDtypeStruct((rows, D), F32),
        compiler_params=_params("arbitrary", "arbitrary"),
        name=name,
    )(y, w, res, mod)


FFN_TM = 1024
FFN_TF = 512
FFN_SUB = 256
FFN_SLAB = 256
HALO = BF16_ROWS


def _ffn_kernel(xp_ref, x_ref, xn_ref, nw_ref, sh_ref, sc_ref, g_ref, wa_ref, wb_ref,
                cwa_ref, cwb_ref, cba_ref, cbb_ref, wd_ref, fnw_ref, o_ref,
                h_ref, r_ref, *slab_refs, is_ctx, final_norm):
    tm = FFN_TM
    i = pl.program_id(0)
    f = pl.program_id(1)

    @pl.when(f == 0)
    def _():
        nw, sh, sc = nw_ref[...], sh_ref[...], sc_ref[...]
        _norm_rows(xp_ref, h_ref, r_ref, 0, HALO, nw, sh, sc)
        _norm_rows(x_ref, h_ref, r_ref, HALO, tm, nw, sh, sc)
        _norm_rows(xn_ref, h_ref, r_ref, HALO + tm, HALO, nw, sh, sc)
        o_ref[...] = jnp.zeros_like(o_ref)

    sub, slab = FFN_SUB, FFN_SLAB
    n_slabs = FFN_TF // slab
    n_blocks = tm // sub
    *u_refs, act_ref = slab_refs
    bounds = [0] + [2 * HALO + sub * (q + 1) for q in range(n_blocks - 1)] + [tm + 2 * HALO]

    def up_proj(s, q=None):
        rows = slice(0, tm + 2 * HALO) if q is None else slice(bounds[q], bounds[q + 1])
        cols = slice(slab * s, slab * (s + 1))
        hh = h_ref[rows, :]
        u_refs[2 * s][rows, :] = jnp.dot(hh, wa_ref[:, cols].astype(BF16), preferred_element_type=F32)
        u_refs[2 * s + 1][rows, :] = jnp.dot(hh, wb_ref[:, cols].astype(BF16), preferred_element_type=F32)

    def clear_outside_rows(s):
        tiles_per_seq = SEQ // tm
        at_start = i % tiles_per_seq == 0
        at_end = i % tiles_per_seq == tiles_per_seq - 1
        before = slice(HALO - 8, HALO)
        after = slice(HALO + tm, HALO + tm + 8)
        for u_ref in u_refs[2 * s:2 * s + 2]:
            u_ref[before, :] = jnp.where(at_start, 0.0, u_ref[before, :])
            u_ref[after, :] = jnp.where(at_end, 0.0, u_ref[after, :])

    def conv_act(s, q):
        cols = slice(slab * s, slab * (s + 1))
        r0 = sub * q
        lo = HALO + r0
        if is_ctx:
            pos = (r0 + lax.broadcasted_iota(jnp.int32, (sub, 1), 0)) % CTX
            has_prev = pos > 0
            has_next = pos < CTX - 1

        def conv(u_ref, cw_ref, cb_ref):
            prev = u_ref[lo - 1:lo - 1 + sub, :]
            cur = u_ref[lo:lo + sub, :]
            nxt = u_ref[lo + 1:lo + 1 + sub, :]
            if is_ctx:
                prev = jnp.where(has_prev, prev, 0.0)
                nxt = jnp.where(has_next, nxt, 0.0)
            return (prev * cw_ref[0:1, cols] + cur * cw_ref[1:2, cols] + nxt * cw_ref[2:3, cols]
                    + cb_ref[:, cols])

        a = conv(u_refs[2 * s], cwa_ref, cba_ref)
        b = conv(u_refs[2 * s + 1], cwb_ref, cbb_ref)
        act_ref[r0:r0 + sub, cols] = (_silu(a) * b).astype(BF16)

    up_proj(0)
    if not is_ctx:
        clear_outside_rows(0)
    for s in range(1, n_slabs):
        for q in range(n_blocks):
            up_proj(s, q)
            conv_act(s - 1, q)
        if not is_ctx:
            clear_outside_rows(s)
    wd = wd_ref[...].astype(BF16)
    for q in range(n_blocks):
        conv_act(n_slabs - 1, q)
        rows = slice(sub * q, sub * (q + 1))
        o_ref[rows, :] += jnp.dot(act_ref[rows, :], wd, preferred_element_type=F32)


    @pl.when(f == pl.num_programs(1) - 1)
    def _():
        gate = g_ref[...]
        for r0 in range(0, tm, sub):
            rows = slice(r0, r0 + sub)
            o_ref[rows, :] = x_ref[rows, :] + gate * o_ref[rows, :]
        if final_norm:
            _scale_rows_by_inv_rms(o_ref, r_ref, tm, fnw_ref[...])


def _conv_ffn(name, xs, nw, mod, layer, w_up, conv_w, conv_b, w_down, fnw, is_ctx, final_norm):
    tm, tf = FFN_TM, FFN_TF
    n_rows = xs.shape[0]
    nf = FFN // tf
    hb = tm // HALO
    last_hb = n_rows // HALO - 1
    mrow = _mod_row_fn(is_ctx, tm)

    def mod_spec(chunk):
        return pl.BlockSpec((None, 1, D), lambda i, f: (mrow(i), 0, chunk))

    return pl.pallas_call(
        functools.partial(_ffn_kernel, is_ctx=is_ctx, final_norm=final_norm),
        grid=(n_rows // tm, nf),
        in_specs=[
            pl.BlockSpec((HALO, D), lambda i, f: (jnp.maximum(i * hb - 1, 0), 0)),
            pl.BlockSpec((tm, D), lambda i, f: (i, 0), pipeline_mode=pl.Buffered(1)),
            pl.BlockSpec((HALO, D), lambda i, f: (jnp.minimum((i + 1) * hb, last_hb), 0)),
            pl.BlockSpec((1, D), lambda i, f: (0, 0)),
            mod_spec(3), mod_spec(4), mod_spec(5),
            pl.BlockSpec((None, D, tf), lambda i, f: (layer, 0, f)),
            pl.BlockSpec((None, D, tf), lambda i, f: (layer, 0, nf + f)),
            pl.BlockSpec((None, 3, tf), lambda i, f: (layer, 0, f)),
            pl.BlockSpec((None, 3, tf), lambda i, f: (layer, 0, nf + f)),
            pl.BlockSpec((None, 1, tf), lambda i, f: (layer, 0, f)),
            pl.BlockSpec((None, 1, tf), lambda i, f: (layer, 0, nf + f)),
            pl.BlockSpec((None, tf, D), lambda i, f: (layer, f, 0)),
            pl.BlockSpec((1, D), lambda i, f: (0, 0)),
        ],
        out_specs=pl.BlockSpec((tm, D), lambda i, f: (i, 0), pipeline_mode=pl.Buffered(1)),
        out_shape=jax.ShapeDtypeStruct((n_rows, D), F32),
        scratch_shapes=[
            pltpu.VMEM((tm + 2 * HALO, D), BF16),
            pltpu.VMEM((tm, LANES), F32),
        ] + [
            pltpu.VMEM((tm + 2 * HALO, FFN_SLAB), F32),
            pltpu.VMEM((tm + 2 * HALO, FFN_SLAB), F32),
        ] * (tf // FFN_SLAB) + [
            pltpu.VMEM((tm, tf), BF16),
        ],
        compiler_params=_params("arbitrary", "arbitrary", vmem=VMEM_LIMIT_FFN),
        name=name,
    )(xs, xs, xs, nw, mod, mod, mod, w_up, w_up, conv_w, conv_w, conv_b, conv_b, w_down, fnw)


ATT_TQ = 1024
ATT_UNIT = 128


def _attn_kernel(q_ref, kx_ref, kc_ref, vx_ref, vc_ref, o_ref):
    kx, kc, vx, vc = kx_ref[...], kc_ref[...], vx_ref[...], vc_ref[...]
    for u in range(ATT_TQ // ATT_UNIT):
        r0 = u * ATT_UNIT
        q = q_ref[r0:r0 + ATT_UNIT, :]
        qs = jnp.concatenate([q[:, j * ATT_HD:(j + 1) * ATT_HD] for j in range(ATT_GROUP)], axis=0)
        sx = lax.dot_general(qs, kx, NT_DIMS, preferred_element_type=F32)
        sc = lax.dot_general(qs, kc, NT_DIMS, preferred_element_type=F32)
        m = jnp.maximum(jnp.max(sx, axis=-1, keepdims=True), jnp.max(sc, axis=-1, keepdims=True))
        px = jnp.exp2(sx - m)
        pc = jnp.exp2(sc - m)
        denom = jnp.sum(px, axis=-1, keepdims=True) + jnp.sum(pc, axis=-1, keepdims=True)
        out = jnp.dot(px.astype(BF16), vx, preferred_element_type=F32)
        out = out + jnp.dot(pc.astype(BF16), vc, preferred_element_type=F32)
        out = out * (1.0 / denom)
        for j in range(ATT_GROUP):
            o_ref[r0:r0 + ATT_UNIT, j * ATT_HD:(j + 1) * ATT_HD] = (
                out[j * ATT_UNIT:(j + 1) * ATT_UNIT, :].astype(BF16))


def _attention(qkv_x, kv_c):
    tq = ATT_TQ
    qb = SEQ // tq
    k0 = D // ATT_HD
    v0 = k0 + ATT_KV
    return pl.pallas_call(
        _attn_kernel,
        grid=(B, ATT_KV, qb),
        in_specs=[
            pl.BlockSpec((tq, ATT_GROUP * ATT_HD), lambda b, g, t: (b * qb + t, g)),
            pl.BlockSpec((SEQ, ATT_HD), lambda b, g, t: (b, k0 + g)),
            pl.BlockSpec((CTX, ATT_HD), lambda b, g, t: (b, g)),
            pl.BlockSpec((SEQ, ATT_HD), lambda b, g, t: (b, v0 + g)),
            pl.BlockSpec((CTX, ATT_HD), lambda b, g, t: (b, ATT_KV + g)),
        ],
        out_specs=pl.BlockSpec((tq, ATT_GROUP * ATT_HD), lambda b, g, t: (b * qb + t, g)),
        out_shape=jax.ShapeDtypeStruct((NX, D), BF16),
        compiler_params=_params("arbitrary", "arbitrary", "arbitrary"),
        name="gqa_attention",
    )(qkv_x, qkv_x, kv_c, qkv_x, kv_c)


def _rope_tables(head_dim):
    rows = SEQ // GRID_W
    row = jnp.repeat(jnp.arange(rows, dtype=F32), GRID_W)
    col = jnp.tile(jnp.arange(GRID_W, dtype=F32), rows)
    n_freq = head_dim // 4
    inv = ROPE_THETA ** (-jnp.arange(n_freq, dtype=F32) / n_freq)
    ang = jnp.concatenate([row[:, None] * inv, col[:, None] * inv], axis=-1)
    return jnp.repeat(jnp.cos(ang), 2, axis=-1), jnp.repeat(jnp.sin(ang), 2, axis=-1)


def kernel(x, c, ctx, c_ctx, ada_w, ada_b, norm_w, ret_w_in, ret_w_out, ret_log_decay, ret_gn_w,
           attn_w_in, attn_w_out, attn_q_norm, attn_k_norm, ffn_w_up, ffn_conv_w, ffn_conv_b,
           ffn_w_down, final_norm_w):
    xx = x.reshape(NX, D)
    xc = ctx.reshape(NC, D)
    cmat = jnp.concatenate([c, c_ctx[None, :], jnp.zeros((MOD_ROWS - B - 1, D), F32)], axis=0)
    mod = _modulation(cmat, ada_w, ada_b)
    mod0 = mod[0].reshape(MOD_ROWS, 1, 6 * D)
    mod1 = mod[1].reshape(MOD_ROWS, 1, 6 * D)
    fnw = final_norm_w.reshape(1, D)

    nw = norm_w[0, 0].reshape(1, D)
    tabs = _rope_tables(RET_DK)
    qkvg_x = _in_proj(_RetInEpilogue, "ret_in_x", xx, False, 0, 6 * D, RET_DK, nw, mod0, ret_w_in[0], tabs, [])
    qkvg_c = _in_proj(_RetInEpilogue, "ret_in_ctx", xc, True, 0, 6 * D, RET_DK, nw, mod0, ret_w_in[0], tabs, [])
    yx, yc = _ret_scan(qkvg_x, qkvg_c, ret_log_decay[0], ret_gn_w[0].reshape(1, 2 * D))
    xx = _proj_res("ret_out_x", yx, ret_w_out[0], xx, mod0, False)
    xc = _proj_res("ret_out_ctx", yc, ret_w_out[0], xc, mod0, True)
    nw = norm_w[0, 1].reshape(1, D)
    ffn = (ffn_w_up, ffn_conv_w, ffn_conv_b.reshape(-1, 1, 2 * FFN), ffn_w_down)
    xx = _conv_ffn("conv_ffn_x", xx, nw, mod0, 0, *ffn, fnw, False, False)
    xc = _conv_ffn("conv_ffn_ctx", xc, nw, mod0, 0, *ffn, fnw, True, False)

    nw = norm_w[1, 0].reshape(1, D)
    tabs = _rope_tables(ATT_HD)
    heads = [attn_q_norm[0].reshape(1, ATT_HD), attn_k_norm[0].reshape(1, ATT_HD)]
    qkv_x = _in_proj(_AttnInEpilogue, "attn_in_x", xx, False, 0, ATT_IN, ATT_HD, nw, mod1, attn_w_in[0],
                     tabs, heads)
    kv_c = _in_proj(_AttnInEpilogue, "attn_in_ctx", xc, True, ATT_Q_TILES, ATT_IN - D, ATT_HD, nw, mod1,
                    attn_w_in[0], tabs, heads)
    ya = _attention(qkv_x, kv_c)
    xx = _proj_res("attn_out", ya, attn_w_out[0], xx, mod1, False)
    out = _conv_ffn("conv_ffn_out", xx, norm_w[1, 1].reshape(1, D), mod1, 1, *ffn, fnw, False, True)
    return out.reshape(B, SEQ, D)
```

```python
import functools
import math

import jax
import jax.numpy as jnp
from jax import lax
from jax.experimental import pallas as pl
from jax.experimental.pallas import tpu as pltpu

D = 2048
B = 4
SEQ = 2048
CTX = 256
GRID_W = 64
RET_HEADS = 8
RET_DK = D // RET_HEADS
RET_DV = 2 * D // RET_HEADS
CHUNK = 256
ATT_HEADS = 16
ATT_KV = 4
ATT_HD = D // ATT_HEADS
ATT_GROUP = ATT_HEADS // ATT_KV
ATT_IN = (ATT_HEADS + 2 * ATT_KV) * ATT_HD
FFN = 256 * ((8 * D // 3 + 255) // 256)
ROPE_THETA = 10000.0
EPS = 1e-6

NX = B * SEQ
NC = B * CTX
CTX_ROW = B
MOD_ROWS = 8

F32 = jnp.float32
BF16 = jnp.bfloat16
BF16_ROWS = 16
LANES = 128

V7X_VMEM_BYTES = 64 * 1024 * 1024
VMEM_LIMIT = V7X_VMEM_BYTES - 8 * 1024 * 1024
VMEM_LIMIT_FFN = V7X_VMEM_BYTES - 4 * 1024 * 1024

NT_DIMS = (((1,), (1,)), ((), ()))
TN_DIMS = (((0,), (0,)), ((), ()))


def _params(*sem, vmem=VMEM_LIMIT):
    return pltpu.CompilerParams(dimension_semantics=sem, vmem_limit_bytes=vmem)


LOG2_E = math.log2(math.e)


def _silu(v):
    return v * (1.0 / (1.0 + jnp.exp2(v * -LOG2_E)))


def _for_row_tiles(n_rows, body, unroll):
    trips = n_rows // BF16_ROWS
    if trips == 1:
        body(0)
    else:
        def step(j, carry):
            body(pl.multiple_of(j * BF16_ROWS, BF16_ROWS))
            return carry
        lax.fori_loop(0, trips, step, 0, unroll=unroll)


def _inv_rms_rows(x_ref, r_ref, n_rows):
    lanes = r_ref.shape[1]
    width = x_ref.shape[1]

    def stats(r):
        x = x_ref[pl.ds(r, BF16_ROWS), :]
        sq = x * x
        part = sq[:, 0:lanes]
        for t in range(1, width // lanes):
            part = part + sq[:, t * lanes:(t + 1) * lanes]
        r_ref[pl.ds(r, BF16_ROWS), :] = part

    _for_row_tiles(n_rows, stats, 4)
    rows = slice(0, n_rows)
    ms = jnp.sum(r_ref[rows, :], axis=-1, keepdims=True) * (1.0 / width)
    r_ref[rows, :] = jnp.broadcast_to(lax.rsqrt(ms + EPS), (n_rows, lanes))


def _row_scale(r_ref, r, width):
    inv = r_ref[pl.ds(r, BF16_ROWS), :]
    return jnp.concatenate([inv] * (width // r_ref.shape[1]), axis=1)


def _norm_rows(x_ref, h_ref, r_ref, h_row0, n_rows, nw, sh, sc):
    gain = nw * (1.0 + sc)
    _inv_rms_rows(x_ref, r_ref, n_rows)

    def apply(r):
        x = x_ref[pl.ds(r, BF16_ROWS), :]
        dst = pl.ds(pl.multiple_of(h_row0 + r, BF16_ROWS), BF16_ROWS)
        h_ref[dst, :] = (x * _row_scale(r_ref, r, x.shape[1]) * gain + sh).astype(BF16)

    _for_row_tiles(n_rows, apply, 2)


def _scale_rows_by_inv_rms(o_ref, r_ref, n_rows, w):
    _inv_rms_rows(o_ref, r_ref, n_rows)

    def apply(r):
        rows = pl.ds(r, BF16_ROWS)
        x = o_ref[rows, :]
        o_ref[rows, :] = x * _row_scale(r_ref, r, x.shape[1]) * w

    _for_row_tiles(n_rows, apply, 2)


def _mod_row_fn(is_ctx, tm):
    if is_ctx:
        return lambda i: CTX_ROW
    return lambda i: i // (SEQ // tm)


def _rope_coeffs(cos, sin):
    even = lax.broadcasted_iota(jnp.int32, cos.shape, 1) % 2 == 0
    return cos, jnp.where(even, -sin, sin)


def _rope(seg, cos, sin_signed):
    lane = lax.broadcasted_iota(jnp.int32, (seg.shape[0], LANES), 1)
    partner = jnp.bitwise_xor(lane, 1)
    swapped = jnp.concatenate(
        [jnp.take_along_axis(seg[:, t * LANES:(t + 1) * LANES], partner, axis=1)
         for t in range(seg.shape[1] // LANES)], axis=1)
    return seg * cos + swapped * sin_signed


MOD_TN = 2048


def _mod_kernel(c_ref, w_ref, b_ref, o_ref):
    a = _silu(c_ref[...]).astype(BF16)
    o_ref[...] = jnp.dot(a, w_ref[...].astype(BF16), preferred_element_type=F32) + b_ref[...]


def _modulation(cmat, ada_w, ada_b):
    depth = ada_w.shape[0]
    return pl.pallas_call(
        _mod_kernel,
        grid=(depth, 6 * D // MOD_TN),
        in_specs=[
            pl.BlockSpec((MOD_ROWS, D), lambda l, n: (0, 0)),
            pl.BlockSpec((None, D, MOD_TN), lambda l, n: (l, 0, n)),
            pl.BlockSpec((None, 1, MOD_TN), lambda l, n: (l, 0, n)),
        ],
        out_specs=pl.BlockSpec((None, MOD_ROWS, MOD_TN), lambda l, n: (l, 0, n)),
        out_shape=jax.ShapeDtypeStruct((depth, MOD_ROWS, 6 * D), F32),
        compiler_params=_params("arbitrary", "arbitrary"),
        name="adaln_mod",
    )(cmat, ada_w, ada_b.reshape(depth, 1, 6 * D))


IN_TM = 1024
IN_TN = 1024


IN_PIECES = 4


class _RetInEpilogue:
    @staticmethod
    def kinds(n, is_ctx):
        if is_ctx:
            return [("plain", None)]
        is_qk = n < 2 * D // IN_TN
        return [("rope", is_qk), ("plain", jnp.logical_not(is_qk))]

    @staticmethod
    def apply(kind, acc, rows, n, o_ref, cos_ref, sin_ref, extra):
        is_k = jnp.logical_and(n >= D // IN_TN, n < 2 * D // IN_TN)
        kscale = jnp.where(is_k, RET_DK ** -0.5, 1.0).astype(F32)
        if kind == "plain":
            o_ref[rows, :] = (acc * kscale).astype(BF16)
            return
        tabs = _rope_coeffs(cos_ref[rows, :], sin_ref[rows, :])
        for s in range(IN_TN // RET_DK):
            cols = slice(s * RET_DK, (s + 1) * RET_DK)
            o_ref[rows, cols] = _rope(acc[:, cols] * kscale, *tabs).astype(BF16)


ATT_QSCALE = ATT_HD ** -0.5 * LOG2_E
ATT_Q_TILES = D // IN_TN


class _AttnInEpilogue:
    @staticmethod
    def kinds(n, is_ctx):
        if is_ctx:
            return [("kv_ctx", None)]
        is_q = n < ATT_Q_TILES
        return [("q", is_q), ("kv", jnp.logical_not(is_q))]

    @staticmethod
    def apply(kind, acc, rows, n, o_ref, cos_ref, sin_ref, extra):
        qn_ref, kn_ref = extra
        heads = IN_TN // ATT_HD
        normed = heads if kind == "q" else ATT_KV
        hw = qn_ref[...] if kind == "q" else kn_ref[...]
        scale = ATT_QSCALE if kind == "q" else 1.0
        if kind != "kv_ctx":
            tabs = _rope_coeffs(cos_ref[rows, :], sin_ref[rows, :])
        for s in range(normed):
            cols = slice(s * ATT_HD, (s + 1) * ATT_HD)
            seg = acc[:, cols]
            ms = jnp.mean(seg * seg, axis=-1, keepdims=True)
            head = seg * (lax.rsqrt(ms + EPS) * scale) * hw
            if kind != "kv_ctx":
                head = _rope(head, *tabs)
            o_ref[rows, cols] = head.astype(BF16)
        if normed < heads:
            cols = slice(normed * ATT_HD, heads * ATT_HD)
            o_ref[rows, cols] = acc[:, cols].astype(BF16)


def _in_proj_kernel(x_ref, nw_ref, sh_ref, sc_ref, w_ref, cos_ref, sin_ref, *rest,
                    epilogue, is_ctx, n_off, n_col_tiles, n_steps):
    *extra, o_ref, h_ref, r_ref, acc_a, acc_b = rest
    s = pl.program_id(0)

    @pl.when(jnp.logical_and(s % n_col_tiles == 0, s < n_steps - 1))
    def _():
        _norm_rows(x_ref, h_ref, r_ref, 0, IN_TM, nw_ref[...], sh_ref[...], sc_ref[...])

    @pl.when(s == 0)
    def _():
        acc_b[...] = jnp.zeros_like(acc_b)

    n_prev = n_off + jnp.maximum(s - 1, 0) % n_col_tiles
    pr = IN_TM // IN_PIECES

    def run(acc_w, acc_r, kind):
        w = w_ref[...].astype(BF16)
        for p in range(IN_PIECES):
            rows = slice(p * pr, (p + 1) * pr)
            acc_w[rows, :] = jnp.dot(h_ref[rows, :], w, preferred_element_type=F32)
            epilogue.apply(kind, acc_r[rows, :], rows, n_prev, o_ref, cos_ref, sin_ref, extra)

    for parity, (acc_w, acc_r) in enumerate(((acc_a, acc_b), (acc_b, acc_a))):
        for kind, cond in epilogue.kinds(n_prev, is_ctx):
            pred = s % 2 == parity
            if cond is not None:
                pred = jnp.logical_and(pred, cond)
            pl.when(pred)(functools.partial(run, acc_w, acc_r, kind))


def _in_proj(epilogue, name, src, is_ctx, n_off, n_cols, hd, nw, mod, w, tabs, extra):
    tm, tn = IN_TM, IN_TN
    tps = SEQ // tm
    n_row_tiles = src.shape[0] // tm
    nct = n_cols // tn
    n_steps = n_row_tiles * nct + 1
    mrow = _mod_row_fn(is_ctx, tm)

    def row_tile(s):
        return jnp.minimum(s // nct, n_row_tiles - 1)

    def prev(s):
        t = jnp.maximum(s - 1, 0)
        return t // nct, t % nct

    tab_spec = pl.BlockSpec((tm, hd), lambda s: (prev(s)[0] % tps, 0))
    return pl.pallas_call(
        functools.partial(_in_proj_kernel, epilogue=epilogue, is_ctx=is_ctx, n_off=n_off,
                          n_col_tiles=nct, n_steps=n_steps),
        grid=(n_steps,),
        in_specs=[
            pl.BlockSpec((tm, D), lambda s: (row_tile(s), 0), pipeline_mode=pl.Buffered(1)),
            pl.BlockSpec((1, D), lambda s: (0, 0)),
            pl.BlockSpec((None, 1, D), lambda s: (mrow(row_tile(s)), 0, 0)),
            pl.BlockSpec((None, 1, D), lambda s: (mrow(row_tile(s)), 0, 1)),
            pl.BlockSpec((D, tn), lambda s: (0, n_off + s % nct)),
            tab_spec, tab_spec,
        ] + [pl.BlockSpec(e.shape, lambda s: (0, 0)) for e in extra],
        out_specs=pl.BlockSpec((tm, tn), lambda s: prev(s)),
        out_shape=jax.ShapeDtypeStruct((src.shape[0], n_cols), BF16),
        scratch_shapes=[pltpu.VMEM((tm, D), BF16), pltpu.VMEM((tm, LANES), F32),
                        pltpu.VMEM((tm, tn), F32), pltpu.VMEM((tm, tn), F32)],
        compiler_params=_params("arbitrary"),
        name=name,
    )(src, nw, mod, mod, w, *tabs, *extra)


def _ret_scan_kernel(ld_ref, qx_ref, kx_ref, vx_ref, gx_ref, qc_ref, kc_ref, vc_ref, gc_ref, gnw_ref,
                     yx_ref, yc_ref, o_ref, gate_ref, sf_ref, sb_ref):
    h = pl.program_id(1)
    c = CHUNK
    nn = lax.broadcasted_iota(jnp.int32, (c, c), 0).astype(F32)
    mm = lax.broadcasted_iota(jnp.int32, (c, c), 1).astype(F32)
    idx = lax.broadcasted_iota(jnp.int32, (c, 1), 0).astype(F32)

    def log_gamma(direction, shape):
        return -jnp.exp(jnp.full(shape, ld_ref[direction, h], F32))

    lgf, lgb = log_gamma(0, (c, c)), log_gamma(1, (c, c))
    lgf1, lgb1 = log_gamma(0, (c, 1)), log_gamma(1, (c, 1))
    lgf0, lgb0 = log_gamma(0, (1, 1)), log_gamma(1, (1, 1))
    fwd = (sf_ref,
           jnp.where(nn >= mm, jnp.exp(lgf * jnp.where(nn >= mm, nn - mm, 0.0)), 0.0),
           jnp.exp(lgf1 * (idx + 1.0)), jnp.exp(lgf1 * (c - 1.0 - idx)), jnp.exp(lgf0 * c))
    bwd = (sb_ref,
           jnp.where(mm >= nn, jnp.exp(lgb * jnp.where(mm >= nn, mm - nn, 0.0)), 0.0),
           jnp.exp(lgb1 * (c - idx)), jnp.exp(lgb1 * idx), jnp.exp(lgb0 * c))
    gnw = gnw_ref[...]

    def chunk_step(direction, q_ref, k_ref, v_ref, rows):
        st_ref, decay, qd, kd, cd = direction
        q, k, v = q_ref[rows, :], k_ref[rows, :], v_ref[rows, :]
        state = st_ref[...]
        scores = lax.dot_general(q, k, NT_DIMS, preferred_element_type=F32) * decay
        out = jnp.dot(scores.astype(BF16), v, preferred_element_type=F32)
        out = out + jnp.dot(q, state.astype(BF16), preferred_element_type=F32) * qd
        kdec = k * kd.astype(BF16)
        st_ref[...] = state * cd + lax.dot_general(kdec, v, TN_DIMS, preferred_element_type=F32)
        return out

    def gate_of(g_ref, rows):
        return _silu(g_ref[rows, :].astype(F32)) * gnw

    def finish(tot, gate, y_ref, rows):
        mu = jnp.mean(tot, axis=-1, keepdims=True)
        cen = tot - mu
        var = jnp.mean(cen * cen, axis=-1, keepdims=True)
        y_ref[rows, :] = (gate * (cen * lax.rsqrt(var + EPS))).astype(BF16)

    def scan(q_ref, k_ref, v_ref, g_ref, y_ref, base, n_chunks):
        def rows_of(j):
            r = pl.multiple_of(j * c, c)
            return pl.ds(r, c), pl.ds(pl.multiple_of(base + r, c), c)

        def first_half(j, carry):
            for direction, jj in ((fwd, j), (bwd, n_chunks - 1 - j)):
                rows, orows = rows_of(jj)
                o_ref[orows, :] = chunk_step(direction, q_ref, k_ref, v_ref, rows)
                gate_ref[orows, :] = gate_of(g_ref, rows)
            return carry

        def second_half(j, carry):
            for direction, jj in ((fwd, j), (bwd, n_chunks - 1 - j)):
                rows, orows = rows_of(jj)
                tot = chunk_step(direction, q_ref, k_ref, v_ref, rows) + o_ref[orows, :]
                finish(tot, gate_ref[orows, :], y_ref, rows)
            return carry

        if n_chunks == 1:
            rows = pl.ds(0, c)
            tot = chunk_step(fwd, q_ref, k_ref, v_ref, rows) + chunk_step(bwd, q_ref, k_ref, v_ref, rows)
            finish(tot, gate_of(g_ref, rows), y_ref, rows)
        else:
            lax.fori_loop(0, n_chunks // 2, first_half, 0, unroll=2)
            lax.fori_loop(n_chunks // 2, n_chunks, second_half, 0, unroll=2)

    sf_ref[...] = jnp.zeros_like(sf_ref)
    sb_ref[...] = jnp.zeros_like(sb_ref)
    scan(qc_ref, kc_ref, vc_ref, gc_ref, yc_ref, SEQ, CTX // c)
    scan(qx_ref, kx_ref, vx_ref, gx_ref, yx_ref, 0, SEQ // c)


def _ret_scan(qkvg_x, qkvg_c, log_decay, gn_w):
    hq = D // RET_DK
    hv = 2 * D // RET_DV

    def specs(rows):
        return [
            pl.BlockSpec((rows, RET_DK), lambda b, h: (b, h)),
            pl.BlockSpec((rows, RET_DK), lambda b, h: (b, hq + h)),
            pl.BlockSpec((rows, RET_DV), lambda b, h: (b, hv + h)),
            pl.BlockSpec((rows, RET_DV), lambda b, h: (b, 2 * hv + h)),
        ]

    state = pltpu.VMEM((RET_DK, RET_DV), F32)
    return pl.pallas_call(
        _ret_scan_kernel,
        grid=(B, RET_HEADS),
        in_specs=[pl.BlockSpec(memory_space=pltpu.SMEM)] + specs(SEQ) + specs(CTX)
                 + [pl.BlockSpec((1, RET_DV), lambda b, h: (0, h))],
        out_specs=[
            pl.BlockSpec((SEQ, RET_DV), lambda b, h: (b, h)),
            pl.BlockSpec((CTX, RET_DV), lambda b, h: (b, h)),
        ],
        out_shape=[jax.ShapeDtypeStruct((NX, 2 * D), BF16), jax.ShapeDtypeStruct((NC, 2 * D), BF16)],
        scratch_shapes=[pltpu.VMEM((SEQ + CTX, RET_DV), F32), pltpu.VMEM((SEQ + CTX, RET_DV), F32), state, state],
        compiler_params=_params("arbitrary", "arbitrary"),
        name="ret_scan",
    )(log_decay, qkvg_x, qkvg_x, qkvg_x, qkvg_x, qkvg_c, qkvg_c, qkvg_c, qkvg_c, gn_w)


PROJ_TM = 1024
PROJ_TN = 512
PROJ_K_REF = 2 * D


def _proj_res_kernel(y_ref, w_ref, x_ref, g_ref, o_ref):
    acc = jnp.dot(y_ref[...], w_ref[...].astype(BF16), preferred_element_type=F32)
    o_ref[...] = x_ref[...] + g_ref[...] * acc


def _proj_res(name, y, w, res, mod, is_ctx):
    rows, ky = y.shape
    tm = PROJ_TM
    tn = PROJ_TN * max(1, PROJ_K_REF // ky)
    mrow = _mod_row_fn(is_ctx, tm)
    gate0 = 2 * D // tn
    return pl.pallas_call(
        _proj_res_kernel,
        grid=(rows // tm, D // tn),
        in_specs=[
            pl.BlockSpec((tm, ky), lambda i, n: (i, 0)),
            pl.BlockSpec((ky, tn), lambda i, n: (0, n)),
            pl.BlockSpec((tm, tn), lambda i, n: (i, n)),
            pl.BlockSpec((None, 1, tn), lambda i, n: (mrow(i), 0, gate0 + n)),
        ],
        out_specs=pl.BlockSpec((tm, tn), lambda i, n: (i, n)),
        out_shape=jax.ShapeDtypeStruct((rows, D), F32),
        compiler_params=_params("arbitrary", "arbitrary"),
        name=name,
    )(y, w, res, mod)


FFN_TM = 1024
FFN_TF = 512
FFN_SUB = 256
FFN_SLAB = 256
HALO = BF16_ROWS


def _ffn_kernel(xp_ref, x_ref, xn_ref, nw_ref, sh_ref, sc_ref, g_ref, wa_ref, wb_ref,
                cwa_ref, cwb_ref, cba_ref, cbb_ref, wd_ref, fnw_ref, o_ref,
                h_ref, r_ref, *slab_refs, is_ctx, final_norm):
    tm = FFN_TM
    i = pl.program_id(0)
    f = pl.program_id(1)

    @pl.when(f == 0)
    def _():
        nw, sh, sc = nw_ref[...], sh_ref[...], sc_ref[...]
        _norm_rows(xp_ref, h_ref, r_ref, 0, HALO, nw, sh, sc)
        _norm_rows(x_ref, h_ref, r_ref, HALO, tm, nw, sh, sc)
        _norm_rows(xn_ref, h_ref, r_ref, HALO + tm, HALO, nw, sh, sc)
        o_ref[...] = jnp.zeros_like(o_ref)

    sub, slab = FFN_SUB, FFN_SLAB
    n_slabs = FFN_TF // slab
    n_blocks = tm // sub
    *u_refs, act_ref = slab_refs
    bounds = [0] + [2 * HALO + sub * (q + 1) for q in range(n_blocks - 1)] + [tm + 2 * HALO]

    def up_proj(s, q=None):
        rows = slice(0, tm + 2 * HALO) if q is None else slice(bounds[q], bounds[q + 1])
        cols = slice(slab * s, slab * (s + 1))
        hh = h_ref[rows, :]
        u_refs[2 * s][rows, :] = jnp.dot(hh, wa_ref[:, cols].astype(BF16), preferred_element_type=F32)
        u_refs[2 * s + 1][rows, :] = jnp.dot(hh, wb_ref[:, cols].astype(BF16), preferred_element_type=F32)

    def clear_outside_rows(s):
        tiles_per_seq = SEQ // tm
        at_start = i % tiles_per_seq == 0
        at_end = i % tiles_per_seq == tiles_per_seq - 1
        before = slice(HALO - 8, HALO)
        after = slice(HALO + tm, HALO + tm + 8)
        for u_ref in u_refs[2 * s:2 * s + 2]:
            u_ref[before, :] = jnp.where(at_start, 0.0, u_ref[before, :])
            u_ref[after, :] = jnp.where(at_end, 0.0, u_ref[after, :])

    def conv_act(s, q):
        cols = slice(slab * s, slab * (s + 1))
        r0 = sub * q
        lo = HALO + r0
        if is_ctx:
            pos = (r0 + lax.broadcasted_iota(jnp.int32, (sub, 1), 0)) % CTX
            has_prev = pos > 0
            has_next = pos < CTX - 1

        def conv(u_ref, cw_ref, cb_ref):
            prev = u_ref[lo - 1:lo - 1 + sub, :]
            cur = u_ref[lo:lo + sub, :]
            nxt = u_ref[lo + 1:lo + 1 + sub, :]
            if is_ctx:
                prev = jnp.where(has_prev, prev, 0.0)
                nxt = jnp.where(has_next, nxt, 0.0)
            return (prev * cw_ref[0:1, cols] + cur * cw_ref[1:2, cols] + nxt * cw_ref[2:3, cols]
                    + cb_ref[:, cols])

        a = conv(u_refs[2 * s], cwa_ref, cba_ref)
        b = conv(u_refs[2 * s + 1], cwb_ref, cbb_ref)
        act_ref[r0:r0 + sub, cols] = (_silu(a) * b).astype(BF16)

    up_proj(0)
    if not is_ctx:
        clear_outside_rows(0)
    for s in range(1, n_slabs):
        for q in range(n_blocks):
            up_proj(s, q)
            conv_act(s - 1, q)
        if not is_ctx:
            clear_outside_rows(s)
    wd = wd_ref[...].astype(BF16)
    for q in range(n_blocks):
        conv_act(n_slabs - 1, q)
        rows = slice(sub * q, sub * (q + 1))
        o_ref[rows, :] += jnp.dot(act_ref[rows, :], wd, preferred_element_type=F32)


    @pl.when(f == pl.num_programs(1) - 1)
    def _():
        gate = g_ref[...]
        for r0 in range(0, tm, sub):
            rows = slice(r0, r0 + sub)
            o_ref[rows, :] = x_ref[rows, :] + gate * o_ref[rows, :]
        if final_norm:
            _scale_rows_by_inv_rms(o_ref, r_ref, tm, fnw_ref[...])


def _conv_ffn(name, xs, nw, mod, layer, w_up, conv_w, conv_b, w_down, fnw, is_ctx, final_norm):
    tm, tf = FFN_TM, FFN_TF
    n_rows = xs.shape[0]
    nf = FFN // tf
    hb = tm // HALO
    last_hb = n_rows // HALO - 1
    mrow = _mod_row_fn(is_ctx, tm)

    def mod_spec(chunk):
        return pl.BlockSpec((None, 1, D), lambda i, f: (mrow(i), 0, chunk))

    return pl.pallas_call(
        functools.partial(_ffn_kernel, is_ctx=is_ctx, final_norm=final_norm),
        grid=(n_rows // tm, nf),
        in_specs=[
            pl.BlockSpec((HALO, D), lambda i, f: (jnp.maximum(i * hb - 1, 0), 0)),
            pl.BlockSpec((tm, D), lambda i, f: (i, 0), pipeline_mode=pl.Buffered(1)),
            pl.BlockSpec((HALO, D), lambda i, f: (jnp.minimum((i + 1) * hb, last_hb), 0)),
            pl.BlockSpec((1, D), lambda i, f: (0, 0)),
            mod_spec(3), mod_spec(4), mod_spec(5),
            pl.BlockSpec((None, D, tf), lambda i, f: (layer, 0, f)),
            pl.BlockSpec((None, D, tf), lambda i, f: (layer, 0, nf + f)),
            pl.BlockSpec((None, 3, tf), lambda i, f: (layer, 0, f)),
            pl.BlockSpec((None, 3, tf), lambda i, f: (layer, 0, nf + f)),
            pl.BlockSpec((None, 1, tf), lambda i, f: (layer, 0, f)),
            pl.BlockSpec((None, 1, tf), lambda i, f: (layer, 0, nf + f)),
            pl.BlockSpec((None, tf, D), lambda i, f: (layer, f, 0)),
            pl.BlockSpec((1, D), lambda i, f: (0, 0)),
        ],
        out_specs=pl.BlockSpec((tm, D), lambda i, f: (i, 0), pipeline_mode=pl.Buffered(1)),
        out_shape=jax.ShapeDtypeStruct((n_rows, D), F32),
        scratch_shapes=[
            pltpu.VMEM((tm + 2 * HALO, D), BF16),
            pltpu.VMEM((tm, LANES), F32),
        ] + [
            pltpu.VMEM((tm + 2 * HALO, FFN_SLAB), F32),
            pltpu.VMEM((tm + 2 * HALO, FFN_SLAB), F32),
        ] * (tf // FFN_SLAB) + [
            pltpu.VMEM((tm, tf), BF16),
        ],
        compiler_params=_params("arbitrary", "arbitrary", vmem=VMEM_LIMIT_FFN),
        name=name,
    )(xs, xs, xs, nw, mod, mod, mod, w_up, w_up, conv_w, conv_w, conv_b, conv_b, w_down, fnw)


ATT_TQ = 1024
ATT_UNIT = 128


def _attn_kernel(q_ref, kx_ref, kc_ref, vx_ref, vc_ref, o_ref):
    kx, kc, vx, vc = kx_ref[...], kc_ref[...], vx_ref[...], vc_ref[...]
    for u in range(ATT_TQ // ATT_UNIT):
        r0 = u * ATT_UNIT
        q = q_ref[r0:r0 + ATT_UNIT, :]
        qs = jnp.concatenate([q[:, j * ATT_HD:(j + 1) * ATT_HD] for j in range(ATT_GROUP)], axis=0)
        sx = lax.dot_general(qs, kx, NT_DIMS, preferred_element_type=F32)
        sc = lax.dot_general(qs, kc, NT_DIMS, preferred_element_type=F32)
        m = jnp.maximum(jnp.max(sx, axis=-1, keepdims=True), jnp.max(sc, axis=-1, keepdims=True))
        px = jnp.exp2(sx - m)
        pc = jnp.exp2(sc - m)
        denom = jnp.sum(px, axis=-1, keepdims=True) + jnp.sum(pc, axis=-1, keepdims=True)
        out = jnp.dot(px.astype(BF16), vx, preferred_element_type=F32)
        out = out + jnp.dot(pc.astype(BF16), vc, preferred_element_type=F32)
        out = out * (1.0 / denom)
        for j in range(ATT_GROUP):
            o_ref[r0:r0 + ATT_UNIT, j * ATT_HD:(j + 1) * ATT_HD] = (
                out[j * ATT_UNIT:(j + 1) * ATT_UNIT, :].astype(BF16))


def _attention(qkv_x, kv_c):
    tq = ATT_TQ
    qb = SEQ // tq
    k0 = D // ATT_HD
    v0 = k0 + ATT_KV
    return pl.pallas_call(
        _attn_kernel,
        grid=(B, ATT_KV, qb),
        in_specs=[
            pl.BlockSpec((tq, ATT_GROUP * ATT_HD), lambda b, g, t: (b * qb + t, g)),
            pl.BlockSpec((SEQ, ATT_HD), lambda b, g, t: (b, k0 + g)),
            pl.BlockSpec((CTX, ATT_HD), lambda b, g, t: (b, g)),
            pl.BlockSpec((SEQ, ATT_HD), lambda b, g, t: (b, v0 + g)),
            pl.BlockSpec((CTX, ATT_HD), lambda b, g, t: (b, ATT_KV + g)),
        ],
        out_specs=pl.BlockSpec((tq, ATT_GROUP * ATT_HD), lambda b, g, t: (b * qb + t, g)),
        out_shape=jax.ShapeDtypeStruct((NX, D), BF16),
        compiler_params=_params("arbitrary", "arbitrary", "arbitrary"),
        name="gqa_attention",
    )(qkv_x, qkv_x, kv_c, qkv_x, kv_c)


def _rope_tables(head_dim):
    rows = SEQ // GRID_W
    row = jnp.repeat(jnp.arange(rows, dtype=F32), GRID_W)
    col = jnp.tile(jnp.arange(GRID_W, dtype=F32), rows)
    n_freq = head_dim // 4
    inv = ROPE_THETA ** (-jnp.arange(n_freq, dtype=F32) / n_freq)
    ang = jnp.concatenate([row[:, None] * inv, col[:, None] * inv], axis=-1)
    return jnp.repeat(jnp.cos(ang), 2, axis=-1), jnp.repeat(jnp.sin(ang), 2, axis=-1)


def kernel(x, c, ctx, c_ctx, ada_w, ada_b, norm_w, ret_w_in, ret_w_out, ret_log_decay, ret_gn_w,
           attn_w_in, attn_w_out, attn_q_norm, attn_k_norm, ffn_w_up, ffn_conv_w, ffn_conv_b,
           ffn_w_down, final_norm_w):
    xx = x.reshape(NX, D)
    xc = ctx.reshape(NC, D)
    cmat = jnp.concatenate([c, c_ctx[None, :], jnp.zeros((MOD_ROWS - B - 1, D), F32)], axis=0)
    mod = _modulation(cmat, ada_w, ada_b)
    mod0 = mod[0].reshape(MOD_ROWS, 1, 6 * D)
    mod1 = mod[1].reshape(MOD_ROWS, 1, 6 * D)
    fnw = final_norm_w.reshape(1, D)

    nw = norm_w[0, 0].reshape(1, D)
    tabs = _rope_tables(RET_DK)
    qkvg_x = _in_proj(_RetInEpilogue, "ret_in_x", xx, False, 0, 6 * D, RET_DK, nw, mod0, ret_w_in[0], tabs, [])
    qkvg_c = _in_proj(_RetInEpilogue, "ret_in_ctx", xc, True, 0, 6 * D, RET_DK, nw, mod0, ret_w_in[0], tabs, [])
    yx, yc = _ret_scan(qkvg_x, qkvg_c, ret_log_decay[0], ret_gn_w[0].reshape(1, 2 * D))
    xx = _proj_res("ret_out_x", yx, ret_w_out[0], xx, mod0, False)
    xc = _proj_res("ret_out_ctx", yc, ret_w_out[0], xc, mod0, True)
    nw = norm_w[0, 1].reshape(1, D)
    ffn = (ffn_w_up, ffn_conv_w, ffn_conv_b.reshape(-1, 1, 2 * FFN), ffn_w_down)
    xx = _conv_ffn("conv_ffn_x", xx, nw, mod0, 0, *ffn, fnw, False, False)
    xc = _conv_ffn("conv_ffn_ctx", xc, nw, mod0, 0, *ffn, fnw, True, False)

    nw = norm_w[1, 0].reshape(1, D)
    tabs = _rope_tables(ATT_HD)
    heads = [attn_q_norm[0].reshape(1, ATT_HD), attn_k_norm[0].reshape(1, ATT_HD)]
    qkv_x = _in_proj(_AttnInEpilogue, "attn_in_x", xx, False, 0, ATT_IN, ATT_HD, nw, mod1, attn_w_in[0],
                     tabs, heads)
    kv_c = _in_proj(_AttnInEpilogue, "attn_in_ctx", xc, True, ATT_Q_TILES, ATT_IN - D, ATT_HD, nw, mod1,
                    attn_w_in[0], tabs, heads)
    ya = _attention(qkv_x, kv_c)
    xx = _proj_res("attn_out", ya, attn_w_out[0], xx, mod1, False)
    out = _conv_ffn("conv_ffn_out", xx, norm_w[1, 1].reshape(1, D), mod1, 1, *ffn, fnw, False, True)
    return out.reshape(B, SEQ, D)
```
